```python
import jax, jax.numpy as jnp
from jax import lax
import numpy as np

D_MODEL = 4096
BATCH = 4
SEQ = 2048
DEPTH = 2
DEC_BATCH = 8
DEC_SEQ = 1
PAST_LEN = 16384
PAGE_SIZE = 128

H_A = 4
DK_A = 256
DV_A = 512
GATE_RANK = 16
GATE_TAU = 16.0
CHUNK_A = 16
H_B = 8
DK_B = 128
DV_B = 256
CHUNK_B = 64
ROPE_BASE = 10000.0
WINDOWS = (128, 512, 2048)
DILATIONS = (1, 4, 16)
N_GROUPS_C = 3
G_C = 4
HD_C = 128
NH_C = N_GROUPS_C * G_C
BAND = 128
QK_A = H_A * DK_A
V_A = H_A * DV_A
QK_B = H_B * DK_B
V_B = H_B * DV_B
V_C = G_C * HD_C
C_COLS = N_GROUPS_C * 3 * V_C
IN_SIZES = (QK_A, QK_A, V_A, V_A, GATE_RANK, QK_B, QK_B, V_B, V_B, C_COLS)
N_IN = 2 * QK_A + 2 * V_A + GATE_RANK + 2 * QK_B + 2 * V_B + C_COLS
D_FF = -(-8 * D_MODEL // (3 * 256)) * 256
EPS = 1e-6
NEG = -1e30

kernel_name = 'hybrid_gla_retention_dilated_step'


def rms(x):
    x32 = x.astype(jnp.float32)
    return x32 * lax.rsqrt(jnp.mean(x32 * x32, axis=-1, keepdims=True) + EPS)


def rmsnorm(x, g):
    return (rms(x) * g.astype(jnp.float32)).astype(x.dtype)


def split_columns(h):
    parts, start = [], 0
    for n in IN_SIZES:
        parts.append(h[..., start:start + n])
        start += n
    return parts


def rotary(x, pos):
    half = x.shape[-1] // 2
    inv = ROPE_BASE ** (-jnp.arange(half, dtype=jnp.float32) / half)
    ang = pos.astype(jnp.float32)[:, None] * inv[None, :]
    cos = jnp.cos(ang)[None, :, None, :]
    sin = jnp.sin(ang)[None, :, None, :]
    x32 = x.astype(jnp.float32)
    x1, x2 = x32[..., :half], x32[..., half:]
    return jnp.concatenate([x1 * cos - x2 * sin, x1 * sin + x2 * cos], axis=-1).astype(x.dtype)


def alibi_slopes():
    h = jnp.arange(1, NH_C + 1, dtype=jnp.float32)
    return (2.0 ** (-8.0 * h / NH_C)).reshape(N_GROUPS_C, G_C)


def chunk_gated_recurrence(q, k, v, log_decay, state0, chunk):
    B, T, H, DK = q.shape
    DV = v.shape[-1]
    X = log_decay.shape[-1]
    c = min(chunk, T)
    pad = (-T) % c
    f32 = jnp.float32
    arrs = [a.astype(f32) for a in (q, k, v, log_decay)]
    if pad:
        arrs = [jnp.pad(a, ((0, 0), (0, pad), (0, 0), (0, 0))) for a in arrs]
    n = (T + pad) // c
    qc, kc, vc, gc = [a.reshape(B, n, c, H, a.shape[-1]).transpose(1, 0, 2, 3, 4) for a in arrs]
    causal = jnp.tril(jnp.ones((c, c), dtype=bool))[None, :, :, None, None]

    def step(S, inp):
        qi, ki, vi, gi = inp
        b = jnp.cumsum(gi, axis=1)
        b_last = b[:, -1]
        rel = jnp.exp(jnp.where(causal, b[:, :, None] - b[:, None, :], NEG))
        if X == 1:
            scores = jnp.einsum('bthk,bshk->btsh', qi, ki) * rel[..., 0]
        else:
            scores = jnp.einsum('bthk,bshk,btshk->btsh', qi, ki, rel)
        o = jnp.einsum('btsh,bshv->bthv', scores, vi) + jnp.einsum('bthk,bhkv->bthv', qi * jnp.exp(b), S)
        S = jnp.exp(b_last)[..., None] * S + jnp.einsum('bshk,bshv->bhkv', ki * jnp.exp(b_last[:, None] - b), vi)
        return S, o

    S, o = lax.scan(step, state0.astype(f32), (qc, kc, vc, gc))
    o = o.transpose(1, 0, 2, 3, 4).reshape(B, n * c, H, DV)[:, :T]
    return o.astype(v.dtype), S.astype(state0.dtype)


def band_attention(q, k, v, dil, slopes):
    N, L, G, HD = q.shape
    QB = BAND
    nb = -(-L // QB)
    qpad = nb * QB - L
    kpad = qpad + QB
    qp = jnp.pad(q, ((0, 0), (qpad, 0), (0, 0), (0, 0))).reshape(N, nb, QB, G, HD)

    def key_blocks(a):
        ap = jnp.pad(a, ((0, 0), (kpad, 0), (0, 0), (0, 0))).reshape(N, nb + 1, QB, G, HD)
        return jnp.concatenate([ap[:, :-1], ap[:, 1:]], axis=2)

    kb, vb = key_blocks(k), key_blocks(v)
    u = jnp.arange(QB)[:, None]
    c = jnp.arange(2 * QB)[None, :]
    j = QB + u - c
    src = jnp.arange(nb)[:, None, None] * QB + c[None] - kpad
    valid = (j >= 0) & (j <= BAND) & (src >= 0)
    s = jnp.einsum('nbqgd,nbkgd->nbgqk', qp, kb).astype(jnp.float32) * (HD ** -0.5)
    s = s - slopes[:, None, None] * (dil * j).astype(jnp.float32)
    s = jnp.where(valid[None, :, None], s, NEG)
    m = jnp.max(s, axis=-1, keepdims=True)
    p = jnp.exp(s - m)
    l = jnp.sum(p, axis=-1, keepdims=True)
    o = jnp.einsum('nbgqk,nbkgd->nbqgd', (p / l).astype(v.dtype), vb)
    lse = (m + jnp.log(l))[..., 0]
    o = o.reshape(N, nb * QB, G, HD)[:, qpad:]
    lse = lse.transpose(0, 1, 3, 2).reshape(N, nb * QB, G)[:, qpad:]
    return o, lse


def dilated_prefill(q, k, v, dil, slopes):
    B, S, G, HD = q.shape
    L = S // dil

    def sub(a):
        return a.reshape(B, L, dil, G, HD).transpose(0, 2, 1, 3, 4).reshape(B * dil, L, G, HD)

    o, lse = band_attention(sub(q), sub(k), sub(v), dil, slopes)
    o = o.reshape(B, dil, L, G, HD).transpose(0, 2, 1, 3, 4).reshape(B, S, G, HD)
    lse = lse.reshape(B, dil, L, G).transpose(0, 2, 1, 3).reshape(B, S, G)
    return o, lse


def dilated_step(q, k, v, buf, dil, window, slopes):
    B, T, G, HD = q.shape
    L = buf.shape[2]
    k_ext = jnp.concatenate([buf[:, 0].astype(k.dtype), k], axis=1)
    v_ext = jnp.concatenate([buf[:, 1].astype(v.dtype), v], axis=1)
    j = jnp.arange(BAND + 1)
    idx = L + jnp.arange(T)[:, None] - dil * j[None, :]
    valid = idx >= 0
    idx = jnp.maximum(idx, 0)
    kg = k_ext[:, idx]
    vg = v_ext[:, idx]
    s = jnp.einsum('btgd,btjgd->btgj', q, kg).astype(jnp.float32) * (HD ** -0.5)
    s = s - slopes[:, None] * (dil * j).astype(jnp.float32)[None, :]
    s = jnp.where(valid[None, :, None, :], s, NEG)
    m = jnp.max(s, axis=-1, keepdims=True)
    p = jnp.exp(s - m)
    l = jnp.sum(p, axis=-1, keepdims=True)
    o = jnp.einsum('btgj,btjgd->btgd', (p / l).astype(v.dtype), vg)
    lse = (m + jnp.log(l))[..., 0]
    new_len = min(window, L + T)
    new_buf = jnp.stack([k_ext[:, L + T - new_len:], v_ext[:, L + T - new_len:]], axis=1)
    return o, lse, new_buf


def dilated_branch(hc, win_bufs):
    B, T, _ = hc.shape
    hc = hc.reshape(B, T, N_GROUPS_C, 3, G_C, HD_C)
    slopes = alibi_slopes()
    outs, lses, bufs = [], [], []
    for g in range(N_GROUPS_C):
        q, k, v = hc[:, :, g, 0], hc[:, :, g, 1], hc[:, :, g, 2]
        if win_bufs is None:
            o, lse = dilated_prefill(q, k, v, DILATIONS[g], slopes[g])
            keep = min(WINDOWS[g], T)
            buf = jnp.stack([k[:, T - keep:], v[:, T - keep:]], axis=1)
        else:
            o, lse, buf = dilated_step(q, k, v, win_bufs[g], DILATIONS[g], WINDOWS[g], slopes[g])
        outs.append(o)
        lses.append(lse)
        bufs.append(buf)
    w = jax.nn.softmax(jnp.stack(lses, axis=-1), axis=-1)
    o = jnp.einsum('btgdn,btgn->btgd', jnp.stack(outs, axis=-1).astype(jnp.float32), w).astype(hc.dtype)
    return o.reshape(B, T, V_C), bufs


def trunk_layer(x, pos0, gla_state0, ret_state0, win_bufs, g_mix, w_in, w_alpha2, b_alpha, g_gla,
                w_merge, w_up_a, w_up_b, w_up_c, w_out, g_ffn, w_ffn_gate, w_ffn_up, w_ffn_down):
    B, T, _ = x.shape
    xn = rmsnorm(x, g_mix)
    qa, ka, va, ra, za, qb, kb, vb, gb, hc = split_columns(xn @ w_in)
    qa = qa.reshape(B, T, H_A, DK_A) * (DK_A ** -0.5)
    ka = ka.reshape(B, T, H_A, DK_A)
    va = va.reshape(B, T, H_A, DV_A)
    log_alpha = jax.nn.log_sigmoid((za @ w_alpha2 + b_alpha).astype(jnp.float32)) / GATE_TAU
    log_alpha = log_alpha.reshape(B, T, H_A, DK_A)
    oa, gla_state = chunk_gated_recurrence(qa, ka, va, log_alpha, gla_state0, CHUNK_A)
    oa = jax.nn.silu(ra) * rmsnorm(oa, g_gla).reshape(B, T, V_A)
    pos = pos0 + jnp.arange(T)
    qb = rotary(qb.reshape(B, T, H_B, DK_B), pos)
    kb = rotary(kb.reshape(B, T, H_B, DK_B), pos) * (DK_B ** -0.5)
    vb = vb.reshape(B, T, H_B, DV_B)
    log_gamma = jnp.log1p(-(2.0 ** (-5.0 - jnp.arange(H_B, dtype=jnp.float32))))
    log_decay = jnp.broadcast_to(log_gamma[None, None, :, None], (B, T, H_B, 1))
    ob, ret_state = chunk_gated_recurrence(qb, kb, vb, log_decay, ret_state0, CHUNK_B)
    ob = jax.nn.silu(gb) * rms(ob).astype(x.dtype).reshape(B, T, V_B)
    oc, new_bufs = dilated_branch(hc, win_bufs)
    gate_a, gate_b, gate_c = jnp.split(jax.nn.sigmoid(xn @ w_merge), 3, axis=-1)
    merged = gate_a * (oa @ w_up_a) + gate_b * (ob @ w_up_b) + gate_c * (oc @ w_up_c)
    x = x + merged @ w_out
    xf = rmsnorm(x, g_ffn)
    x = x + (jax.nn.silu(xf @ w_ffn_gate) * (xf @ w_ffn_up)) @ w_ffn_down
    return x, gla_state, ret_state, new_bufs


def setup_inputs(seed: int = 0) -> dict:
    key = jax.random.key(seed)
    ks = jax.random.split(key, 24)
    f32 = jnp.float32

    def nrm(k, shape, scale):
        return jax.random.normal(k, shape, f32) * scale

    def gain(k, shape):
        return 1.0 + 0.01 * jax.random.normal(k, shape, f32)

    return {
        'x_prompt': nrm(ks[0], (BATCH, SEQ, D_MODEL), 1.0),
        'x_sample': nrm(ks[1], (DEC_BATCH, DEC_SEQ, D_MODEL), 1.0),
        'state_gla': nrm(ks[2], (DEPTH, DEC_BATCH, H_A, DK_A, DV_A), 0.5),
        'state_ret': nrm(ks[3], (DEPTH, DEC_BATCH, H_B, DK_B, DV_B), 0.5),
        'cache_win0': nrm(ks[4], (DEPTH, DEC_BATCH, 2, min(WINDOWS[0], PAST_LEN), G_C, HD_C), 1.0),
        'cache_win1': nrm(ks[5], (DEPTH, DEC_BATCH, 2, min(WINDOWS[1], PAST_LEN), G_C, HD_C), 1.0),
        'cache_win2': nrm(ks[6], (DEPTH, DEC_BATCH, 2, min(WINDOWS[2], PAST_LEN), G_C, HD_C), 1.0),
        'g_mix': gain(ks[7], (DEPTH, D_MODEL)),
        'w_in': nrm(ks[8], (DEPTH, D_MODEL, N_IN), D_MODEL ** -0.5),
        'w_alpha2': nrm(ks[9], (DEPTH, GATE_RANK, QK_A), GATE_RANK ** -0.5),
        'b_alpha': nrm(ks[10], (DEPTH, QK_A), 0.1),
        'g_gla': gain(ks[11], (DEPTH, DV_A)),
        'w_merge': nrm(ks[12], (DEPTH, D_MODEL, 3 * D_MODEL), D_MODEL ** -0.5),
        'w_up_a': nrm(ks[13], (DEPTH, V_A, D_MODEL), V_A ** -0.5),
        'w_up_b': nrm(ks[14], (DEPTH, V_B, D_MODEL), V_B ** -0.5),
        'w_up_c': nrm(ks[15], (DEPTH, V_C, D_MODEL), V_C ** -0.5),
        'w_out': nrm(ks[16], (DEPTH, D_MODEL, D_MODEL), D_MODEL ** -0.5),
        'g_ffn': gain(ks[17], (DEPTH, D_MODEL)),
        'w_ffn_gate': nrm(ks[18], (DEPTH, D_MODEL, D_FF), D_MODEL ** -0.5),
        'w_ffn_up': nrm(ks[19], (DEPTH, D_MODEL, D_FF), D_MODEL ** -0.5),
        'w_ffn_down': nrm(ks[20], (DEPTH, D_FF, D_MODEL), D_FF ** -0.5),
        'g_final': gain(ks[21], (D_MODEL,)),
    }


def reference(x_prompt, x_sample, state_gla, state_ret, cache_win0, cache_win1, cache_win2,
              g_mix, w_in, w_alpha2, b_alpha, g_gla, w_merge, w_up_a, w_up_b, w_up_c, w_out,
              g_ffn, w_ffn_gate, w_ffn_up, w_ffn_down, g_final):
    xp, xs = x_prompt, x_sample
    Bp = x_prompt.shape[0]
    gla_p, gla_s, ret_p, ret_s = [], [], [], []
    win_p = [[], [], []]
    win_s = [[], [], []]
    for l in range(DEPTH):
        weights = (g_mix[l], w_in[l], w_alpha2[l], b_alpha[l], g_gla[l], w_merge[l], w_up_a[l], w_up_b[l],
                   w_up_c[l], w_out[l], g_ffn[l], w_ffn_gate[l], w_ffn_up[l], w_ffn_down[l])
        zeros_a = jnp.zeros((Bp, H_A, DK_A, DV_A), xp.dtype)
        zeros_b = jnp.zeros((Bp, H_B, DK_B, DV_B), xp.dtype)
        xp, sa, sb, bufs_p = trunk_layer(xp, 0, zeros_a, zeros_b, None, *weights)
        xs, sa2, sb2, bufs_s = trunk_layer(xs, PAST_LEN, state_gla[l], state_ret[l],
                                           (cache_win0[l], cache_win1[l], cache_win2[l]), *weights)
        gla_p.append(sa)
        gla_s.append(sa2)
        ret_p.append(sb)
        ret_s.append(sb2)
        for g in range(N_GROUPS_C):
            win_p[g].append(bufs_p[g])
            win_s[g].append(bufs_s[g])
    y_prompt = rmsnorm(xp, g_final)
    y_sample = rmsnorm(xs, g_final)
    return (y_prompt, y_sample, jnp.stack(gla_p), jnp.stack(gla_s), jnp.stack(ret_p), jnp.stack(ret_s),
            jnp.stack(win_p[0]), jnp.stack(win_s[0]), jnp.stack(win_p[1]), jnp.stack(win_s[1]),
            jnp.stack(win_p[2]), jnp.stack(win_s[2]))
```

```python
import functools

import numpy as np
import jax
import jax.numpy as jnp
from jax import lax
from jax.experimental import pallas as pl
from jax.experimental.pallas import tpu as pltpu

F32 = jnp.float32
BF16 = jnp.bfloat16
HIGHEST = lax.Precision.HIGHEST

D_MODEL = 4096
H_A, DK_A, DV_A = 4, 256, 512
GATE_RANK, GATE_TAU = 16, 16.0
H_B, DK_B, DV_B = 8, 128, 256
ROPE_BASE = 10000.0
DILATIONS = (1, 4, 16)
N_GROUPS_C, G_C, HD_C = 3, 4, 128
BAND = 128
PAST_LEN = 16384
QK_A, V_A = H_A * DK_A, H_A * DV_A
QK_B, V_B = H_B * DK_B, H_B * DV_B
V_C = G_C * HD_C
C_COLS = N_GROUPS_C * 3 * V_C
EPS = 1e-6
NEG = -1e30

OFF_QA, OFF_KA, OFF_VA, OFF_RA = 0, QK_A, 2 * QK_A, 2 * QK_A + V_A
OFF_B = 2 * QK_A + 2 * V_A
OFF_QB, OFF_KB, OFF_VB, OFF_GB = OFF_B, OFF_B + QK_B, OFF_B + 2 * QK_B, OFF_B + 2 * QK_B + V_B
OFF_C = OFF_B + 2 * QK_B + 2 * V_B
N_MAIN = OFF_C + C_COLS

LANES = 128
SEQ_PAD = 16
GLA_SUB = 16
GLA_ROWS = 128
RET_ROWS = 256
VMEM_CAP = 56 * 1024 * 1024


def _vmem(nbytes):
    return int(min(VMEM_CAP, nbytes + (6 << 20)))


def _params(sem, nbytes):
    return pltpu.CompilerParams(dimension_semantics=sem, vmem_limit_bytes=_vmem(nbytes))


def _pick_tile(m, cap, align=16):
    best = align
    for t in range(align, cap + 1, align):
        if m % t == 0:
            best = t
    assert m % best == 0, (m, cap, align)
    return best


def _iota(shape, dim):
    return lax.broadcasted_iota(jnp.int32, shape, dim)


def _dot(a, b):
    return jnp.dot(a, b, preferred_element_type=F32)


def _dot_nt(a, b):
    return lax.dot_general(a, b, (((1,), (1,)), ((), ())), preferred_element_type=F32)


def _dot_tn(a, b):
    return lax.dot_general(a, b, (((0,), (0,)), ((), ())), preferred_element_type=F32)


def _sigmoid(x):
    return 1.0 / (1.0 + jnp.exp(-x))


def _rmsnorm_body(x_ref, g_ref, o_ref):
    x = x_ref[...]
    ms = jnp.mean(x * x, axis=-1, keepdims=True)
    o_ref[...] = (x * lax.rsqrt(ms + EPS) * g_ref[...]).astype(o_ref.dtype)


def _rmsnorm(x, g, out_dtype):
    m, d = x.shape
    tm = _pick_tile(m, 512)
    nbytes = 2 * tm * d * (4 + jnp.dtype(out_dtype).itemsize) + 2 * tm * d * 4
    return pl.pallas_call(
        _rmsnorm_body,
        out_shape=jax.ShapeDtypeStruct((m, d), out_dtype),
        grid=(m // tm,),
        in_specs=[pl.BlockSpec((tm, d), lambda i: (i, 0)), pl.BlockSpec((1, d), lambda i: (0, 0))],
        out_specs=pl.BlockSpec((tm, d), lambda i: (i, 0)),
        compiler_params=_params(("parallel",), nbytes),
        name="rmsnorm",
    )(x, g.reshape(1, d))


def _mm_body(x_ref, w_ref, o_ref):
    o_ref[...] = _dot(x_ref[...], w_ref[...]).astype(o_ref.dtype)


def _mm_res_body(x_ref, w_ref, r_ref, o_ref):
    o_ref[...] = r_ref[...] + _dot(x_ref[...], w_ref[...])


def _matmul(x, w, *, tm, tn, out_dtype=F32, residual=None, name="matmul"):
    m, k = x.shape
    n = w.shape[1]
    assert m % tm == 0 and n % tn == 0
    in_specs = [pl.BlockSpec((tm, k), lambda j, i: (i, 0)), pl.BlockSpec((k, tn), lambda j, i: (0, j))]
    args = [x, w]
    body = _mm_body
    nbytes = 2 * (tm * k * 2 + k * tn * 2 + tm * tn * jnp.dtype(out_dtype).itemsize)
    if residual is not None:
        in_specs.append(pl.BlockSpec((tm, tn), lambda j, i: (i, j)))
        args.append(residual)
        body = _mm_res_body
        nbytes += 2 * tm * tn * 4
    return pl.pallas_call(
        body,
        out_shape=jax.ShapeDtypeStruct((m, n), out_dtype),
        grid=(n // tn, m // tm),
        in_specs=in_specs,
        out_specs=pl.BlockSpec((tm, tn), lambda j, i: (i, j)),
        compiler_params=_params(("parallel", "parallel"), nbytes),
        name=name,
    )(*args)


def _mm_acc_res_body(x_ref, w_ref, r_ref, o_ref, acc_ref, *, n_k):
    kk = pl.program_id(2)

    @pl.when(kk == 0)
    def _():
        acc_ref[...] = r_ref[...]

    acc_ref[...] += _dot(x_ref[...], w_ref[...])

    @pl.when(kk == n_k - 1)
    def _():
        o_ref[...] = acc_ref[...]


def _matmul_ksplit_res(x, w, residual, *, tm, tn, tk, name):
    m, k = x.shape
    n = w.shape[1]
    assert m % tm == 0 and n % tn == 0 and k % tk == 0
    n_k = k // tk
    nbytes = 2 * (tm * tk * 2 + tk * tn * 2 + 2 * tm * tn * 4) + tm * tn * 4
    return pl.pallas_call(
        functools.partial(_mm_acc_res_body, n_k=n_k),
        out_shape=jax.ShapeDtypeStruct((m, n), F32),
        grid=(n // tn, m // tm, n_k),
        in_specs=[
            pl.BlockSpec((tm, tk), lambda j, i, kk: (i, kk)),
            pl.BlockSpec((tk, tn), lambda j, i, kk: (kk, j)),
            pl.BlockSpec((tm, tn), lambda j, i, kk: (i, j)),
        ],
        out_specs=pl.BlockSpec((tm, tn), lambda j, i, kk: (i, j)),
        scratch_shapes=[pltpu.VMEM((tm, tn), F32)],
        compiler_params=_params(("parallel", "parallel", "arbitrary"), nbytes),
        name=name,
    )(x, w, residual)


def _merge_body(xn_ref, oa_ref, ob_ref, oc_ref, wma_ref, wmb_ref, wmc_ref, wua_ref, wub_ref, wuc_ref, o_ref):
    xn = xn_ref[...]
    acc = _sigmoid(_dot(xn, wma_ref[...])) * _dot(oa_ref[...], wua_ref[...])
    acc += _sigmoid(_dot(xn, wmb_ref[...])) * _dot(ob_ref[...], wub_ref[...])
    acc += _sigmoid(_dot(xn, wmc_ref[...])) * _dot(oc_ref[...], wuc_ref[...])
    o_ref[...] = acc.astype(o_ref.dtype)


def _merge(xn, oa, ob, oc, w_merge, w_up_a, w_up_b, w_up_c, *, tm, tn):
    m, d = xn.shape
    nt = d // tn

    def row(width):
        return pl.BlockSpec((tm, width), lambda j, i: (i, 0))

    def col(kdim, shift):
        return pl.BlockSpec((kdim, tn), lambda j, i: (0, j + shift))

    kk = 3 * d + V_A + V_B + V_C
    nbytes = 2 * (tm * (d + V_A + V_B + V_C) * 2 + kk * tn * 2 + tm * tn * 2) + 6 * tm * tn * 4
    return pl.pallas_call(
        _merge_body,
        out_shape=jax.ShapeDtypeStruct((m, d), BF16),
        grid=(nt, m // tm),
        in_specs=[row(d), row(V_A), row(V_B), row(V_C), col(d, 0), col(d, nt), col(d, 2 * nt),
                  col(V_A, 0), col(V_B, 0), col(V_C, 0)],
        out_specs=pl.BlockSpec((tm, tn), lambda j, i: (i, j)),
        compiler_params=_params(("parallel", "parallel"), nbytes),
        name="merge",
    )(xn, oa, ob, oc, w_merge, w_merge, w_merge, w_up_a, w_up_b, w_up_c)


def _ffn_up_body(x_ref, wg_ref, wu_ref, o_ref):
    x = x_ref[...]
    a = _dot(x, wg_ref[...])
    o_ref[...] = (a * _sigmoid(a) * _dot(x, wu_ref[...])).astype(o_ref.dtype)


def _ffn_up(x, wg, wu, *, tm, tn):
    m, k = x.shape
    n = wg.shape[1]
    assert m % tm == 0 and n % tn == 0
    nbytes = 2 * (tm * k * 2 + 2 * k * tn * 2 + tm * tn * 2) + 3 * tm * tn * 4
    return pl.pallas_call(
        _ffn_up_body,
        out_shape=jax.ShapeDtypeStruct((m, n), BF16),
        grid=(m // tm, n // tn),
        in_specs=[pl.BlockSpec((tm, k), lambda i, j: (i, 0)), pl.BlockSpec((k, tn), lambda i, j: (0, j)),
                  pl.BlockSpec((k, tn), lambda i, j: (0, j))],
        out_specs=pl.BlockSpec((tm, tn), lambda i, j: (i, j)),
        compiler_params=_params(("parallel", "parallel"), nbytes),
        name="ffn_up",
    )(x, wg, wu)


def _gla_body(*refs, rows, n_t, t_valid, has_s0, has_alias):
    it = iter(refs)
    q_ref, k_ref, v_ref, r_ref, za_ref, wa_ref, ba_ref, g_ref = [next(it) for _ in range(8)]
    s0_ref = next(it) if has_s0 else None
    if has_alias:
        next(it)
    o_ref, sout_ref, st_scr, b_scr, a_scr = [next(it) for _ in range(5)]
    t = pl.program_id(1)

    @pl.when(t == 0)
    def _():
        if has_s0:
            st_scr[...] = s0_ref[...].T
        else:
            st_scr[...] = jnp.zeros_like(st_scr)

    z = jnp.dot(za_ref[...], wa_ref[...], precision=HIGHEST, preferred_element_type=F32) + ba_ref[...]
    la = (jnp.minimum(z, 0.0) - jnp.log1p(jnp.exp(-jnp.abs(z)))) * (1.0 / GATE_TAU)
    row_k = _iota((rows, DK_A), 0)
    if t_valid is not None:
        la = jnp.where(row_k < t_valid, la, 0.0)
    tri = (_iota((rows, rows), 0) >= _iota((rows, rows), 1)).astype(F32)
    b = jnp.dot(tri, la, precision=HIGHEST, preferred_element_type=F32)
    b_scr[...] = b

    q = q_ref[...] * (DK_A ** -0.5)
    k = k_ref[...]
    v16 = v_ref[...].astype(BF16)
    st = st_scr[...]
    o = _dot_nt((q * jnp.exp(b)).astype(BF16), st.astype(BF16))

    row_s = _iota((GLA_SUB, DK_A), 0)
    lane_r = _iota((GLA_SUB, rows), 1)
    for i in range(rows // GLA_SUB):
        r0 = GLA_SUB * i
        qi = q[r0:r0 + GLA_SUB]
        bi = b[r0:r0 + GLA_SUB]
        if i > 0:
            b_in = b_scr[pl.ds(r0 - 1, 1), :]
            qs = qi * jnp.exp(bi - b_in)
            ks = k * jnp.exp(jnp.where(row_k < r0, b_in - b, NEG))
            a_i = _dot_nt(qs.astype(BF16), ks.astype(BF16))
        else:
            a_i = jnp.zeros((GLA_SUB, rows), F32)
        for s in range(GLA_SUB):
            k_s = k_ref[pl.ds(r0 + s, 1), :]
            b_s = b_scr[pl.ds(r0 + s, 1), :]
            e = jnp.exp(jnp.where(row_s >= s, bi - b_s, NEG))
            col = jnp.sum(qi * k_s * e, axis=-1, keepdims=True)
            a_i = jnp.where(lane_r == r0 + s, col, a_i)
        a_scr[r0:r0 + GLA_SUB, :] = a_i
    o = o + _dot(a_scr[...].astype(BF16), v16)

    b_last = b_scr[pl.ds(rows - 1, 1), :]
    k_st = k if t_valid is None else jnp.where(row_k < t_valid, k, 0.0)
    st_new = st * jnp.exp(b_last) + _dot_tn(v16, (k_st * jnp.exp(b_last - b)).astype(BF16))
    st_scr[...] = st_new

    @pl.when(t == n_t - 1)
    def _():
        sout_ref[...] = st_new.T

    ms = jnp.mean(o * o, axis=-1, keepdims=True)
    r = r_ref[...]
    o_ref[...] = (r * _sigmoid(r) * (o * lax.rsqrt(ms + EPS) * g_ref[...])).astype(o_ref.dtype)


def _gla(h, za, wa_pad, b_alpha, g_gla, *, n_seq, seq_rows, rows, row0, t_valid, state0, out_rows, alias):
    n_t = seq_rows // rows
    rb0 = row0 // rows

    def rmap(width_blk):
        return lambda s, t: (rb0 + (s // H_A) * n_t + t, width_blk(s % H_A))

    in_specs = [
        pl.BlockSpec((rows, DK_A), rmap(lambda hh: OFF_QA // DK_A + hh)),
        pl.BlockSpec((rows, DK_A), rmap(lambda hh: OFF_KA // DK_A + hh)),
        pl.BlockSpec((rows, DV_A), rmap(lambda hh: OFF_VA // DV_A + hh)),
        pl.BlockSpec((rows, DV_A), rmap(lambda hh: OFF_RA // DV_A + hh)),
        pl.BlockSpec((rows, LANES), rmap(lambda hh: 0)),
        pl.BlockSpec((LANES, DK_A), lambda s, t: (0, s % H_A)),
        pl.BlockSpec((1, DK_A), lambda s, t: (0, s % H_A)),
        pl.BlockSpec((1, DV_A), lambda s, t: (0, 0)),
    ]
    args = [h, h, h, h, za, wa_pad, b_alpha.reshape(1, QK_A), g_gla.reshape(1, DV_A)]
    if state0 is not None:
        in_specs.append(pl.BlockSpec((None, None, DK_A, DV_A), lambda s, t: (s // H_A, s % H_A, 0, 0)))
        args.append(state0)
    aliases = {}
    if alias is not None:
        in_specs.append(pl.BlockSpec(memory_space=pl.ANY))
        aliases = {len(args): 0}
        args.append(alias)
    nbytes = 2 * rows * (2 * DK_A + 2 * DV_A + LANES) * 4 + 2 * rows * DV_A * 2 + 5 * DK_A * DV_A * 4 \
        + rows * (DK_A + rows) * 4 + 24 * rows * DV_A * 4
    return pl.pallas_call(
        functools.partial(_gla_body, rows=rows, n_t=n_t, t_valid=t_valid, has_s0=state0 is not None,
                          has_alias=alias is not None),
        out_shape=(jax.ShapeDtypeStruct((out_rows, V_A), BF16), jax.ShapeDtypeStruct((n_seq, H_A, DK_A, DV_A), F32)),
        grid=(n_seq * H_A, n_t),
        in_specs=in_specs,
        out_specs=(pl.BlockSpec((rows, DV_A), rmap(lambda hh: hh)),
                   pl.BlockSpec((None, None, DK_A, DV_A), lambda s, t: (s // H_A, s % H_A, 0, 0))),
        scratch_shapes=[pltpu.VMEM((DV_A, DK_A), F32), pltpu.VMEM((rows, DK_A), F32), pltpu.VMEM((rows, rows), F32)],
        input_output_aliases=aliases,
        compiler_params=_params(("parallel", "arbitrary"), nbytes),
        name="gla",
    )(*args)


def _ret_body(*refs, rows, n_t, t_valid, has_s0, has_alias):
    it = iter(refs)
    q_ref, k_ref, v_ref, g_ref, cos_ref, sin_ref, lg_ref = [next(it) for _ in range(7)]
    s0_ref = next(it) if has_s0 else None
    if has_alias:
        next(it)
    o_ref, sout_ref, st_scr = [next(it) for _ in range(3)]
    t = pl.program_id(1)

    @pl.when(t == 0)
    def _():
        if has_s0:
            st_scr[...] = s0_ref[...]
        else:
            st_scr[...] = jnp.zeros_like(st_scr)

    lg = lg_ref[pl.ds(pl.program_id(0) % H_B, 1), :]
    cos = cos_ref[...]
    sin = sin_ref[...]

    def rot(x):
        return x * cos + pltpu.roll(x, DK_B // 2, 1) * sin

    n_valid = rows if t_valid is None else t_valid
    q = rot(q_ref[...])
    k = rot(k_ref[...]) * (DK_B ** -0.5)
    row_k = _iota((rows, DK_B), 0)
    if t_valid is not None:
        k = jnp.where(row_k < t_valid, k, 0.0)
    steps = jnp.minimum(row_k + 1, n_valid).astype(F32)
    lg_k = lg[:, :DK_B]
    decay_q = jnp.exp(lg_k * steps)
    decay_k = jnp.exp(lg_k * (n_valid - steps))
    st = jnp.minimum(_iota((rows, rows), 0) + 1, n_valid)
    ss = jnp.minimum(_iota((rows, rows), 1) + 1, n_valid)
    causal = _iota((rows, rows), 0) >= _iota((rows, rows), 1)
    rel = jnp.exp(jnp.where(causal, lg[:, :rows] * (st - ss).astype(F32), NEG))

    v16 = v_ref[...].astype(BF16)
    state = st_scr[...]
    scores = _dot_nt(q.astype(BF16), k.astype(BF16)) * rel
    o = _dot(scores.astype(BF16), v16) + _dot((q * decay_q).astype(BF16), state.astype(BF16))
    state_new = state * jnp.exp(lg[:, :DV_B] * float(n_valid)) + _dot_tn((k * decay_k).astype(BF16), v16)
    st_scr[...] = state_new

    @pl.when(t == n_t - 1)
    def _():
        sout_ref[...] = state_new

    ms = jnp.mean(o * o, axis=-1, keepdims=True)
    g = g_ref[...]
    o_ref[...] = (g * _sigmoid(g) * (o * lax.rsqrt(ms + EPS))).astype(o_ref.dtype)


def _ret(h, cos, sin, lg_tab, *, n_seq, seq_rows, rows, row0, t_valid, state0, out_rows, alias):
    n_t = seq_rows // rows
    rb0 = row0 // rows

    def rmap(width_blk):
        return lambda s, t: (rb0 + (s // H_B) * n_t + t, width_blk(s % H_B))

    in_specs = [
        pl.BlockSpec((rows, DK_B), rmap(lambda hh: OFF_QB // DK_B + hh)),
        pl.BlockSpec((rows, DK_B), rmap(lambda hh: OFF_KB // DK_B + hh)),
        pl.BlockSpec((rows, DV_B), rmap(lambda hh: OFF_VB // DV_B + hh)),
        pl.BlockSpec((rows, DV_B), rmap(lambda hh: OFF_GB // DV_B + hh)),
        pl.BlockSpec((rows, DK_B), lambda s, t: (t, 0)),
        pl.BlockSpec((rows, DK_B), lambda s, t: (t, 0)),
        pl.BlockSpec((H_B, RET_ROWS), lambda s, t: (0, 0)),
    ]
    args = [h, h, h, h, cos, sin, lg_tab]
    if state0 is not None:
        in_specs.append(pl.BlockSpec((None, None, DK_B, DV_B), lambda s, t: (s // H_B, s % H_B, 0, 0)))
        args.append(state0)
    aliases = {}
    if alias is not None:
        in_specs.append(pl.BlockSpec(memory_space=pl.ANY))
        aliases = {len(args): 0}
        args.append(alias)
    nbytes = 2 * rows * (4 * DK_B + 2 * DV_B) * 4 + 2 * rows * DV_B * 2 + 5 * DK_B * DV_B * 4 \
        + 6 * rows * rows * 4 + 8 * rows * DV_B * 4
    return pl.pallas_call(
        functools.partial(_ret_body, rows=rows, n_t=n_t, t_valid=t_valid, has_s0=state0 is not None,
                          has_alias=alias is not None),
        out_shape=(jax.ShapeDtypeStruct((out_rows, V_B), BF16), jax.ShapeDtypeStruct((n_seq, H_B, DK_B, DV_B), F32)),
        grid=(n_seq * H_B, n_t),
        in_specs=in_specs,
        out_specs=(pl.BlockSpec((rows, DV_B), rmap(lambda hh: hh)),
                   pl.BlockSpec((None, None, DK_B, DV_B), lambda s, t: (s // H_B, s % H_B, 0, 0))),
        scratch_shapes=[pltpu.VMEM((DK_B, DV_B), F32)],
        input_output_aliases=aliases,
        compiler_params=_params(("parallel", "arbitrary"), nbytes),
        name="retention",
    )(*args)


def _dil_prefill_body(slopes_ref, *refs, seq):
    qkv = refs[:9]
    o_ref, og_scr, lse_scr = refs[9:]
    slot = pl.program_id(1)
    scale = HD_C ** -0.5
    rr = _iota((BAND, BAND), 0)
    cc = _iota((BAND, BAND), 1)
    dist_cur = (rr - cc).astype(F32)
    ok_cur = rr >= cc
    ok_prev = cc >= rr
    n_units = seq // BAND
    for g, d in enumerate(DILATIONS):
        q_ref, k_ref, v_ref = qkv[3 * g:3 * g + 3]
        pen = slopes_ref[g * G_C + slot] * float(d)
        shift = d.bit_length() - 1

        def rows_of(start, d=d):
            return pl.ds(start, BAND) if d == 1 else pl.ds(start, BAND, stride=d)

        def unit(u, carry, q_ref=q_ref, k_ref=k_ref, v_ref=v_ref, pen=pen, d=d, shift=shift, rows_of=rows_of, g=g):
            blk = u >> shift
            start = (u & (d - 1)) + blk * (BAND * d)
            prev = jnp.maximum(start - BAND * d, 0)
            q = q_ref[rows_of(start), :].astype(BF16)
            s_cur = _dot_nt(q, k_ref[rows_of(start), :].astype(BF16)) * scale - pen * dist_cur
            s_cur = jnp.where(ok_cur, s_cur, NEG)
            s_prev = _dot_nt(q, k_ref[rows_of(prev), :].astype(BF16)) * scale - pen * (dist_cur + float(BAND))
            s_prev = jnp.where(ok_prev, s_prev + jnp.where(blk > 0, 0.0, NEG), NEG)
            m = jnp.maximum(jnp.max(s_cur, axis=-1, keepdims=True), jnp.max(s_prev, axis=-1, keepdims=True))
            p_cur = jnp.exp(s_cur - m)
            p_prev = jnp.exp(s_prev - m)
            l = jnp.sum(p_cur, axis=-1, keepdims=True) + jnp.sum(p_prev, axis=-1, keepdims=True)
            acc = _dot(p_cur.astype(BF16), v_ref[rows_of(start), :].astype(BF16))
            acc += _dot(p_prev.astype(BF16), v_ref[rows_of(prev), :].astype(BF16))
            og_scr[g, rows_of(start), :] = acc / l
            lse_scr[g, rows_of(start), :] = jnp.broadcast_to(m + jnp.log(l), (BAND, HD_C))
            return carry

        lax.fori_loop(0, n_units, unit, 0)

    lse = [lse_scr[g] for g in range(N_GROUPS_C)]
    top = jnp.maximum(jnp.maximum(lse[0], lse[1]), lse[2])
    w = [jnp.exp(x - top) for x in lse]
    total = w[0] + w[1] + w[2]
    o_ref[...] = ((w[0] * og_scr[0] + w[1] * og_scr[1] + w[2] * og_scr[2]) / total).astype(o_ref.dtype)


def _dil_prefill(h, slopes, *, n_seq, seq, out_rows):
    def cmap(g, j):
        return lambda b, s: (b, (OFF_C + g * 3 * V_C + j * V_C) // HD_C + s)

    in_specs = [pl.BlockSpec(memory_space=pltpu.SMEM)]
    in_specs += [pl.BlockSpec((seq, HD_C), cmap(g, j)) for g in range(N_GROUPS_C) for j in range(3)]
    nbytes = 2 * 9 * seq * HD_C * 4 + 2 * seq * HD_C * 2 + 2 * N_GROUPS_C * seq * HD_C * 4 + 8 * seq * HD_C * 4
    return pl.pallas_call(
        functools.partial(_dil_prefill_body, seq=seq),
        out_shape=jax.ShapeDtypeStruct((out_rows, V_C), BF16),
        grid=(n_seq, G_C),
        in_specs=in_specs,
        out_specs=pl.BlockSpec((seq, HD_C), lambda b, s: (b, s)),
        scratch_shapes=[pltpu.VMEM((N_GROUPS_C, seq, HD_C), F32), pltpu.VMEM((N_GROUPS_C, seq, HD_C), F32)],
        compiler_params=_params(("parallel", "parallel"), nbytes),
        name="dilated_prefill",
    )(slopes, *([h] * 9))


def _dil_step_body(slopes_ref, *refs):
    new = refs[:9]
    caches = refs[9:12]
    o_ref = refs[13]
    scale = HD_C ** -0.5
    for slot in range(G_C):
        lanes = slice(slot * HD_C, (slot + 1) * HD_C)
        scores, self_scores, values = [], [], []
        for g, d in enumerate(DILATIONS):
            q_ref, k_ref, v_ref = new[3 * g:3 * g + 3]
            c_ref = caches[g]
            length = c_ref.shape[1]
            q = q_ref[:, lanes]
            s = _dot_nt(q.astype(BF16), c_ref[0, :, lanes].astype(BF16)) * scale
            pos = _iota((SEQ_PAD, length), 1)
            pen = slopes_ref[g * G_C + slot] * (length - pos).astype(F32)
            s = jnp.where((pos & (d - 1)) == 0, s - pen, NEG)
            scores.append(s)
            self_scores.append(jnp.sum(q * k_ref[:, lanes], axis=-1, keepdims=True) * scale)
            values.append((c_ref, v_ref))
        m = self_scores[0]
        for x in self_scores[1:]:
            m = jnp.maximum(m, x)
        for x in scores:
            m = jnp.maximum(m, jnp.max(x, axis=-1, keepdims=True))
        acc = jnp.zeros((SEQ_PAD, HD_C), F32)
        total = jnp.zeros((SEQ_PAD, 1), F32)
        for s, s_self, (c_ref, v_ref) in zip(scores, self_scores, values):
            p = jnp.exp(s - m)
            p_self = jnp.exp(s_self - m)
            total += jnp.sum(p, axis=-1, keepdims=True) + p_self
            acc += _dot(p.astype(BF16), c_ref[1, :, lanes].astype(BF16)) + p_self * v_ref[:, lanes]
        o_ref[:, lanes] = (acc / total).astype(o_ref.dtype)


def _dil_step(h, slopes, caches, oc, *, n_seq, row0):
    rb0 = row0 // SEQ_PAD

    def cmap(g, j):
        return lambda b: (rb0 + b, (OFF_C + g * 3 * V_C + j * V_C) // V_C)

    in_specs = [pl.BlockSpec(memory_space=pltpu.SMEM)]
    in_specs += [pl.BlockSpec((SEQ_PAD, V_C), cmap(g, j)) for g in range(N_GROUPS_C) for j in range(3)]
    in_specs += [pl.BlockSpec((None, 2, c.shape[2], V_C), lambda b: (b, 0, 0, 0)) for c in caches]
    in_specs.append(pl.BlockSpec(memory_space=pl.ANY))
    nbytes = 2 * sum(2 * c.shape[2] * V_C * 4 for c in caches) + 16 * SEQ_PAD * caches[-1].shape[2] * 4
    return pl.pallas_call(
        _dil_step_body,
        out_shape=jax.ShapeDtypeStruct(oc.shape, oc.dtype),
        grid=(n_seq,),
        in_specs=in_specs,
        out_specs=pl.BlockSpec((SEQ_PAD, V_C), lambda b: (rb0 + b, 0)),
        input_output_aliases={13: 0},
        compiler_params=_params(("parallel",), nbytes),
        name="dilated_step",
    )(slopes, *([h] * 9), *caches, oc)


def _rope_tables(positions):
    half = DK_B // 2
    inv = ROPE_BASE ** (-np.arange(half, dtype=np.float64) / half)
    ang = np.asarray(positions, np.float64)[:, None] * inv[None, :]
    cos = np.concatenate([np.cos(ang), np.cos(ang)], axis=-1)
    sin = np.concatenate([-np.sin(ang), np.sin(ang)], axis=-1)
    return jnp.asarray(cos, F32), jnp.asarray(sin, F32)


def _layer(x, lw, state_gla, state_ret, caches, tabs, dims):
    n_p, seq, n_s, m_p, m = dims
    tm = _pick_tile(m, 704)
    xn = _rmsnorm(x, lw["g_mix"], BF16)
    h = _matmul(xn, lw["w_in"], tm=tm, tn=1536, name="in_proj")
    za = _matmul(xn, lw["w_za"], tm=tm, tn=LANES, name="gate_proj")

    gla = functools.partial(_gla, h, za, lw["w_alpha2"], lw["b_alpha"], lw["g_gla"], out_rows=m)
    oa, sa_p = gla(n_seq=n_p, seq_rows=seq, rows=GLA_ROWS, row0=0, t_valid=None, state0=None, alias=None)
    oa, sa_s = gla(n_seq=n_s, seq_rows=SEQ_PAD, rows=SEQ_PAD, row0=m_p, t_valid=1, state0=state_gla, alias=oa)

    ret = functools.partial(_ret, h, out_rows=m)
    ob, sb_p = ret(tabs["cos_p"], tabs["sin_p"], tabs["lg"], n_seq=n_p, seq_rows=seq, rows=RET_ROWS, row0=0,
                   t_valid=None, state0=None, alias=None)
    ob, sb_s = ret(tabs["cos_s"], tabs["sin_s"], tabs["lg"], n_seq=n_s, seq_rows=SEQ_PAD, rows=SEQ_PAD, row0=m_p,
                   t_valid=1, state0=state_ret, alias=ob)

    oc = _dil_prefill(h, tabs["slopes"], n_seq=n_p, seq=seq, out_rows=m)
    flat = [c.reshape(c.shape[0], 2, c.shape[2], V_C) for c in caches]
    oc = _dil_step(h, tabs["slopes"], flat, oc, n_seq=n_s, row0=m_p)

    merged = _merge(xn, oa, ob, oc, lw["w_merge"], lw["w_up_a"], lw["w_up_b"], lw["w_up_c"], tm=tm, tn=256)
    x = _matmul(merged, lw["w_out"], tm=tm, tn=1024, residual=x, name="out_proj")
    xf = _rmsnorm(x, lw["g_ffn"], BF16)
    hid = _ffn_up(xf, lw["w_ffn_gate"], lw["w_ffn_up"], tm=_pick_tile(m, 1728), tn=256)
    d_ff = hid.shape[1]
    x = _matmul_ksplit_res(hid, lw["w_ffn_down"], x, tm=tm, tn=512, tk=d_ff // 2, name="ffn_down")

    hp = h[:m_p].reshape(n_p, seq, N_MAIN)
    hs = h[m_p:].reshape(n_s, SEQ_PAD, N_MAIN)[:, :1]
    win_p, win_s = [], []
    for g in range(N_GROUPS_C):
        off_k = OFF_C + g * 3 * V_C + V_C
        keep = min(BAND * DILATIONS[g], seq)
        kv_p = jnp.stack([hp[:, seq - keep:, off_k:off_k + V_C], hp[:, seq - keep:, off_k + V_C:off_k + 2 * V_C]], axis=1)
        win_p.append(kv_p.reshape(n_p, 2, keep, G_C, HD_C))
        kv_s = jnp.stack([hs[:, :, off_k:off_k + V_C], hs[:, :, off_k + V_C:off_k + 2 * V_C]], axis=1)
        kv_s = kv_s.reshape(n_s, 2, 1, G_C, HD_C)
        win_s.append(jnp.concatenate([caches[g][:, :, 1:], kv_s], axis=2))
    return x, (sa_p, sa_s, sb_p, sb_s), win_p, win_s


def kernel(x_prompt, x_sample, state_gla, state_ret, cache_win0, cache_win1, cache_win2, g_mix, w_in, w_alpha2,
           b_alpha, g_gla, w_merge, w_up_a, w_up_b, w_up_c, w_out, g_ffn, w_ffn_gate, w_ffn_up, w_ffn_down, g_final):
    n_p, seq, d = x_prompt.shape
    n_s = x_sample.shape[0]
    depth = g_mix.shape[0]
    assert d == D_MODEL and x_sample.shape[1] == 1 and seq % (BAND * DILATIONS[-1]) == 0 and seq % RET_ROWS == 0
    m_p = n_p * seq
    m = m_p + n_s * SEQ_PAD
    dims = (n_p, seq, n_s, m_p, m)

    xs = jnp.pad(x_sample, ((0, 0), (0, SEQ_PAD - 1), (0, 0)))
    x = jnp.concatenate([x_prompt.reshape(m_p, d), xs.reshape(n_s * SEQ_PAD, d)], axis=0)

    cos_p, sin_p = _rope_tables(np.arange(seq))
    cos_s, sin_s = _rope_tables(PAST_LEN + np.arange(SEQ_PAD))
    log_gamma = np.log1p(-(2.0 ** (-5.0 - np.arange(H_B, dtype=np.float64))))
    heads = np.arange(1, N_GROUPS_C * G_C + 1, dtype=np.float64)
    tabs = {
        "cos_p": cos_p, "sin_p": sin_p, "cos_s": cos_s, "sin_s": sin_s,
        "lg": jnp.asarray(np.broadcast_to(log_gamma[:, None], (H_B, RET_ROWS)), F32),
        "slopes": jnp.asarray(2.0 ** (-8.0 * heads / (N_GROUPS_C * G_C)), F32),
    }

    za0 = 2 * QK_A + 2 * V_A
    caches = (cache_win0, cache_win1, cache_win2)
    outs = {k: [] for k in ("gla_p", "gla_s", "ret_p", "ret_s")}
    win_p = [[] for _ in range(N_GROUPS_C)]
    win_s = [[] for _ in range(N_GROUPS_C)]
    for l in range(depth):
        lw = {
            "g_mix": g_mix[l], "g_gla": g_gla[l], "g_ffn": g_ffn[l], "b_alpha": b_alpha[l],
            "w_in": jnp.concatenate([w_in[l, :, :za0], w_in[l, :, za0 + GATE_RANK:]], axis=1).astype(BF16),
            "w_za": jnp.pad(w_in[l, :, za0:za0 + GATE_RANK], ((0, 0), (0, LANES - GATE_RANK))).astype(BF16),
            "w_alpha2": jnp.pad(w_alpha2[l], ((0, LANES - GATE_RANK), (0, 0))),
            "w_merge": w_merge[l].astype(BF16), "w_up_a": w_up_a[l].astype(BF16), "w_up_b": w_up_b[l].astype(BF16),
            "w_up_c": w_up_c[l].astype(BF16), "w_out": w_out[l].astype(BF16),
            "w_ffn_gate": w_ffn_gate[l].astype(BF16), "w_ffn_up": w_ffn_up[l].astype(BF16),
            "w_ffn_down": w_ffn_down[l].astype(BF16),
        }
        x, (sa_p, sa_s, sb_p, sb_s), wp, ws = _layer(x, lw, state_gla[l], state_ret[l], [c[l] for c in caches], tabs, dims)
        outs["gla_p"].append(sa_p)
        outs["gla_s"].append(sa_s)
        outs["ret_p"].append(sb_p)
        outs["ret_s"].append(sb_s)
        for g in range(N_GROUPS_C):
            win_p[g].append(wp[g])
            win_s[g].append(ws[g])

    y = _rmsnorm(x, g_final, F32)
    y_prompt = y[:m_p].reshape(n_p, seq, d)
    y_sample = y[m_p:].reshape(n_s, SEQ_PAD, d)[:, :1]
    return (y_prompt, y_sample, jnp.stack(outs["gla_p"]), jnp.stack(outs["gla_s"]), jnp.stack(outs["ret_p"]),
            jnp.stack(outs["ret_s"]), jnp.stack(win_p[0]), jnp.stack(win_s[0]), jnp.stack(win_p[1]),
            jnp.stack(win_s[1]), jnp.stack(win_p[2]), jnp.stack(win_s[2]))
```

```python
import functools

import numpy as np
import jax
import jax.numpy as jnp
from jax import lax
from jax.experimental import pallas as pl
from jax.experimental.pallas import tpu as pltpu

F32 = jnp.float32
BF16 = jnp.bfloat16
HIGHEST = lax.Precision.HIGHEST

D_MODEL = 4096
H_A, DK_A, DV_A = 4, 256, 512
GATE_RANK, GATE_TAU = 16, 16.0
H_B, DK_B, DV_B = 8, 128, 256
ROPE_BASE = 10000.0
DILATIONS = (1, 4, 16)
N_GROUPS_C, G_C, HD_C = 3, 4, 128
BAND = 128
PAST_LEN = 16384
QK_A, V_A = H_A * DK_A, H_A * DV_A
QK_B, V_B = H_B * DK_B, H_B * DV_B
V_C = G_C * HD_C
C_COLS = N_GROUPS_C * 3 * V_C
EPS = 1e-6
NEG = -1e30

OFF_QA, OFF_KA, OFF_VA, OFF_RA = 0, QK_A, 2 * QK_A, 2 * QK_A + V_A
OFF_B = 2 * QK_A + 2 * V_A
OFF_QB, OFF_KB, OFF_VB, OFF_GB = OFF_B, OFF_B + QK_B, OFF_B + 2 * QK_B, OFF_B + 2 * QK_B + V_B
OFF_C = OFF_B + 2 * QK_B + 2 * V_B
N_MAIN = OFF_C + C_COLS

LANES = 128
SEQ_PAD = 16
GLA_ROWS = 128
RET_ROWS = 256
RET_HEADS = 4
CAST_ROWS = 512
VMEM_CAP = 56 * 1024 * 1024


def _vmem(nbytes):
    return int(min(VMEM_CAP, nbytes + (6 << 20)))


def _params(sem, nbytes):
    return pltpu.CompilerParams(dimension_semantics=sem, vmem_limit_bytes=_vmem(nbytes))


def _pick_tile(m, cap, align=16):
    best = align
    for t in range(align, cap + 1, align):
        if m % t == 0:
            best = t
    assert m % best == 0, (m, cap, align)
    return best


def _iota(shape, dim):
    return lax.broadcasted_iota(jnp.int32, shape, dim)


def _dot(a, b):
    return jnp.dot(a, b, preferred_element_type=F32)


def _dot_nt(a, b):
    return lax.dot_general(a, b, (((1,), (1,)), ((), ())), preferred_element_type=F32)


def _dot_tn(a, b):
    return lax.dot_general(a, b, (((0,), (0,)), ((), ())), preferred_element_type=F32)


def _sigmoid(x):
    return 1.0 / (1.0 + jnp.exp(-x))


def _rmsnorm_body(x_ref, g_ref, o_ref):
    x = x_ref[...]
    ms = jnp.mean(x * x, axis=-1, keepdims=True)
    o_ref[...] = (x * lax.rsqrt(ms + EPS) * g_ref[...]).astype(o_ref.dtype)


def _rmsnorm(x, g, out_dtype, rows=None):
    m, d = x.shape
    m = m if rows is None else rows
    tm = _pick_tile(m, 512)
    nbytes = 2 * tm * d * (4 + jnp.dtype(out_dtype).itemsize) + 2 * tm * d * 4
    return pl.pallas_call(
        _rmsnorm_body,
        out_shape=jax.ShapeDtypeStruct((m, d), out_dtype),
        grid=(m // tm,),
        in_specs=[pl.BlockSpec((tm, d), lambda i: (i, 0)), pl.BlockSpec((1, d), lambda i: (0, 0))],
        out_specs=pl.BlockSpec((tm, d), lambda i: (i, 0)),
        compiler_params=_params(("parallel",), nbytes),
        name="rmsnorm",
    )(x, g.reshape(1, d))


def _cast_panel(w_ref, wb_ref, nxt_ref=None, shift=0):
    k, tn = wb_ref.shape
    step = min(CAST_ROWS, k)

    def body(c, carry):
        rows = pl.ds(pl.multiple_of(c * step, step), step)
        w = w_ref[rows, :]
        if nxt_ref is not None:
            w = jnp.concatenate([w, nxt_ref[rows, :]], axis=1)[:, shift:shift + tn]
        wb_ref[rows, :] = w.astype(BF16)
        return carry

    lax.fori_loop(0, k // step, body, 0)


def _mm_ws_body(*refs, act, has_res):
    x_ref, w_ref = refs[:2]
    r_ref = refs[2] if has_res else None
    o_ref, wb_ref = refs[-2:]

    @pl.when(pl.program_id(1) == 0)
    def _():
        _cast_panel(w_ref, wb_ref)

    y = _dot(x_ref[...], wb_ref[...])
    if act is not None:
        y = act(y)
    if has_res:
        y = r_ref[...] + y
    o_ref[...] = y.astype(o_ref.dtype)


def _matmul_ws(x, w, layer, *, tm, tn, col0=0, n=None, out_dtype=F32, act=None, residual=None, name="matmul"):
    m, k = x.shape
    n = w.shape[2] if n is None else n
    assert m % tm == 0 and n % tn == 0 and col0 % tn == 0 and k % min(CAST_ROWS, k) == 0
    cb0 = col0 // tn
    in_specs = [pl.BlockSpec((tm, k), lambda j, i: (i, 0)), pl.BlockSpec((None, k, tn), lambda j, i: (layer, 0, cb0 + j))]
    args = [x, w]
    osz = jnp.dtype(out_dtype).itemsize
    nbytes = 2 * (tm * k * 2 + k * tn * 4 + tm * tn * osz) + k * tn * 2 + 2 * tm * tn * 4 + CAST_ROWS * tn * 8
    if residual is not None:
        in_specs.append(pl.BlockSpec((tm, tn), lambda j, i: (i, j)))
        args.append(residual)
        nbytes += 2 * tm * tn * 4
    return pl.pallas_call(
        functools.partial(_mm_ws_body, act=act, has_res=residual is not None),
        out_shape=jax.ShapeDtypeStruct((m, n), out_dtype),
        grid=(n // tn, m // tm),
        in_specs=in_specs,
        out_specs=pl.BlockSpec((tm, tn), lambda j, i: (i, j)),
        scratch_shapes=[pltpu.VMEM((k, tn), BF16)],
        compiler_params=_params(("parallel", "arbitrary"), nbytes),
        name=name,
    )(*args)


def _in_proj_body(x_ref, w_ref, nxt_ref, o_ref, wb_ref, *, n_plain):
    j = pl.program_id(0)
    first = pl.program_id(1) == 0

    @pl.when(jnp.logical_and(first, j < n_plain))
    def _():
        _cast_panel(w_ref, wb_ref)

    @pl.when(jnp.logical_and(first, j >= n_plain))
    def _():
        _cast_panel(w_ref, wb_ref, nxt_ref, GATE_RANK)

    o_ref[...] = _dot(x_ref[...], wb_ref[...])


def _in_proj(x, w_in, layer, *, tm, tn):
    m, k = x.shape
    assert m % tm == 0 and N_MAIN % tn == 0 and OFF_B % tn == 0 and tn % LANES == 0
    per = tn // LANES
    nbytes = 2 * (tm * k * 2 + k * (tn + LANES) * 4 + tm * tn * 4) + k * tn * 2 + CAST_ROWS * (tn + LANES) * 12
    return pl.pallas_call(
        functools.partial(_in_proj_body, n_plain=OFF_B // tn),
        out_shape=jax.ShapeDtypeStruct((m, N_MAIN), F32),
        grid=(N_MAIN // tn, m // tm),
        in_specs=[
            pl.BlockSpec((tm, k), lambda j, i: (i, 0)),
            pl.BlockSpec((None, k, tn), lambda j, i: (layer, 0, j)),
            pl.BlockSpec((None, k, LANES), lambda j, i: (layer, 0, (j + 1) * per)),
        ],
        out_specs=pl.BlockSpec((tm, tn), lambda j, i: (i, j)),
        scratch_shapes=[pltpu.VMEM((k, tn), BF16)],
        compiler_params=_params(("parallel", "arbitrary"), nbytes),
        name="in_proj",
    )(x, w_in, w_in)


def _mm_acc_res_body(x_ref, w_ref, r_ref, o_ref, acc_ref, *, n_k):
    kk = pl.program_id(2)

    @pl.when(kk == 0)
    def _():
        acc_ref[...] = r_ref[...]

    acc_ref[...] += _dot(x_ref[...], w_ref[...])

    @pl.when(kk == n_k - 1)
    def _():
        o_ref[...] = acc_ref[...]


def _matmul_ksplit_res(x, w, layer, residual, *, tm, tn, tk, name):
    m, k = x.shape
    n = w.shape[2]
    assert m % tm == 0 and n % tn == 0 and k % tk == 0
    n_k = k // tk
    nbytes = 2 * (tm * tk * 2 + tk * tn * 2 + 2 * tm * tn * 4) + tm * tn * 4
    return pl.pallas_call(
        functools.partial(_mm_acc_res_body, n_k=n_k),
        out_shape=jax.ShapeDtypeStruct((m, n), F32),
        grid=(n // tn, m // tm, n_k),
        in_specs=[
            pl.BlockSpec((tm, tk), lambda j, i, kk: (i, kk)),
            pl.BlockSpec((None, tk, tn), lambda j, i, kk: (layer, kk, j)),
            pl.BlockSpec((tm, tn), lambda j, i, kk: (i, j)),
        ],
        out_specs=pl.BlockSpec((tm, tn), lambda j, i, kk: (i, j)),
        scratch_shapes=[pltpu.VMEM((tm, tn), F32)],
        compiler_params=_params(("parallel", "parallel", "arbitrary"), nbytes),
        name=name,
    )(x, w, residual)


def _merge_up_body(oa_ref, ob_ref, oc_ref, ga_ref, gb_ref, gc_ref, wa_ref, wb_ref, wc_ref, o_ref, sa_ref, sb_ref, sc_ref):
    @pl.when(pl.program_id(1) == 0)
    def _():
        _cast_panel(wa_ref, sa_ref)
        _cast_panel(wb_ref, sb_ref)
        _cast_panel(wc_ref, sc_ref)

    acc = ga_ref[...].astype(F32) * _dot(oa_ref[...], sa_ref[...])
    acc += gb_ref[...].astype(F32) * _dot(ob_ref[...], sb_ref[...])
    acc += gc_ref[...].astype(F32) * _dot(oc_ref[...], sc_ref[...])
    o_ref[...] = acc.astype(o_ref.dtype)


def _merge_up(oa, ob, oc, gates, w_up_a, w_up_b, w_up_c, layer, *, tm, tn):
    m = oa.shape[0]
    d = w_up_a.shape[2]
    nt = d // tn
    assert m % tm == 0 and d % tn == 0

    def row(width):
        return pl.BlockSpec((tm, width), lambda j, i: (i, 0))

    def gate(branch):
        return pl.BlockSpec((tm, tn), lambda j, i: (i, branch * nt + j))

    def panel(kdim):
        return pl.BlockSpec((None, kdim, tn), lambda j, i: (layer, 0, j))

    kk = V_A + V_B + V_C
    nbytes = 2 * (tm * kk * 2 + 3 * tm * tn * 2 + kk * tn * 4 + tm * tn * 2) + kk * tn * 2 + 6 * tm * tn * 4 \
        + CAST_ROWS * tn * 8
    return pl.pallas_call(
        _merge_up_body,
        out_shape=jax.ShapeDtypeStruct((m, d), BF16),
        grid=(nt, m // tm),
        in_specs=[row(V_A), row(V_B), row(V_C), gate(0), gate(1), gate(2), panel(V_A), panel(V_B), panel(V_C)],
        out_specs=pl.BlockSpec((tm, tn), lambda j, i: (i, j)),
        scratch_shapes=[pltpu.VMEM((V_A, tn), BF16), pltpu.VMEM((V_B, tn), BF16), pltpu.VMEM((V_C, tn), BF16)],
        compiler_params=_params(("parallel", "arbitrary"), nbytes),
        name="merge_up",
    )(oa, ob, oc, gates, gates, gates, w_up_a, w_up_b, w_up_c)


def _ffn_up_body(x_ref, wg_ref, wu_ref, o_ref):
    x = x_ref[...]
    a = _dot(x, wg_ref[...])
    o_ref[...] = (a * _sigmoid(a) * _dot(x, wu_ref[...])).astype(o_ref.dtype)


def _ffn_up(x, wg, wu, layer, *, tm, tn):
    m, k = x.shape
    n = wg.shape[2]
    assert m % tm == 0 and n % tn == 0
    nbytes = 2 * (tm * k * 2 + 2 * k * tn * 2 + tm * tn * 2) + 3 * tm * tn * 4
    return pl.pallas_call(
        _ffn_up_body,
        out_shape=jax.ShapeDtypeStruct((m, n), BF16),
        grid=(m // tm, n // tn),
        in_specs=[pl.BlockSpec((tm, k), lambda i, j: (i, 0)), pl.BlockSpec((None, k, tn), lambda i, j: (layer, 0, j)),
                  pl.BlockSpec((None, k, tn), lambda i, j: (layer, 0, j))],
        out_specs=pl.BlockSpec((tm, tn), lambda i, j: (i, j)),
        compiler_params=_params(("parallel", "parallel"), nbytes),
        name="ffn_up",
    )(x, wg, wu)


def _gla_body(*refs, rows, n_t, t_valid, heads, has_bsum, has_s0, has_alias):
    it = iter(refs)
    q_ref, k_ref, v_ref, r_ref, la_ref, g_ref = [next(it) for _ in range(6)]
    bsum_ref = next(it) if has_bsum else None
    s0_ref = next(it) if has_s0 else None
    if has_alias:
        next(it)
    o_ref, sout_ref, st_scr, b_scr = [next(it) for _ in range(4)]
    for hh in range(heads):
        kk = slice(hh * DK_A, (hh + 1) * DK_A)
        vv = slice(hh * DV_A, (hh + 1) * DV_A)
        _gla_head(q_ref.at[:, kk], k_ref.at[:, kk], v_ref.at[:, vv], r_ref.at[:, vv], la_ref.at[:, kk], g_ref,
                  None if bsum_ref is None else bsum_ref.at[:, kk], None if s0_ref is None else s0_ref.at[hh],
                  o_ref.at[:, vv], sout_ref.at[hh], st_scr.at[hh], b_scr.at[hh], rows=rows, n_t=n_t, t_valid=t_valid)


def _gla_head(q_ref, k_ref, v_ref, r_ref, la_ref, g_ref, bsum_ref, s0_ref, o_ref, sout_ref, st_scr, b_scr, *,
              rows, n_t, t_valid):
    t = pl.program_id(1)

    @pl.when(t == 0)
    def _():
        if s0_ref is not None:
            st_scr[...] = s0_ref[...].T
        else:
            st_scr[...] = jnp.zeros_like(st_scr)

    la = la_ref[...]
    row_k = _iota((rows, DK_A), 0)
    if t_valid is not None:
        la = jnp.where(row_k < t_valid, la, 0.0)
    if bsum_ref is not None:
        b = bsum_ref[...]
    else:
        tri = (_iota((rows, rows), 0) >= _iota((rows, rows), 1)).astype(F32)
        b = jnp.dot(tri, la, precision=HIGHEST, preferred_element_type=F32)
    b_scr[...] = b

    q = q_ref[...] * (DK_A ** -0.5)
    k = k_ref[...]
    v16 = v_ref[...].astype(BF16)
    st = st_scr[...]
    o = _dot_nt((q * jnp.exp(b)).astype(BF16), st.astype(BF16))

    t_idx = _iota((rows, rows), 0)
    s_idx = _iota((rows, rows), 1)
    scores = jnp.where(t_idx == s_idx, _dot_nt(q.astype(BF16), k.astype(BF16)), 0.0)
    m = rows // 2
    while m >= 1:
        sub = row_k & (2 * m - 1)
        if m >= 4:
            pieces = [jnp.broadcast_to(b_scr[pl.ds(p * 2 * m + m - 1, 1), :], (2 * m, DK_A)) for p in range(rows // (2 * m))]
            log_fac = -jnp.abs(b - (pieces[0] if len(pieces) == 1 else jnp.concatenate(pieces, axis=0)))
        elif m == 2:
            la_next = pltpu.roll(la, rows - 1, 0)
            la_prev = pltpu.roll(la, 1, 0)
            log_fac = jnp.where(sub == 0, la_next, jnp.where(sub == 1, 0.0, jnp.where(sub == 2, la, la + la_prev)))
        else:
            log_fac = jnp.where(sub == 1, la, 0.0)
        fac = jnp.exp(log_fac)
        upper = sub >= m
        qs = jnp.where(upper, q * fac, 0.0)
        ks = jnp.where(upper, 0.0, k * fac)
        part = _dot_nt(qs.astype(BF16), ks.astype(BF16))
        if 2 * m < rows:
            shift = (2 * m).bit_length() - 1
            part = jnp.where((t_idx >> shift) == (s_idx >> shift), part, 0.0)
        scores = scores + part
        m //= 2
    o = o + _dot(scores.astype(BF16), v16)

    b_last = b_scr[pl.ds(rows - 1, 1), :]
    k_st = k if t_valid is None else jnp.where(row_k < t_valid, k, 0.0)
    st_new = st * jnp.exp(b_last) + _dot_tn(v16, (k_st * jnp.exp(b_last - b)).astype(BF16))
    st_scr[...] = st_new

    @pl.when(t == n_t - 1)
    def _():
        sout_ref[...] = st_new.T

    ms = jnp.mean(o * o, axis=-1, keepdims=True)
    r = r_ref[...]
    o_ref[...] = (r * _sigmoid(r) * (o * lax.rsqrt(ms + EPS) * g_ref[...])).astype(o_ref.dtype)


def _gla(h, la, bsum, g_gla, *, n_seq, seq_rows, rows, row0, t_valid, state0, out_rows, alias, heads=H_A):
    n_t = seq_rows // rows
    rb0 = row0 // rows
    groups = H_A // heads
    wk, wv = heads * DK_A, heads * DV_A

    def rmap(blk0):
        return lambda s, t: (rb0 + (s // groups) * n_t + t, blk0 + s % groups)

    smap = lambda s, t: (s // groups, s % groups, 0, 0)
    in_specs = [
        pl.BlockSpec((rows, wk), rmap(OFF_QA // wk)),
        pl.BlockSpec((rows, wk), rmap(OFF_KA // wk)),
        pl.BlockSpec((rows, wv), rmap(OFF_VA // wv)),
        pl.BlockSpec((rows, wv), rmap(OFF_RA // wv)),
        pl.BlockSpec((rows, wk), rmap(0)),
        pl.BlockSpec((1, DV_A), lambda s, t: (0, 0)),
    ]
    args = [h, h, h, h, la, g_gla.reshape(1, DV_A)]
    if bsum is not None:
        in_specs.append(pl.BlockSpec((rows, wk), rmap(0)))
        args.append(bsum)
    if state0 is not None:
        in_specs.append(pl.BlockSpec((None, heads, DK_A, DV_A), smap))
        args.append(state0)
    aliases = {}
    if alias is not None:
        in_specs.append(pl.BlockSpec(memory_space=pl.ANY))
        aliases = {len(args): 0}
        args.append(alias)
    nbytes = heads * (2 * rows * (4 * DK_A + 2 * DV_A) * 4 + 2 * rows * DV_A * 2 + 5 * DK_A * DV_A * 4
                      + rows * DK_A * 4 + 16 * rows * DV_A * 4)
    return pl.pallas_call(
        functools.partial(_gla_body, rows=rows, n_t=n_t, t_valid=t_valid, heads=heads, has_bsum=bsum is not None,
                          has_s0=state0 is not None, has_alias=alias is not None),
        out_shape=(jax.ShapeDtypeStruct((out_rows, V_A), BF16), jax.ShapeDtypeStruct((n_seq, H_A, DK_A, DV_A), F32)),
        grid=(n_seq * groups, n_t),
        in_specs=in_specs,
        out_specs=(pl.BlockSpec((rows, wv), rmap(0)), pl.BlockSpec((None, heads, DK_A, DV_A), smap)),
        scratch_shapes=[pltpu.VMEM((heads, DV_A, DK_A), F32), pltpu.VMEM((heads, rows, DK_A), F32)],
        input_output_aliases=aliases,
        compiler_params=_params(("parallel", "arbitrary"), nbytes),
        name="gla",
    )(*args)


def _gate_body(x_ref, w_ref, wa_hi_ref, wa_lo_ref, ba_ref, la_ref, bsum_ref, wb_ref, *, block):
    @pl.when(pl.program_id(0) == 0)
    def _():
        _cast_panel(w_ref, wb_ref)

    za = _dot(x_ref[...], wb_ref[...])
    za_hi = za.astype(BF16)
    za_lo = (za - za_hi.astype(F32)).astype(BF16)
    wa_hi = wa_hi_ref[...]
    z = _dot(za_hi, wa_hi) + (_dot(za_lo, wa_hi) + _dot(za_hi, wa_lo_ref[...])) + ba_ref[...]
    la = (jnp.minimum(z, 0.0) - jnp.log(1.0 + jnp.exp(-jnp.abs(z)))) * (1.0 / GATE_TAU)
    la_ref[...] = la
    tri = (_iota((block, block), 0) >= _iota((block, block), 1)).astype(F32)
    for p in range(x_ref.shape[0] // block):
        rows = slice(p * block, (p + 1) * block)
        bsum_ref[rows, :] = jnp.dot(tri, la[rows], precision=HIGHEST, preferred_element_type=F32)


def _gate(x, w_in, layer, w_alpha2, b_alpha, *, tm, block):
    m, k = x.shape
    assert m % tm == 0 and tm % block == 0 and OFF_B % LANES == 0
    wa = jnp.pad(w_alpha2, ((0, LANES - GATE_RANK), (0, 0)))
    wa_hi = wa.astype(BF16)
    wa_lo = (wa - wa_hi.astype(F32)).astype(BF16)
    full = lambda shape: pl.BlockSpec(shape, lambda i: (0, 0))
    nbytes = 2 * (tm * k * 2 + k * LANES * 4 + 2 * tm * QK_A * 4) + k * LANES * 2 + 8 * tm * QK_A * 4
    return pl.pallas_call(
        functools.partial(_gate_body, block=block),
        out_shape=(jax.ShapeDtypeStruct((m, QK_A), F32), jax.ShapeDtypeStruct((m, QK_A), F32)),
        grid=(m // tm,),
        in_specs=[pl.BlockSpec((tm, k), lambda i: (i, 0)),
                  pl.BlockSpec((None, k, LANES), lambda i: (layer, 0, OFF_B // LANES)),
                  full((LANES, QK_A)), full((LANES, QK_A)), full((1, QK_A))],
        out_specs=(pl.BlockSpec((tm, QK_A), lambda i: (i, 0)), pl.BlockSpec((tm, QK_A), lambda i: (i, 0))),
        scratch_shapes=[pltpu.VMEM((k, LANES), BF16)],
        compiler_params=_params(("arbitrary",), nbytes),
        name="gla_gate",
    )(x, w_in, wa_hi, wa_lo, b_alpha.reshape(1, QK_A))


def _ret_body(*refs, rows, n_t, t_valid, heads, has_s0, has_alias):
    it = iter(refs)
    q_ref, k_ref, v_ref, g_ref, cos_ref, sin_ref, lg_ref = [next(it) for _ in range(7)]
    s0_ref = next(it) if has_s0 else None
    if has_alias:
        next(it)
    o_ref, sout_ref, st_scr, rel_scr, dq_scr, dk_scr = [next(it) for _ in range(6)]
    head0 = (pl.program_id(0) % (H_B // heads)) * heads
    for hh in range(heads):
        kk = slice(hh * DK_B, (hh + 1) * DK_B)
        vv = slice(hh * DV_B, (hh + 1) * DV_B)
        _ret_head(q_ref.at[:, kk], k_ref.at[:, kk], v_ref.at[:, vv], g_ref.at[:, vv], cos_ref, sin_ref,
                  lg_ref.at[pl.ds(head0 + hh, 1), :], None if s0_ref is None else s0_ref.at[hh], o_ref.at[:, vv],
                  sout_ref.at[hh], st_scr.at[hh], rel_scr.at[hh], dq_scr.at[hh], dk_scr.at[hh],
                  rows=rows, n_t=n_t, t_valid=t_valid)


def _ret_head(q_ref, k_ref, v_ref, g_ref, cos_ref, sin_ref, lg_ref, s0_ref, o_ref, sout_ref, st_scr, rel_scr, dq_scr,
              dk_scr, *, rows, n_t, t_valid):
    t = pl.program_id(1)
    lg = lg_ref[...]
    n_valid = rows if t_valid is None else t_valid
    row_k = _iota((rows, DK_B), 0)

    @pl.when(t == 0)
    def _():
        if s0_ref is not None:
            st_scr[...] = s0_ref[...]
        else:
            st_scr[...] = jnp.zeros_like(st_scr)
        steps = jnp.minimum(row_k + 1, n_valid).astype(F32)
        dq_scr[...] = jnp.exp(lg[:, :DK_B] * steps)
        dk_scr[...] = jnp.exp(lg[:, :DK_B] * (n_valid - steps))
        st = jnp.minimum(_iota((rows, rows), 0) + 1, n_valid)
        ss = jnp.minimum(_iota((rows, rows), 1) + 1, n_valid)
        causal = _iota((rows, rows), 0) >= _iota((rows, rows), 1)
        rel_scr[...] = jnp.exp(jnp.where(causal, lg[:, :rows] * (st - ss).astype(F32), NEG))

    cos = cos_ref[...]
    sin = sin_ref[...]

    def rot(x):
        return x * cos + pltpu.roll(x, DK_B // 2, 1) * sin

    q = rot(q_ref[...])
    k = rot(k_ref[...]) * (DK_B ** -0.5)
    if t_valid is not None:
        k = jnp.where(row_k < t_valid, k, 0.0)
    decay_q = dq_scr[...]
    decay_k = dk_scr[...]

    v16 = v_ref[...].astype(BF16)
    state = st_scr[...]
    scores = _dot_nt(q.astype(BF16), k.astype(BF16)) * rel_scr[...]
    o = _dot(scores.astype(BF16), v16) + _dot((q * decay_q).astype(BF16), state.astype(BF16))
    state_new = state * jnp.exp(lg[:, :DV_B] * float(n_valid)) + _dot_tn((k * decay_k).astype(BF16), v16)
    st_scr[...] = state_new

    @pl.when(t == n_t - 1)
    def _():
        sout_ref[...] = state_new

    ms = jnp.mean(o * o, axis=-1, keepdims=True)
    g = g_ref[...]
    o_ref[...] = (g * _sigmoid(g) * (o * lax.rsqrt(ms + EPS))).astype(o_ref.dtype)


def _ret(h, cos, sin, lg_tab, *, n_seq, seq_rows, rows, row0, t_valid, state0, out_rows, alias, heads=RET_HEADS):
    n_t = seq_rows // rows
    rb0 = row0 // rows
    groups = H_B // heads
    wk, wv = heads * DK_B, heads * DV_B

    def rmap(blk0):
        return lambda s, t: (rb0 + (s // groups) * n_t + t, blk0 + s % groups)

    smap = lambda s, t: (s // groups, s % groups, 0, 0)
    in_specs = [
        pl.BlockSpec((rows, wk), rmap(OFF_QB // wk)),
        pl.BlockSpec((rows, wk), rmap(OFF_KB // wk)),
        pl.BlockSpec((rows, wv), rmap(OFF_VB // wv)),
        pl.BlockSpec((rows, wv), rmap(OFF_GB // wv)),
        pl.BlockSpec((rows, DK_B), lambda s, t: (t, 0)),
        pl.BlockSpec((rows, DK_B), lambda s, t: (t, 0)),
        pl.BlockSpec((H_B, RET_ROWS), lambda s, t: (0, 0)),
    ]
    args = [h, h, h, h, cos, sin, lg_tab]
    if state0 is not None:
        in_specs.append(pl.BlockSpec((None, heads, DK_B, DV_B), smap))
        args.append(state0)
    aliases = {}
    if alias is not None:
        in_specs.append(pl.BlockSpec(memory_space=pl.ANY))
        aliases = {len(args): 0}
        args.append(alias)
    nbytes = heads * (2 * rows * (2 * DK_B + 2 * DV_B) * 4 + 2 * rows * DV_B * 2 + 5 * DK_B * DV_B * 4
                      + 6 * rows * rows * 4 + 8 * rows * DV_B * 4) + 4 * rows * DK_B * 4
    return pl.pallas_call(
        functools.partial(_ret_body, rows=rows, n_t=n_t, t_valid=t_valid, heads=heads, has_s0=state0 is not None,
                          has_alias=alias is not None),
        out_shape=(jax.ShapeDtypeStruct((out_rows, V_B), BF16), jax.ShapeDtypeStruct((n_seq, H_B, DK_B, DV_B), F32)),
        grid=(n_seq * groups, n_t),
        in_specs=in_specs,
        out_specs=(pl.BlockSpec((rows, wv), rmap(0)), pl.BlockSpec((None, heads, DK_B, DV_B), smap)),
        scratch_shapes=[pltpu.VMEM((heads, DK_B, DV_B), F32), pltpu.VMEM((heads, rows, rows), F32),
                        pltpu.VMEM((heads, rows, DK_B), F32), pltpu.VMEM((heads, rows, DK_B), F32)],
        input_output_aliases=aliases,
        compiler_params=_params(("parallel", "arbitrary"), nbytes),
        name="retention",
    )(*args)


def _dil_prefill_body(slopes_ref, *refs, seq):
    qkv = refs[:9]
    o_ref, og_scr, lse_scr = refs[9:]
    slot = pl.program_id(1)
    scale = HD_C ** -0.5
    rr = _iota((BAND, BAND), 0)
    cc = _iota((BAND, BAND), 1)
    dist_cur = (rr - cc).astype(F32)
    ok_cur = rr >= cc
    ok_prev = cc >= rr
    n_units = seq // BAND
    for g, d in enumerate(DILATIONS):
        q_ref, k_ref, v_ref = qkv[3 * g:3 * g + 3]
        pen = slopes_ref[g * G_C + slot] * float(d)
        shift = d.bit_length() - 1

        def rows_of(start, d=d):
            return pl.ds(start, BAND) if d == 1 else pl.ds(start, BAND, stride=d)

        has_prev = seq // (BAND * d) > 1

        def unit(u, carry, q_ref=q_ref, k_ref=k_ref, v_ref=v_ref, pen=pen, d=d, shift=shift, rows_of=rows_of, g=g,
                 has_prev=has_prev):
            blk = u >> shift
            start = (u & (d - 1)) + blk * (BAND * d)
            q = q_ref[rows_of(start), :].astype(BF16)
            s_cur = _dot_nt(q, k_ref[rows_of(start), :].astype(BF16)) * scale - pen * dist_cur
            s_cur = jnp.where(ok_cur, s_cur, NEG)
            m = jnp.max(s_cur, axis=-1, keepdims=True)
            if has_prev:
                prev = jnp.maximum(start - BAND * d, 0)
                s_prev = _dot_nt(q, k_ref[rows_of(prev), :].astype(BF16)) * scale - pen * (dist_cur + float(BAND))
                s_prev = jnp.where(ok_prev, s_prev + jnp.where(blk > 0, 0.0, NEG), NEG)
                m = jnp.maximum(m, jnp.max(s_prev, axis=-1, keepdims=True))
            p_cur = jnp.exp(s_cur - m)
            l = jnp.sum(p_cur, axis=-1, keepdims=True)
            acc = _dot(p_cur.astype(BF16), v_ref[rows_of(start), :].astype(BF16))
            if has_prev:
                p_prev = jnp.exp(s_prev - m)
                l += jnp.sum(p_prev, axis=-1, keepdims=True)
                acc += _dot(p_prev.astype(BF16), v_ref[rows_of(prev), :].astype(BF16))
            og_scr[g, rows_of(start), :] = acc / l
            lse_scr[g, rows_of(start), :] = jnp.broadcast_to(m + jnp.log(l), (BAND, HD_C))
            return carry

        lax.fori_loop(0, n_units, unit, 0, unroll=2)

    lse = [lse_scr[g] for g in range(N_GROUPS_C)]
    top = jnp.maximum(jnp.maximum(lse[0], lse[1]), lse[2])
    w = [jnp.exp(x - top) for x in lse]
    total = w[0] + w[1] + w[2]
    o_ref[...] = ((w[0] * og_scr[0] + w[1] * og_scr[1] + w[2] * og_scr[2]) / total).astype(o_ref.dtype)


def _dil_prefill(h, slopes, *, n_seq, seq, out_rows):
    def cmap(g, j):
        return lambda b, s: (b, (OFF_C + g * 3 * V_C + j * V_C) // HD_C + s)

    in_specs = [pl.BlockSpec(memory_space=pltpu.SMEM)]
    in_specs += [pl.BlockSpec((seq, HD_C), cmap(g, j)) for g in range(N_GROUPS_C) for j in range(3)]
    nbytes = 2 * 9 * seq * HD_C * 4 + 2 * seq * HD_C * 2 + 2 * N_GROUPS_C * seq * HD_C * 4 + 8 * seq * HD_C * 4
    return pl.pallas_call(
        functools.partial(_dil_prefill_body, seq=seq),
        out_shape=jax.ShapeDtypeStruct((out_rows, V_C), BF16),
        grid=(n_seq, G_C),
        in_specs=in_specs,
        out_specs=pl.BlockSpec((seq, HD_C), lambda b, s: (b, s)),
        scratch_shapes=[pltpu.VMEM((N_GROUPS_C, seq, HD_C), F32), pltpu.VMEM((N_GROUPS_C, seq, HD_C), F32)],
        compiler_params=_params(("parallel", "parallel"), nbytes),
        name="dilated_prefill",
    )(slopes, *([h] * 9))


def _dil_step_body(slopes_ref, *refs):
    new = refs[:9]
    caches = refs[9:12]
    o_ref = refs[13]
    scale = HD_C ** -0.5
    for slot in range(G_C):
        lanes = slice(slot * HD_C, (slot + 1) * HD_C)
        scores, self_scores, values = [], [], []
        for g, d in enumerate(DILATIONS):
            q_ref, k_ref, v_ref = new[3 * g:3 * g + 3]
            c_ref = caches[g]
            length = c_ref.shape[1]
            q = q_ref[:, lanes]
            s = _dot_nt(q.astype(BF16), c_ref[0, :, lanes].astype(BF16)) * scale
            pos = _iota((SEQ_PAD, length), 1)
            pen = slopes_ref[g * G_C + slot] * (length - pos).astype(F32)
            s = jnp.where((pos & (d - 1)) == 0, s - pen, NEG)
            scores.append(s)
            self_scores.append(jnp.sum(q * k_ref[:, lanes], axis=-1, keepdims=True) * scale)
            values.append((c_ref, v_ref))
        m = self_scores[0]
        for x in self_scores[1:]:
            m = jnp.maximum(m, x)
        for x in scores:
            m = jnp.maximum(m, jnp.max(x, axis=-1, keepdims=True))
        acc = jnp.zeros((SEQ_PAD, HD_C), F32)
        total = jnp.zeros((SEQ_PAD, 1), F32)
        for s, s_self, (c_ref, v_ref) in zip(scores, self_scores, values):
            p = jnp.exp(s - m)
            p_self = jnp.exp(s_self - m)
            total += jnp.sum(p, axis=-1, keepdims=True) + p_self
            acc += _dot(p.astype(BF16), c_ref[1, :, lanes].astype(BF16)) + p_self * v_ref[:, lanes]
        o_ref[:, lanes] = (acc / total).astype(o_ref.dtype)


def _dil_step(h, slopes, caches, oc, *, n_seq, row0):
    rb0 = row0 // SEQ_PAD

    def cmap(g, j):
        return lambda b: (rb0 + b, (OFF_C + g * 3 * V_C + j * V_C) // V_C)

    in_specs = [pl.BlockSpec(memory_space=pltpu.SMEM)]
    in_specs += [pl.BlockSpec((SEQ_PAD, V_C), cmap(g, j)) for g in range(N_GROUPS_C) for j in range(3)]
    in_specs += [pl.BlockSpec((None, 2, c.shape[2], V_C), lambda b: (b, 0, 0, 0)) for c in caches]
    in_specs.append(pl.BlockSpec(memory_space=pl.ANY))
    nbytes = 2 * sum(2 * c.shape[2] * V_C * 4 for c in caches) + 16 * SEQ_PAD * caches[-1].shape[2] * 4
    return pl.pallas_call(
        _dil_step_body,
        out_shape=jax.ShapeDtypeStruct(oc.shape, oc.dtype),
        grid=(n_seq,),
        in_specs=in_specs,
        out_specs=pl.BlockSpec((SEQ_PAD, V_C), lambda b: (rb0 + b, 0)),
        input_output_aliases={13: 0},
        compiler_params=_params(("parallel",), nbytes),
        name="dilated_step",
    )(slopes, *([h] * 9), *caches, oc)


def _rope_tables(positions):
    half = DK_B // 2
    inv = ROPE_BASE ** (-np.arange(half, dtype=np.float64) / half)
    ang = np.asarray(positions, np.float64)[:, None] * inv[None, :]
    cos = np.concatenate([np.cos(ang), np.cos(ang)], axis=-1)
    sin = np.concatenate([-np.sin(ang), np.sin(ang)], axis=-1)
    return jnp.asarray(cos, F32), jnp.asarray(sin, F32)


def _layer(x, l, w, lw, state_gla, state_ret, caches, tabs, dims):
    n_p, seq, n_s, m_p, m = dims
    tm = _pick_tile(m, 704)
    xn = _rmsnorm(x, lw["g_mix"], BF16)
    h = _in_proj(xn, w["w_in"], l, tm=tm, tn=768)
    la, bsum = _gate(xn, w["w_in"], l, lw["w_alpha2"], lw["b_alpha"], tm=_pick_tile(m, 704, GLA_ROWS), block=GLA_ROWS)

    gla = functools.partial(_gla, h, la, g_gla=lw["g_gla"], out_rows=m)
    oa, sa_p = gla(bsum, n_seq=n_p, seq_rows=seq, rows=GLA_ROWS, row0=0, t_valid=None, state0=None, alias=None)
    oa, sa_s = gla(None, n_seq=n_s, seq_rows=SEQ_PAD, rows=SEQ_PAD, row0=m_p, t_valid=1, state0=state_gla, alias=oa)

    ret = functools.partial(_ret, h, out_rows=m)
    ob, sb_p = ret(tabs["cos_p"], tabs["sin_p"], tabs["lg"], n_seq=n_p, seq_rows=seq, rows=RET_ROWS, row0=0,
                   t_valid=None, state0=None, alias=None)
    ob, sb_s = ret(tabs["cos_s"], tabs["sin_s"], tabs["lg"], n_seq=n_s, seq_rows=SEQ_PAD, rows=SEQ_PAD, row0=m_p,
                   t_valid=1, state0=state_ret, alias=ob)

    oc = _dil_prefill(h, tabs["slopes"], n_seq=n_p, seq=seq, out_rows=m)
    flat = [c.reshape(c.shape[0], 2, c.shape[2], V_C) for c in caches]
    oc = _dil_step(h, tabs["slopes"], flat, oc, n_seq=n_s, row0=m_p)

    gates = _matmul_ws(xn, w["w_merge"], l, tm=tm, tn=768, out_dtype=BF16, act=_sigmoid, name="merge_gates")
    merged = _merge_up(oa, ob, oc, gates, w["w_up_a"], w["w_up_b"], w["w_up_c"], l, tm=tm, tn=512)
    x = _matmul_ws(merged, w["w_out"], l, tm=tm, tn=512, residual=x, name="out_proj")
    xf = _rmsnorm(x, lw["g_ffn"], BF16)
    hid = _ffn_up(xf, w["w_ffn_gate"], w["w_ffn_up"], l, tm=_pick_tile(m, 1728), tn=256)
    d_ff = hid.shape[1]
    x = _matmul_ksplit_res(hid, w["w_ffn_down"], l, x, tm=tm, tn=512, tk=d_ff // 2, name="ffn_down")

    win_p, win_s = [], []
    for g in range(N_GROUPS_C):
        off_k = OFF_C + g * 3 * V_C + V_C
        keep = min(BAND * DILATIONS[g], seq)
        kv_p = jnp.stack([lax.slice(h, ((b + 1) * seq - keep, off_k), ((b + 1) * seq, off_k + 2 * V_C))
                          for b in range(n_p)])
        win_p.append(kv_p.reshape(n_p, keep, 2, G_C, HD_C).transpose(0, 2, 1, 3, 4))
        kv_s = lax.slice(h, (m_p, off_k), (m, off_k + 2 * V_C)).reshape(n_s, SEQ_PAD, 2, G_C, HD_C)[:, :1]
        win_s.append(jnp.concatenate([caches[g][:, :, 1:], kv_s.transpose(0, 2, 1, 3, 4)], axis=2))
    return x, (sa_p, sa_s, sb_p, sb_s), win_p, win_s


def kernel(x_prompt, x_sample, state_gla, state_ret, cache_win0, cache_win1, cache_win2, g_mix, w_in, w_alpha2,
           b_alpha, g_gla, w_merge, w_up_a, w_up_b, w_up_c, w_out, g_ffn, w_ffn_gate, w_ffn_up, w_ffn_down, g_final):
    n_p, seq, d = x_prompt.shape
    n_s = x_sample.shape[0]
    depth = g_mix.shape[0]
    assert d == D_MODEL and x_sample.shape[1] == 1 and seq % (BAND * DILATIONS[-1]) == 0 and seq % RET_ROWS == 0
    m_p = n_p * seq
    m = m_p + n_s * SEQ_PAD
    dims = (n_p, seq, n_s, m_p, m)

    xs = jnp.pad(x_sample, ((0, 0), (0, SEQ_PAD - 1), (0, 0)))
    x = jnp.concatenate([x_prompt.reshape(m_p, d), xs.reshape(n_s * SEQ_PAD, d)], axis=0)

    cos_p, sin_p = _rope_tables(np.arange(seq))
    cos_s, sin_s = _rope_tables(PAST_LEN + np.arange(SEQ_PAD))
    log_gamma = np.log1p(-(2.0 ** (-5.0 - np.arange(H_B, dtype=np.float64))))
    heads = np.arange(1, N_GROUPS_C * G_C + 1, dtype=np.float64)
    tabs = {
        "cos_p": cos_p, "sin_p": sin_p, "cos_s": cos_s, "sin_s": sin_s,
        "lg": jnp.asarray(np.broadcast_to(log_gamma[:, None], (H_B, RET_ROWS)), F32),
        "slopes": jnp.asarray(2.0 ** (-8.0 * heads / (N_GROUPS_C * G_C)), F32),
    }

    assert w_in.shape[2] == N_MAIN + GATE_RANK
    w = {"w_in": w_in, "w_merge": w_merge, "w_up_a": w_up_a, "w_up_b": w_up_b, "w_up_c": w_up_c, "w_out": w_out,
         "w_ffn_gate": w_ffn_gate.astype(BF16), "w_ffn_up": w_ffn_up.astype(BF16),
         "w_ffn_down": w_ffn_down.astype(BF16)}
    caches = (cache_win0, cache_win1, cache_win2)
    outs = {k: [] for k in ("gla_p", "gla_s", "ret_p", "ret_s")}
    win_p = [[] for _ in range(N_GROUPS_C)]
    win_s = [[] for _ in range(N_GROUPS_C)]
    for l in range(depth):
        lw = {
            "g_mix": g_mix[l], "g_gla": g_gla[l], "g_ffn": g_ffn[l], "b_alpha": b_alpha[l],
            "w_alpha2": w_alpha2[l],
        }
        x, (sa_p, sa_s, sb_p, sb_s), wp, ws = _layer(x, l, w, lw, state_gla[l], state_ret[l], [c[l] for c in caches],
                                                     tabs, dims)
        outs["gla_p"].append(sa_p)
        outs["gla_s"].append(sa_s)
        outs["ret_p"].append(sb_p)
        outs["ret_s"].append(sb_s)
        for g in range(N_GROUPS_C):
            win_p[g].append(wp[g])
            win_s[g].append(ws[g])

    y_prompt = _rmsnorm(x, g_final, F32, rows=m_p).reshape(n_p, seq, d)
    y_sample = _rmsnorm(x[m_p:], g_final, F32).reshape(n_s, SEQ_PAD, d)[:, :1]
    return (y_prompt, y_sample, jnp.stack(outs["gla_p"]), jnp.stack(outs["gla_s"]), jnp.stack(outs["ret_p"]),
            jnp.stack(outs["ret_s"]), jnp.stack(win_p[0]), jnp.stack(win_s[0]), jnp.stack(win_p[1]),
            jnp.stack(win_s[1]), jnp.stack(win_p[2]), jnp.stack(win_s[2]))
```

```python
import functools

import numpy as np
import jax
import jax.numpy as jnp
from jax import lax
from jax.experimental import pallas as pl
from jax.experimental.pallas import tpu as pltpu

F32 = jnp.float32
BF16 = jnp.bfloat16
HIGHEST = lax.Precision.HIGHEST

D_MODEL = 4096
H_A, DK_A, DV_A = 4, 256, 512
GATE_RANK, GATE_TAU = 16, 16.0
H_B, DK_B, DV_B = 8, 128, 256
ROPE_BASE = 10000.0
DILATIONS = (1, 4, 16)
N_GROUPS_C, G_C, HD_C = 3, 4, 128
BAND = 128
PAST_LEN = 16384
QK_A, V_A = H_A * DK_A, H_A * DV_A
QK_B, V_B = H_B * DK_B, H_B * DV_B
V_C = G_C * HD_C
C_COLS = N_GROUPS_C * 3 * V_C
EPS = 1e-6
NEG = -1e30

OFF_QA, OFF_KA, OFF_VA, OFF_RA = 0, QK_A, 2 * QK_A, 2 * QK_A + V_A
OFF_B = 2 * QK_A + 2 * V_A
OFF_QB, OFF_KB, OFF_VB, OFF_GB = OFF_B, OFF_B + QK_B, OFF_B + 2 * QK_B, OFF_B + 2 * QK_B + V_B
OFF_C = OFF_B + 2 * QK_B + 2 * V_B
N_MAIN = OFF_C + C_COLS

LANES = 128
SEQ_PAD = 16
GLA_ROWS = 128
RET_ROWS = 256
RET_HEADS = 4
SUBLANES = 8
VMEM_CAP = 56 * 1024 * 1024


def _vmem(nbytes):
    return int(min(VMEM_CAP, nbytes + (6 << 20)))


def _params(sem, nbytes):
    return pltpu.CompilerParams(dimension_semantics=sem, vmem_limit_bytes=_vmem(nbytes))


def _pick_tile(m, cap, align=16):
    best = align
    for t in range(align, cap + 1, align):
        if m % t == 0:
            best = t
    assert m % best == 0, (m, cap, align)
    return best


def _iota(shape, dim):
    return lax.broadcasted_iota(jnp.int32, shape, dim)


def _dot(a, b):
    return jnp.dot(a, b, preferred_element_type=F32)


def _dot_nt(a, b):
    return lax.dot_general(a, b, (((1,), (1,)), ((), ())), preferred_element_type=F32)


def _dot_tn(a, b):
    return lax.dot_general(a, b, (((0,), (0,)), ((), ())), preferred_element_type=F32)


def _sigmoid(x):
    return 1.0 / (1.0 + jnp.exp(-x))


def _rmsnorm_body(x_ref, g_ref, o_ref):
    x = x_ref[...]
    ms = jnp.mean(x * x, axis=-1, keepdims=True)
    o_ref[...] = (x * lax.rsqrt(ms + EPS) * g_ref[...]).astype(o_ref.dtype)


def _rmsnorm(x, g, out_dtype, rows=None):
    m, d = x.shape
    m = m if rows is None else rows
    tm = _pick_tile(m, 512)
    nbytes = 2 * tm * d * (4 + jnp.dtype(out_dtype).itemsize) + 2 * tm * d * 4
    return pl.pallas_call(
        _rmsnorm_body,
        out_shape=jax.ShapeDtypeStruct((m, d), out_dtype),
        grid=(m // tm,),
        in_specs=[pl.BlockSpec((tm, d), lambda i: (i, 0)), pl.BlockSpec((1, d), lambda i: (0, 0))],
        out_specs=pl.BlockSpec((tm, d), lambda i: (i, 0)),
        compiler_params=_params(("parallel",), nbytes),
        name="rmsnorm",
    )(x, g.reshape(1, d))


def _mm_ws_body(*refs, act, has_res, transposed):
    x_ref, w_ref = refs[:2]
    r_ref = refs[2] if has_res else None
    o_ref = refs[-1]
    if transposed:
        y = _dot_nt(x_ref[...], w_ref[0].astype(BF16))
    else:
        y = _dot(x_ref[...], w_ref[...].astype(BF16))
    if act is not None:
        y = act(y)
    if has_res:
        y = r_ref[...] + y
    o_ref[...] = y.astype(o_ref.dtype)


def _matmul_ws(x, w, layer, *, tm, tn, n=None, row_of=None, out_dtype=F32, act=None, residual=None, name="matmul"):
    m, k = x.shape
    transposed = row_of is not None
    n = w.shape[2] if n is None else n
    assert m % tm == 0 and n % tn == 0
    if transposed:
        w_spec = pl.BlockSpec((pl.Element(1), pl.Element(tn), pl.Element(k)),
                              lambda j, i: (layer, pl.multiple_of(row_of(j), SUBLANES), 0))
    else:
        w_spec = pl.BlockSpec((None, k, tn), lambda j, i: (layer, 0, j))
    in_specs = [pl.BlockSpec((tm, k), lambda j, i: (i, 0)), w_spec]
    args = [x, w]
    osz = jnp.dtype(out_dtype).itemsize
    nbytes = 2 * (tm * k * 2 + k * tn * 4 + tm * tn * osz) + k * tn * 2 + 2 * tm * tn * 4
    if residual is not None:
        in_specs.append(pl.BlockSpec((tm, tn), lambda j, i: (i, j)))
        args.append(residual)
        nbytes += 2 * tm * tn * 4
    return pl.pallas_call(
        functools.partial(_mm_ws_body, act=act, has_res=residual is not None, transposed=transposed),
        out_shape=jax.ShapeDtypeStruct((m, n), out_dtype),
        grid=(n // tn, m // tm),
        in_specs=in_specs,
        out_specs=pl.BlockSpec((tm, tn), lambda j, i: (i, j)),
        compiler_params=_params(("parallel", "parallel"), nbytes),
        name=name,
    )(*args)


def _in_proj(x, w_in_t, layer, *, tm, tn):
    assert OFF_B % tn == 0
    return _matmul_ws(x, w_in_t, layer, tm=tm, tn=tn, n=N_MAIN, name="in_proj",
                      row_of=lambda j: j * tn + jnp.where(j >= OFF_B // tn, GATE_RANK, 0))


def _mm_acc_res_body(x_ref, w_ref, r_ref, o_ref, acc_ref, *, n_k):
    kk = pl.program_id(2)

    @pl.when(kk == 0)
    def _():
        acc_ref[...] = r_ref[...]

    acc_ref[...] += _dot(x_ref[...], w_ref[...])

    @pl.when(kk == n_k - 1)
    def _():
        o_ref[...] = acc_ref[...]


def _matmul_ksplit_res(x, w, layer, residual, *, tm, tn, tk, name):
    m, k = x.shape
    n = w.shape[2]
    assert m % tm == 0 and n % tn == 0 and k % tk == 0
    n_k = k // tk
    nbytes = 2 * (tm * tk * 2 + tk * tn * 2 + 2 * tm * tn * 4) + tm * tn * 4
    return pl.pallas_call(
        functools.partial(_mm_acc_res_body, n_k=n_k),
        out_shape=jax.ShapeDtypeStruct((m, n), F32),
        grid=(n // tn, m // tm, n_k),
        in_specs=[
            pl.BlockSpec((tm, tk), lambda j, i, kk: (i, kk)),
            pl.BlockSpec((None, tk, tn), lambda j, i, kk: (layer, kk, j)),
            pl.BlockSpec((tm, tn), lambda j, i, kk: (i, j)),
        ],
        out_specs=pl.BlockSpec((tm, tn), lambda j, i, kk: (i, j)),
        scratch_shapes=[pltpu.VMEM((tm, tn), F32)],
        compiler_params=_params(("parallel", "parallel", "arbitrary"), nbytes),
        name=name,
    )(x, w, residual)


def _merge_up_body(oa_ref, ob_ref, oc_ref, ga_ref, gb_ref, gc_ref, wa_ref, wb_ref, wc_ref, o_ref):
    acc = ga_ref[...].astype(F32) * _dot(oa_ref[...], wa_ref[...].astype(BF16))
    acc += gb_ref[...].astype(F32) * _dot(ob_ref[...], wb_ref[...].astype(BF16))
    acc += gc_ref[...].astype(F32) * _dot(oc_ref[...], wc_ref[...].astype(BF16))
    o_ref[...] = acc.astype(o_ref.dtype)


def _merge_up(oa, ob, oc, gates, w_up_a, w_up_b, w_up_c, layer, *, tm, tn):
    m = oa.shape[0]
    d = w_up_a.shape[2]
    nt = d // tn
    assert m % tm == 0 and d % tn == 0

    def row(width):
        return pl.BlockSpec((tm, width), lambda j, i: (i, 0))

    def gate(branch):
        return pl.BlockSpec((tm, tn), lambda j, i: (i, branch * nt + j))

    def panel(kdim):
        return pl.BlockSpec((None, kdim, tn), lambda j, i: (layer, 0, j))

    kk = V_A + V_B + V_C
    nbytes = 2 * (tm * kk * 2 + 3 * tm * tn * 2 + kk * tn * 4 + tm * tn * 2) + kk * tn * 2 + 6 * tm * tn * 4
    return pl.pallas_call(
        _merge_up_body,
        out_shape=jax.ShapeDtypeStruct((m, d), BF16),
        grid=(nt, m // tm),
        in_specs=[row(V_A), row(V_B), row(V_C), gate(0), gate(1), gate(2), panel(V_A), panel(V_B), panel(V_C)],
        out_specs=pl.BlockSpec((tm, tn), lambda j, i: (i, j)),
        compiler_params=_params(("parallel", "parallel"), nbytes),
        name="merge_up",
    )(oa, ob, oc, gates, gates, gates, w_up_a, w_up_b, w_up_c)


def _ffn_up_body(x_ref, wg_ref, wu_ref, o_ref):
    x = x_ref[...]
    a = _dot(x, wg_ref[...].astype(BF16))
    o_ref[...] = (a * _sigmoid(a) * _dot(x, wu_ref[...].astype(BF16))).astype(o_ref.dtype)


def _ffn_up(x, wg, wu, layer, *, tm, tn):
    m, k = x.shape
    n = wg.shape[2]
    assert m % tm == 0 and n % tn == 0
    nbytes = 2 * (tm * k * 2 + 2 * k * tn * 4 + tm * tn * 2) + 2 * k * tn * 2 + 3 * tm * tn * 4
    return pl.pallas_call(
        _ffn_up_body,
        out_shape=jax.ShapeDtypeStruct((m, n), BF16),
        grid=(m // tm, n // tn),
        in_specs=[pl.BlockSpec((tm, k), lambda i, j: (i, 0)), pl.BlockSpec((None, k, tn), lambda i, j: (layer, 0, j)),
                  pl.BlockSpec((None, k, tn), lambda i, j: (layer, 0, j))],
        out_specs=pl.BlockSpec((tm, tn), lambda i, j: (i, j)),
        compiler_params=_params(("parallel", "parallel"), nbytes),
        name="ffn_up",
    )(x, wg, wu)


def _gla_body(*refs, rows, n_t, t_valid, heads, has_bsum, has_s0, has_alias):
    it = iter(refs)
    q_ref, k_ref, v_ref, r_ref, la_ref, g_ref = [next(it) for _ in range(6)]
    bsum_ref = next(it) if has_bsum else None
    s0_ref = next(it) if has_s0 else None
    if has_alias:
        next(it)
    o_ref, sout_ref, st_scr, b_scr = [next(it) for _ in range(4)]
    for hh in range(heads):
        kk = slice(hh * DK_A, (hh + 1) * DK_A)
        vv = slice(hh * DV_A, (hh + 1) * DV_A)
        _gla_head(q_ref.at[:, kk], k_ref.at[:, kk], v_ref.at[:, vv], r_ref.at[:, vv], la_ref.at[:, kk], g_ref,
                  None if bsum_ref is None else bsum_ref.at[:, kk], None if s0_ref is None else s0_ref.at[hh],
                  o_ref.at[:, vv], sout_ref.at[hh], st_scr.at[hh], b_scr.at[hh], rows=rows, n_t=n_t, t_valid=t_valid)


def _gla_head(q_ref, k_ref, v_ref, r_ref, la_ref, g_ref, bsum_ref, s0_ref, o_ref, sout_ref, st_scr, b_scr, *,
              rows, n_t, t_valid):
    t = pl.program_id(1)

    @pl.when(t == 0)
    def _():
        if s0_ref is not None:
            st_scr[...] = s0_ref[...].T
        else:
            st_scr[...] = jnp.zeros_like(st_scr)

    la = la_ref[...]
    row_k = _iota((rows, DK_A), 0)
    if t_valid is not None:
        la = jnp.where(row_k < t_valid, la, 0.0)
    if bsum_ref is not None:
        b = bsum_ref[...]
    else:
        tri = (_iota((rows, rows), 0) >= _iota((rows, rows), 1)).astype(F32)
        b = jnp.dot(tri, la, precision=HIGHEST, preferred_element_type=F32)
    b_scr[...] = b

    q = q_ref[...] * (DK_A ** -0.5)
    k = k_ref[...]
    v16 = v_ref[...].astype(BF16)
    st = st_scr[...]
    o = _dot_nt((q * jnp.exp(b)).astype(BF16), st.astype(BF16))

    t_idx = _iota((rows, rows), 0)
    s_idx = _iota((rows, rows), 1)
    scores = jnp.where(t_idx == s_idx, _dot_nt(q.astype(BF16), k.astype(BF16)), 0.0)
    m = rows // 2
    while m >= 1:
        sub = row_k & (2 * m - 1)
        if m >= 4:
            pieces = [jnp.broadcast_to(b_scr[pl.ds(p * 2 * m + m - 1, 1), :], (2 * m, DK_A)) for p in range(rows // (2 * m))]
            log_fac = -jnp.abs(b - (pieces[0] if len(pieces) == 1 else jnp.concatenate(pieces, axis=0)))
        elif m == 2:
            la_next = pltpu.roll(la, rows - 1, 0)
            la_prev = pltpu.roll(la, 1, 0)
            log_fac = jnp.where(sub == 0, la_next, jnp.where(sub == 1, 0.0, jnp.where(sub == 2, la, la + la_prev)))
        else:
            log_fac = jnp.where(sub == 1, la, 0.0)
        fac = jnp.exp(log_fac)
        upper = sub >= m
        qs = jnp.where(upper, q * fac, 0.0)
        ks = jnp.where(upper, 0.0, k * fac)
        part = _dot_nt(qs.astype(BF16), ks.astype(BF16))
        if 2 * m < rows:
            shift = (2 * m).bit_length() - 1
            part = jnp.where((t_idx >> shift) == (s_idx >> shift), part, 0.0)
        scores = scores + part
        m //= 2
    o = o + _dot(scores.astype(BF16), v16)

    b_last = b_scr[pl.ds(rows - 1, 1), :]
    k_st = k if t_valid is None else jnp.where(row_k < t_valid, k, 0.0)
    st_new = st * jnp.exp(b_last) + _dot_tn(v16, (k_st * jnp.exp(b_last - b)).astype(BF16))
    st_scr[...] = st_new

    @pl.when(t == n_t - 1)
    def _():
        sout_ref[...] = st_new.T

    ms = jnp.mean(o * o, axis=-1, keepdims=True)
    r = r_ref[...]
    o_ref[...] = (r * _sigmoid(r) * (o * lax.rsqrt(ms + EPS) * g_ref[...])).astype(o_ref.dtype)


def _gla(h, la, bsum, g_gla, *, n_seq, seq_rows, rows, row0, t_valid, state0, out_rows, alias, heads=H_A):
    n_t = seq_rows // rows
    rb0 = row0 // rows
    groups = H_A // heads
    wk, wv = heads * DK_A, heads * DV_A

    def rmap(blk0):
        return lambda s, t: (rb0 + (s // groups) * n_t + t, blk0 + s % groups)

    smap = lambda s, t: (s // groups, s % groups, 0, 0)
    in_specs = [
        pl.BlockSpec((rows, wk), rmap(OFF_QA // wk)),
        pl.BlockSpec((rows, wk), rmap(OFF_KA // wk)),
        pl.BlockSpec((rows, wv), rmap(OFF_VA // wv)),
        pl.BlockSpec((rows, wv), rmap(OFF_RA // wv)),
        pl.BlockSpec((rows, wk), rmap(0)),
        pl.BlockSpec((1, DV_A), lambda s, t: (0, 0)),
    ]
    args = [h, h, h, h, la, g_gla.reshape(1, DV_A)]
    if bsum is not None:
        in_specs.append(pl.BlockSpec((rows, wk), rmap(0)))
        args.append(bsum)
    if state0 is not None:
        in_specs.append(pl.BlockSpec((None, heads, DK_A, DV_A), smap))
        args.append(state0)
    aliases = {}
    if alias is not None:
        in_specs.append(pl.BlockSpec(memory_space=pl.ANY))
        aliases = {len(args): 0}
        args.append(alias)
    nbytes = heads * (2 * rows * (4 * DK_A + 2 * DV_A) * 4 + 2 * rows * DV_A * 2 + 5 * DK_A * DV_A * 4
                      + rows * DK_A * 4 + 16 * rows * DV_A * 4)
    return pl.pallas_call(
        functools.partial(_gla_body, rows=rows, n_t=n_t, t_valid=t_valid, heads=heads, has_bsum=bsum is not None,
                          has_s0=state0 is not None, has_alias=alias is not None),
        out_shape=(jax.ShapeDtypeStruct((out_rows, V_A), BF16), jax.ShapeDtypeStruct((n_seq, H_A, DK_A, DV_A), F32)),
        grid=(n_seq * groups, n_t),
        in_specs=in_specs,
        out_specs=(pl.BlockSpec((rows, wv), rmap(0)), pl.BlockSpec((None, heads, DK_A, DV_A), smap)),
        scratch_shapes=[pltpu.VMEM((heads, DV_A, DK_A), F32), pltpu.VMEM((heads, rows, DK_A), F32)],
        input_output_aliases=aliases,
        compiler_params=_params(("parallel", "arbitrary"), nbytes),
        name="gla",
    )(*args)


def _gate_body(x_ref, w_ref, wa_hi_ref, wa_lo_ref, ba_ref, la_ref, bsum_ref, *, block):
    za = _dot_nt(x_ref[...], w_ref[0].astype(BF16))
    za_hi = za.astype(BF16)
    za_lo = (za - za_hi.astype(F32)).astype(BF16)
    wa_hi = wa_hi_ref[...]
    z = _dot(za_hi, wa_hi) + (_dot(za_lo, wa_hi) + _dot(za_hi, wa_lo_ref[...])) + ba_ref[...]
    la = (jnp.minimum(z, 0.0) - jnp.log(1.0 + jnp.exp(-jnp.abs(z)))) * (1.0 / GATE_TAU)
    la_ref[...] = la
    tri = (_iota((block, block), 0) >= _iota((block, block), 1)).astype(F32)
    for p in range(x_ref.shape[0] // block):
        rows = slice(p * block, (p + 1) * block)
        bsum_ref[rows, :] = jnp.dot(tri, la[rows], precision=HIGHEST, preferred_element_type=F32)


def _gate(x, w_in_t, layer, w_alpha2, b_alpha, *, tm, block):
    m, k = x.shape
    assert m % tm == 0 and tm % block == 0 and OFF_B % SUBLANES == 0
    wa = jnp.pad(w_alpha2, ((0, LANES - GATE_RANK), (0, 0)))
    wa_hi = wa.astype(BF16)
    wa_lo = (wa - wa_hi.astype(F32)).astype(BF16)
    full = lambda shape: pl.BlockSpec(shape, lambda i: (0, 0))
    nbytes = 2 * (tm * k * 2 + k * LANES * 4 + 2 * tm * QK_A * 4) + k * LANES * 2 + 8 * tm * QK_A * 4
    return pl.pallas_call(
        functools.partial(_gate_body, block=block),
        out_shape=(jax.ShapeDtypeStruct((m, QK_A), F32), jax.ShapeDtypeStruct((m, QK_A), F32)),
        grid=(m // tm,),
        in_specs=[pl.BlockSpec((tm, k), lambda i: (i, 0)),
                  pl.BlockSpec((pl.Element(1), pl.Element(LANES), pl.Element(k)), lambda i: (layer, OFF_B, 0)),
                  full((LANES, QK_A)), full((LANES, QK_A)), full((1, QK_A))],
        out_specs=(pl.BlockSpec((tm, QK_A), lambda i: (i, 0)), pl.BlockSpec((tm, QK_A), lambda i: (i, 0))),
        compiler_params=_params(("parallel",), nbytes),
        name="gla_gate",
    )(x, w_in_t, wa_hi, wa_lo, b_alpha.reshape(1, QK_A))


def _ret_body(*refs, rows, n_t, t_valid, heads, has_s0, has_alias):
    it = iter(refs)
    q_ref, k_ref, v_ref, g_ref, cos_ref, sin_ref, lg_ref = [next(it) for _ in range(7)]
    s0_ref = next(it) if has_s0 else None
    if has_alias:
        next(it)
    o_ref, sout_ref, st_scr, rel_scr, dq_scr, dk_scr = [next(it) for _ in range(6)]
    head0 = (pl.program_id(0) % (H_B // heads)) * heads
    for hh in range(heads):
        kk = slice(hh * DK_B, (hh + 1) * DK_B)
        vv = slice(hh * DV_B, (hh + 1) * DV_B)
        _ret_head(q_ref.at[:, kk], k_ref.at[:, kk], v_ref.at[:, vv], g_ref.at[:, vv], cos_ref, sin_ref,
                  lg_ref.at[pl.ds(head0 + hh, 1), :], None if s0_ref is None else s0_ref.at[hh], o_ref.at[:, vv],
                  sout_ref.at[hh], st_scr.at[hh], rel_scr.at[hh], dq_scr.at[hh], dk_scr.at[hh],
                  rows=rows, n_t=n_t, t_valid=t_valid)


def _ret_head(q_ref, k_ref, v_ref, g_ref, cos_ref, sin_ref, lg_ref, s0_ref, o_ref, sout_ref, st_scr, rel_scr, dq_scr,
              dk_scr, *, rows, n_t, t_valid):
    t = pl.program_id(1)
    lg = lg_ref[...]
    n_valid = rows if t_valid is None else t_valid
    row_k = _iota((rows, DK_B), 0)

    @pl.when(t == 0)
    def _():
        if s0_ref is not None:
            st_scr[...] = s0_ref[...]
        else:
            st_scr[...] = jnp.zeros_like(st_scr)
        steps = jnp.minimum(row_k + 1, n_valid).astype(F32)
        dq_scr[...] = jnp.exp(lg[:, :DK_B] * steps)
        dk_scr[...] = jnp.exp(lg[:, :DK_B] * (n_valid - steps))
        st = jnp.minimum(_iota((rows, rows), 0) + 1, n_valid)
        ss = jnp.minimum(_iota((rows, rows), 1) + 1, n_valid)
        causal = _iota((rows, rows), 0) >= _iota((rows, rows), 1)
        rel_scr[...] = jnp.exp(jnp.where(causal, lg[:, :rows] * (st - ss).astype(F32), NEG))

    cos = cos_ref[...]
    sin = sin_ref[...]

    def rot(x):
        return x * cos + pltpu.roll(x, DK_B // 2, 1) * sin

    q = rot(q_ref[...])
    k = rot(k_ref[...]) * (DK_B ** -0.5)
    if t_valid is not None:
        k = jnp.where(row_k < t_valid, k, 0.0)
    decay_q = dq_scr[...]
    decay_k = dk_scr[...]

    v16 = v_ref[...].astype(BF16)
    state = st_scr[...]
    scores = _dot_nt(q.astype(BF16), k.astype(BF16)) * rel_scr[...]
    o = _dot(scores.astype(BF16), v16) + _dot((q * decay_q).astype(BF16), state.astype(BF16))
    state_new = state * jnp.exp(lg[:, :DV_B] * float(n_valid)) + _dot_tn((k * decay_k).astype(BF16), v16)
    st_scr[...] = state_new

    @pl.when(t == n_t - 1)
    def _():
        sout_ref[...] = state_new

    ms = jnp.mean(o * o, axis=-1, keepdims=True)
    g = g_ref[...]
    o_ref[...] = (g * _sigmoid(g) * (o * lax.rsqrt(ms + EPS))).astype(o_ref.dtype)


def _ret(h, cos, sin, lg_tab, *, n_seq, seq_rows, rows, row0, t_valid, state0, out_rows, alias, heads=RET_HEADS):
    n_t = seq_rows // rows
    rb0 = row0 // rows
    groups = H_B // heads
    wk, wv = heads * DK_B, heads * DV_B

    def rmap(blk0):
        return lambda s, t: (rb0 + (s // groups) * n_t + t, blk0 + s % groups)

    smap = lambda s, t: (s // groups, s % groups, 0, 0)
    in_specs = [
        pl.BlockSpec((rows, wk), rmap(OFF_QB // wk)),
        pl.BlockSpec((rows, wk), rmap(OFF_KB // wk)),
        pl.BlockSpec((rows, wv), rmap(OFF_VB // wv)),
        pl.BlockSpec((rows, wv), rmap(OFF_GB // wv)),
        pl.BlockSpec((rows, DK_B), lambda s, t: (t, 0)),
        pl.BlockSpec((rows, DK_B), lambda s, t: (t, 0)),
        pl.BlockSpec((H_B, RET_ROWS), lambda s, t: (0, 0)),
    ]
    args = [h, h, h, h, cos, sin, lg_tab]
    if state0 is not None:
        in_specs.append(pl.BlockSpec((None, heads, DK_B, DV_B), smap))
        args.append(state0)
    aliases = {}
    if alias is not None:
        in_specs.append(pl.BlockSpec(memory_space=pl.ANY))
        aliases = {len(args): 0}
        args.append(alias)
    nbytes = heads * (2 * rows * (2 * DK_B + 2 * DV_B) * 4 + 2 * rows * DV_B * 2 + 5 * DK_B * DV_B * 4
                      + 6 * rows * rows * 4 + 8 * rows * DV_B * 4) + 4 * rows * DK_B * 4
    return pl.pallas_call(
        functools.partial(_ret_body, rows=rows, n_t=n_t, t_valid=t_valid, heads=heads, has_s0=state0 is not None,
                          has_alias=alias is not None),
        out_shape=(jax.ShapeDtypeStruct((out_rows, V_B), BF16), jax.ShapeDtypeStruct((n_seq, H_B, DK_B, DV_B), F32)),
        grid=(n_seq * groups, n_t),
        in_specs=in_specs,
        out_specs=(pl.BlockSpec((rows, wv), rmap(0)), pl.BlockSpec((None, heads, DK_B, DV_B), smap)),
        scratch_shapes=[pltpu.VMEM((heads, DK_B, DV_B), F32), pltpu.VMEM((heads, rows, rows), F32),
                        pltpu.VMEM((heads, rows, DK_B), F32), pltpu.VMEM((heads, rows, DK_B), F32)],
        input_output_aliases=aliases,
        compiler_params=_params(("parallel", "arbitrary"), nbytes),
        name="retention",
    )(*args)


def _dil_prefill_body(slopes_ref, *refs, seq):
    qkv = refs[:9]
    o_ref, og_scr, lse_scr = refs[9:]
    slot = pl.program_id(1)
    scale = HD_C ** -0.5
    rr = _iota((BAND, BAND), 0)
    cc = _iota((BAND, BAND), 1)
    dist_cur = (rr - cc).astype(F32)
    ok_cur = rr >= cc
    ok_prev = cc >= rr
    n_units = seq // BAND
    for g, d in enumerate(DILATIONS):
        q_ref, k_ref, v_ref = qkv[3 * g:3 * g + 3]
        pen = slopes_ref[g * G_C + slot] * float(d)
        shift = d.bit_length() - 1

        def rows_of(start, d=d):
            return pl.ds(start, BAND) if d == 1 else pl.ds(start, BAND, stride=d)

        has_prev = seq // (BAND * d) > 1

        def unit(u, carry, q_ref=q_ref, k_ref=k_ref, v_ref=v_ref, pen=pen, d=d, shift=shift, rows_of=rows_of, g=g,
                 has_prev=has_prev):
            blk = u >> shift
            start = (u & (d - 1)) + blk * (BAND * d)
            q = q_ref[rows_of(start), :].astype(BF16)
            s_cur = _dot_nt(q, k_ref[rows_of(start), :].astype(BF16)) * scale - pen * dist_cur
            s_cur = jnp.where(ok_cur, s_cur, NEG)
            m = jnp.max(s_cur, axis=-1, keepdims=True)
            if has_prev:
                prev = jnp.maximum(start - BAND * d, 0)
                s_prev = _dot_nt(q, k_ref[rows_of(prev), :].astype(BF16)) * scale - pen * (dist_cur + float(BAND))
                s_prev = jnp.where(ok_prev, s_prev + jnp.where(blk > 0, 0.0, NEG), NEG)
                m = jnp.maximum(m, jnp.max(s_prev, axis=-1, keepdims=True))
            p_cur = jnp.exp(s_cur - m)
            l = jnp.sum(p_cur, axis=-1, keepdims=True)
            acc = _dot(p_cur.astype(BF16), v_ref[rows_of(start), :].astype(BF16))
            if has_prev:
                p_prev = jnp.exp(s_prev - m)
                l += jnp.sum(p_prev, axis=-1, keepdims=True)
                acc += _dot(p_prev.astype(BF16), v_ref[rows_of(prev), :].astype(BF16))
            og_scr[g, rows_of(start), :] = acc / l
            lse_scr[g, rows_of(start), :] = jnp.broadcast_to(m + jnp.log(l), (BAND, HD_C))
            return carry

        lax.fori_loop(0, n_units, unit, 0, unroll=2)

    lse = [lse_scr[g] for g in range(N_GROUPS_C)]
    top = jnp.maximum(jnp.maximum(lse[0], lse[1]), lse[2])
    w = [jnp.exp(x - top) for x in lse]
    total = w[0] + w[1] + w[2]
    o_ref[...] = ((w[0] * og_scr[0] + w[1] * og_scr[1] + w[2] * og_scr[2]) / total).astype(o_ref.dtype)


def _dil_prefill(h, slopes, *, n_seq, seq, out_rows):
    def cmap(g, j):
        return lambda b, s: (b, (OFF_C + g * 3 * V_C + j * V_C) // HD_C + s)

    in_specs = [pl.BlockSpec(memory_space=pltpu.SMEM)]
    in_specs += [pl.BlockSpec((seq, HD_C), cmap(g, j)) for g in range(N_GROUPS_C) for j in range(3)]
    nbytes = 2 * 9 * seq * HD_C * 4 + 2 * seq * HD_C * 2 + 2 * N_GROUPS_C * seq * HD_C * 4 + 8 * seq * HD_C * 4
    return pl.pallas_call(
        functools.partial(_dil_prefill_body, seq=seq),
        out_shape=jax.ShapeDtypeStruct((out_rows, V_C), BF16),
        grid=(n_seq, G_C),
        in_specs=in_specs,
        out_specs=pl.BlockSpec((seq, HD_C), lambda b, s: (b, s)),
        scratch_shapes=[pltpu.VMEM((N_GROUPS_C, seq, HD_C), F32), pltpu.VMEM((N_GROUPS_C, seq, HD_C), F32)],
        compiler_params=_params(("parallel", "parallel"), nbytes),
        name="dilated_prefill",
    )(slopes, *([h] * 9))


def _dil_step_body(slopes_ref, *refs):
    new = refs[:9]
    caches = refs[9:12]
    o_ref = refs[13]
    scale = HD_C ** -0.5
    for slot in range(G_C):
        lanes = slice(slot * HD_C, (slot + 1) * HD_C)
        scores, self_scores, values = [], [], []
        for g, d in enumerate(DILATIONS):
            q_ref, k_ref, v_ref = new[3 * g:3 * g + 3]
            c_ref = caches[g]
            length = c_ref.shape[1]
            q = q_ref[:, lanes]
            s = _dot_nt(q.astype(BF16), c_ref[0, :, lanes].astype(BF16)) * scale
            pos = _iota((SEQ_PAD, length), 1)
            pen = slopes_ref[g * G_C + slot] * (length - pos).astype(F32)
            s = jnp.where((pos & (d - 1)) == 0, s - pen, NEG)
            scores.append(s)
            self_scores.append(jnp.sum(q * k_ref[:, lanes], axis=-1, keepdims=True) * scale)
            values.append((c_ref, v_ref))
        m = self_scores[0]
        for x in self_scores[1:]:
            m = jnp.maximum(m, x)
        for x in scores:
            m = jnp.maximum(m, jnp.max(x, axis=-1, keepdims=True))
        acc = jnp.zeros((SEQ_PAD, HD_C), F32)
        total = jnp.zeros((SEQ_PAD, 1), F32)
        for s, s_self, (c_ref, v_ref) in zip(scores, self_scores, values):
            p = jnp.exp(s - m)
            p_self = jnp.exp(s_self - m)
            total += jnp.sum(p, axis=-1, keepdims=True) + p_self
            acc += _dot(p.astype(BF16), c_ref[1, :, lanes].astype(BF16)) + p_self * v_ref[:, lanes]
        o_ref[:, lanes] = (acc / total).astype(o_ref.dtype)


def _dil_step(h, slopes, caches, oc, *, n_seq, row0):
    rb0 = row0 // SEQ_PAD

    def cmap(g, j):
        return lambda b: (rb0 + b, (OFF_C + g * 3 * V_C + j * V_C) // V_C)

    in_specs = [pl.BlockSpec(memory_space=pltpu.SMEM)]
    in_specs += [pl.BlockSpec((SEQ_PAD, V_C), cmap(g, j)) for g in range(N_GROUPS_C) for j in range(3)]
    in_specs += [pl.BlockSpec((None, 2, c.shape[2], V_C), lambda b: (b, 0, 0, 0)) for c in caches]
    in_specs.append(pl.BlockSpec(memory_space=pl.ANY))
    nbytes = 2 * sum(2 * c.shape[2] * V_C * 4 for c in caches) + 16 * SEQ_PAD * caches[-1].shape[2] * 4
    return pl.pallas_call(
        _dil_step_body,
        out_shape=jax.ShapeDtypeStruct(oc.shape, oc.dtype),
        grid=(n_seq,),
        in_specs=in_specs,
        out_specs=pl.BlockSpec((SEQ_PAD, V_C), lambda b: (rb0 + b, 0)),
        input_output_aliases={13: 0},
        compiler_params=_params(("parallel",), nbytes),
        name="dilated_step",
    )(slopes, *([h] * 9), *caches, oc)


def _rope_tables(positions):
    half = DK_B // 2
    inv = ROPE_BASE ** (-np.arange(half, dtype=np.float64) / half)
    ang = np.asarray(positions, np.float64)[:, None] * inv[None, :]
    cos = np.concatenate([np.cos(ang), np.cos(ang)], axis=-1)
    sin = np.concatenate([-np.sin(ang), np.sin(ang)], axis=-1)
    return jnp.asarray(cos, F32), jnp.asarray(sin, F32)


def _layer(x, l, w, lw, state_gla, state_ret, caches, tabs, dims):
    n_p, seq, n_s, m_p, m = dims
    tm = _pick_tile(m, 704)
    tm_big = _pick_tile(m, 1100)
    xn = _rmsnorm(x, lw["g_mix"], BF16)
    h = _in_proj(xn, w["w_in_t"], l, tm=tm_big, tn=768)
    la, bsum = _gate(xn, w["w_in_t"], l, lw["w_alpha2"], lw["b_alpha"], tm=_pick_tile(m, 704, GLA_ROWS), block=GLA_ROWS)

    gla = functools.partial(_gla, h, la, g_gla=lw["g_gla"], out_rows=m)
    oa, sa_p = gla(bsum, n_seq=n_p, seq_rows=seq, rows=GLA_ROWS, row0=0, t_valid=None, state0=None, alias=None)
    oa, sa_s = gla(None, n_seq=n_s, seq_rows=SEQ_PAD, rows=SEQ_PAD, row0=m_p, t_valid=1, state0=state_gla, alias=oa)

    ret = functools.partial(_ret, h, out_rows=m)
    ob, sb_p = ret(tabs["cos_p"], tabs["sin_p"], tabs["lg"], n_seq=n_p, seq_rows=seq, rows=RET_ROWS, row0=0,
                   t_valid=None, state0=None, alias=None)
    ob, sb_s = ret(tabs["cos_s"], tabs["sin_s"], tabs["lg"], n_seq=n_s, seq_rows=SEQ_PAD, rows=SEQ_PAD, row0=m_p,
                   t_valid=1, state0=state_ret, alias=ob)

    oc = _dil_prefill(h, tabs["slopes"], n_seq=n_p, seq=seq, out_rows=m)
    flat = [c.reshape(c.shape[0], 2, c.shape[2], V_C) for c in caches]
    oc = _dil_step(h, tabs["slopes"], flat, oc, n_seq=n_s, row0=m_p)

    gates = _matmul_ws(xn, w["w_merge"], l, tm=tm_big, tn=768, out_dtype=BF16, act=_sigmoid, name="merge_gates")
    merged = _merge_up(oa, ob, oc, gates, w["w_up_a"], w["w_up_b"], w["w_up_c"], l, tm=tm_big, tn=512)
    x = _matmul_ws(merged, w["w_out"], l, tm=tm_big, tn=512, residual=x, name="out_proj")
    xf = _rmsnorm(x, lw["g_ffn"], BF16)
    hid = _ffn_up(xf, w["w_ffn_gate"], w["w_ffn_up"], l, tm=_pick_tile(m, 1728), tn=256)
    d_ff = hid.shape[1]
    x = _matmul_ksplit_res(hid, w["w_ffn_down"], l, x, tm=tm, tn=512, tk=d_ff // 2, name="ffn_down")

    win_p, win_s = [], []
    for g in range(N_GROUPS_C):
        off_k = OFF_C + g * 3 * V_C + V_C
        keep = min(BAND * DILATIONS[g], seq)
        kv_p = jnp.stack([lax.slice(h, ((b + 1) * seq - keep, off_k), ((b + 1) * seq, off_k + 2 * V_C))
                          for b in range(n_p)])
        win_p.append(kv_p.reshape(n_p, keep, 2, G_C, HD_C).transpose(0, 2, 1, 3, 4))
        kv_s = lax.slice(h, (m_p, off_k), (m, off_k + 2 * V_C)).reshape(n_s, SEQ_PAD, 2, G_C, HD_C)[:, :1]
        win_s.append(jnp.concatenate([caches[g][:, :, 1:], kv_s.transpose(0, 2, 1, 3, 4)], axis=2))
    return x, (sa_p, sa_s, sb_p, sb_s), win_p, win_s


def kernel(x_prompt, x_sample, state_gla, state_ret, cache_win0, cache_win1, cache_win2, g_mix, w_in, w_alpha2,
           b_alpha, g_gla, w_merge, w_up_a, w_up_b, w_up_c, w_out, g_ffn, w_ffn_gate, w_ffn_up, w_ffn_down, g_final):
    n_p, seq, d = x_prompt.shape
    n_s = x_sample.shape[0]
    depth = g_mix.shape[0]
    assert d == D_MODEL and x_sample.shape[1] == 1 and seq % (BAND * DILATIONS[-1]) == 0 and seq % RET_ROWS == 0
    m_p = n_p * seq
    m = m_p + n_s * SEQ_PAD
    dims = (n_p, seq, n_s, m_p, m)

    xs = jnp.pad(x_sample, ((0, 0), (0, SEQ_PAD - 1), (0, 0)))
    x = jnp.concatenate([x_prompt.reshape(m_p, d), xs.reshape(n_s * SEQ_PAD, d)], axis=0)

    cos_p, sin_p = _rope_tables(np.arange(seq))
    cos_s, sin_s = _rope_tables(PAST_LEN + np.arange(SEQ_PAD))
    log_gamma = np.log1p(-(2.0 ** (-5.0 - np.arange(H_B, dtype=np.float64))))
    heads = np.arange(1, N_GROUPS_C * G_C + 1, dtype=np.float64)
    tabs = {
        "cos_p": cos_p, "sin_p": sin_p, "cos_s": cos_s, "sin_s": sin_s,
        "lg": jnp.asarray(np.broadcast_to(log_gamma[:, None], (H_B, RET_ROWS)), F32),
        "slopes": jnp.asarray(2.0 ** (-8.0 * heads / (N_GROUPS_C * G_C)), F32),
    }

    assert w_in.shape[2] == N_MAIN + GATE_RANK
    w = {"w_in_t": jnp.swapaxes(w_in, 1, 2), "w_merge": w_merge, "w_up_a": w_up_a, "w_up_b": w_up_b, "w_up_c": w_up_c,
         "w_out": w_out, "w_ffn_gate": w_ffn_gate, "w_ffn_up": w_ffn_up, "w_ffn_down": w_ffn_down.astype(BF16)}
    caches = (cache_win0, cache_win1, cache_win2)
    outs = {k: [] for k in ("gla_p", "gla_s", "ret_p", "ret_s")}
    win_p = [[] for _ in range(N_GROUPS_C)]
    win_s = [[] for _ in range(N_GROUPS_C)]
    for l in range(depth):
        lw = {
            "g_mix": g_mix[l], "g_gla": g_gla[l], "g_ffn": g_ffn[l], "b_alpha": b_alpha[l],
            "w_alpha2": w_alpha2[l],
        }
        x, (sa_p, sa_s, sb_p, sb_s), wp, ws = _layer(x, l, w, lw, state_gla[l], state_ret[l], [c[l] for c in caches],
                                                     tabs, dims)
        outs["gla_p"].append(sa_p)
        outs["gla_s"].append(sa_s)
        outs["ret_p"].append(sb_p)
        outs["ret_s"].append(sb_s)
        for g in range(N_GROUPS_C):
            win_p[g].append(wp[g])
            win_s[g].append(ws[g])

    y_prompt = _rmsnorm(x, g_final, F32, rows=m_p).reshape(n_p, seq, d)
    y_sample = _rmsnorm(x[m_p:], g_final, F32).reshape(n_s, SEQ_PAD, d)[:, :1]
    return (y_prompt, y_sample, jnp.stack(outs["gla_p"]), jnp.stack(outs["gla_s"]), jnp.stack(outs["ret_p"]),
            jnp.stack(outs["ret_s"]), jnp.stack(win_p[0]), jnp.stack(win_s[0]), jnp.stack(win_p[1]),
            jnp.stack(win_s[1]), jnp.stack(win_p[2]), jnp.stack(win_s[2]))
```

```python
import functools

import numpy as np
import jax
import jax.numpy as jnp
from jax import lax
from jax.experimental import pallas as pl
from jax.experimental.pallas import tpu as pltpu

F32 = jnp.float32
BF16 = jnp.bfloat16
HIGHEST = lax.Precision.HIGHEST

D_MODEL = 4096
H_A, DK_A, DV_A = 4, 256, 512
GATE_RANK, GATE_TAU = 16, 16.0
H_B, DK_B, DV_B = 8, 128, 256
ROPE_BASE = 10000.0
DILATIONS = (1, 4, 16)
N_GROUPS_C, G_C, HD_C = 3, 4, 128
BAND = 128
PAST_LEN = 16384
QK_A, V_A = H_A * DK_A, H_A * DV_A
QK_B, V_B = H_B * DK_B, H_B * DV_B
V_C = G_C * HD_C
C_COLS = N_GROUPS_C * 3 * V_C
EPS = 1e-6
NEG = -1e30

OFF_QA, OFF_KA, OFF_VA, OFF_RA = 0, QK_A, 2 * QK_A, 2 * QK_A + V_A
OFF_B = 2 * QK_A + 2 * V_A
OFF_QB, OFF_KB, OFF_VB, OFF_GB = OFF_B, OFF_B + QK_B, OFF_B + 2 * QK_B, OFF_B + 2 * QK_B + V_B
OFF_C = OFF_B + 2 * QK_B + 2 * V_B
N_MAIN = OFF_C + C_COLS

LANES = 128
SEQ_PAD = 16
GLA_ROWS = 128
RET_ROWS = 256
RET_HEADS = 4
SUBLANES = 8
VMEM_CAP = 56 * 1024 * 1024


def _vmem(nbytes):
    return int(min(VMEM_CAP, nbytes + (6 << 20)))


def _params(sem, nbytes):
    return pltpu.CompilerParams(dimension_semantics=sem, vmem_limit_bytes=_vmem(nbytes))


def _pick_tile(m, cap, align=16):
    best = align
    for t in range(align, cap + 1, align):
        if m % t == 0:
            best = t
    assert m % best == 0, (m, cap, align)
    return best


def _iota(shape, dim):
    return lax.broadcasted_iota(jnp.int32, shape, dim)


def _dot(a, b):
    return jnp.dot(a, b, preferred_element_type=F32)


def _dot_nt(a, b):
    return lax.dot_general(a, b, (((1,), (1,)), ((), ())), preferred_element_type=F32)


def _dot_tn(a, b):
    return lax.dot_general(a, b, (((0,), (0,)), ((), ())), preferred_element_type=F32)


def _sigmoid(x):
    return 1.0 / (1.0 + jnp.exp(-x))


def _rmsnorm_body(x_ref, g_ref, o_ref):
    x = x_ref[...]
    ms = jnp.mean(x * x, axis=-1, keepdims=True)
    o_ref[...] = (x * lax.rsqrt(ms + EPS) * g_ref[...]).astype(o_ref.dtype)


def _rmsnorm(x, g, out_dtype, rows=None):
    m, d = x.shape
    m = m if rows is None else rows
    tm = _pick_tile(m, 512)
    nbytes = 2 * tm * d * (4 + jnp.dtype(out_dtype).itemsize) + 2 * tm * d * 4
    return pl.pallas_call(
        _rmsnorm_body,
        out_shape=jax.ShapeDtypeStruct((m, d), out_dtype),
        grid=(m // tm,),
        in_specs=[pl.BlockSpec((tm, d), lambda i: (i, 0)), pl.BlockSpec((1, d), lambda i: (0, 0))],
        out_specs=pl.BlockSpec((tm, d), lambda i: (i, 0)),
        compiler_params=_params(("parallel",), nbytes),
        name="rmsnorm",
    )(x, g.reshape(1, d))


def _mm_ws_body(*refs, act, has_res, transposed):
    x_ref, w_ref = refs[:2]
    r_ref = refs[2] if has_res else None
    o_ref = refs[-1]
    if transposed:
        y = _dot_nt(x_ref[...], w_ref[0].astype(BF16))
    else:
        y = _dot(x_ref[...], w_ref[...].astype(BF16))
    if act is not None:
        y = act(y)
    if has_res:
        y = r_ref[...] + y
    o_ref[...] = y.astype(o_ref.dtype)


def _matmul_ws(x, w, layer, *, tm, tn, n=None, row_of=None, out_dtype=F32, act=None, residual=None, name="matmul"):
    m, k = x.shape
    transposed = row_of is not None
    n = w.shape[2] if n is None else n
    assert m % tm == 0 and n % tn == 0
    if transposed:
        w_spec = pl.BlockSpec((pl.Element(1), pl.Element(tn), pl.Element(k)),
                              lambda j, i: (layer, pl.multiple_of(row_of(j), SUBLANES), 0))
    else:
        w_spec = pl.BlockSpec((None, k, tn), lambda j, i: (layer, 0, j))
    in_specs = [pl.BlockSpec((tm, k), lambda j, i: (i, 0)), w_spec]
    args = [x, w]
    osz = jnp.dtype(out_dtype).itemsize
    nbytes = 2 * (tm * k * 2 + k * tn * 4 + tm * tn * osz) + k * tn * 2 + 2 * tm * tn * 4
    if residual is not None:
        in_specs.append(pl.BlockSpec((tm, tn), lambda j, i: (i, j)))
        args.append(residual)
        nbytes += 2 * tm * tn * 4
    return pl.pallas_call(
        functools.partial(_mm_ws_body, act=act, has_res=residual is not None, transposed=transposed),
        out_shape=jax.ShapeDtypeStruct((m, n), out_dtype),
        grid=(n // tn, m // tm),
        in_specs=in_specs,
        out_specs=pl.BlockSpec((tm, tn), lambda j, i: (i, j)),
        compiler_params=_params(("parallel", "parallel"), nbytes),
        name=name,
    )(*args)


def _in_proj(x, w_in_t, layer, *, tm, tn):
    assert OFF_B % tn == 0
    return _matmul_ws(x, w_in_t, layer, tm=tm, tn=tn, n=N_MAIN, name="in_proj",
                      row_of=lambda j: j * tn + jnp.where(j >= OFF_B // tn, GATE_RANK, 0))


def _mm_ksplit_res_body(x_ref, w_ref, r_ref, o_ref, acc_ref, *, tm, n_k):
    kk = pl.program_id(1)
    rows = pl.ds(pl.multiple_of(pl.program_id(2) * tm, tm), tm)
    part = _dot(x_ref[...], w_ref[...])

    @pl.when(kk == 0)
    def _():
        acc_ref[rows, :] = r_ref[...] + part

    @pl.when(jnp.logical_and(kk > 0, kk < n_k - 1))
    def _():
        acc_ref[rows, :] += part

    @pl.when(kk == n_k - 1)
    def _():
        o_ref[...] = acc_ref[rows, :] + part


def _matmul_ksplit_res(x, w, layer, residual, *, tm, tn, tk, name):
    m, k = x.shape
    n = w.shape[2]
    assert m % tm == 0 and n % tn == 0 and k % tk == 0 and k // tk >= 2
    n_k, n_i = k // tk, m // tm
    nbytes = 2 * (tm * tk * 2 + tk * tn * 2 + 2 * tm * tn * 4) + m * tn * 4 + 2 * tm * tn * 4
    return pl.pallas_call(
        functools.partial(_mm_ksplit_res_body, tm=tm, n_k=n_k),
        out_shape=jax.ShapeDtypeStruct((m, n), F32),
        grid=(n // tn, n_k, n_i),
        in_specs=[
            pl.BlockSpec((tm, tk), lambda j, kk, i: (i, kk)),
            pl.BlockSpec((None, tk, tn), lambda j, kk, i: (layer, kk, j)),
            pl.BlockSpec((tm, tn), lambda j, kk, i: (jnp.where(kk == 0, i, n_i - 1), j)),
        ],
        out_specs=pl.BlockSpec((tm, tn), lambda j, kk, i: (jnp.where(kk == n_k - 1, i, 0), j)),
        scratch_shapes=[pltpu.VMEM((m, tn), F32)],
        compiler_params=_params(("parallel", "arbitrary", "arbitrary"), nbytes),
        name=name,
    )(x, w, residual)


def _merge_up_body(oa_ref, ob_ref, oc_ref, ga_ref, gb_ref, gc_ref, wa_ref, wb_ref, wc_ref, o_ref):
    acc = ga_ref[...].astype(F32) * _dot(oa_ref[...], wa_ref[...].astype(BF16))
    acc += gb_ref[...].astype(F32) * _dot(ob_ref[...], wb_ref[...].astype(BF16))
    acc += gc_ref[...].astype(F32) * _dot(oc_ref[...], wc_ref[...].astype(BF16))
    o_ref[...] = acc.astype(o_ref.dtype)


def _merge_up(oa, ob, oc, gates, w_up_a, w_up_b, w_up_c, layer, *, tm, tn):
    m = oa.shape[0]
    d = w_up_a.shape[2]
    nt = d // tn
    assert m % tm == 0 and d % tn == 0

    def row(width):
        return pl.BlockSpec((tm, width), lambda j, i: (i, 0))

    def gate(branch):
        return pl.BlockSpec((tm, tn), lambda j, i: (i, branch * nt + j))

    def panel(kdim):
        return pl.BlockSpec((None, kdim, tn), lambda j, i: (layer, 0, j))

    kk = V_A + V_B + V_C
    nbytes = 2 * (tm * kk * 2 + 3 * tm * tn * 2 + kk * tn * 4 + tm * tn * 2) + kk * tn * 2 + 6 * tm * tn * 4
    return pl.pallas_call(
        _merge_up_body,
        out_shape=jax.ShapeDtypeStruct((m, d), BF16),
        grid=(nt, m // tm),
        in_specs=[row(V_A), row(V_B), row(V_C), gate(0), gate(1), gate(2), panel(V_A), panel(V_B), panel(V_C)],
        out_specs=pl.BlockSpec((tm, tn), lambda j, i: (i, j)),
        compiler_params=_params(("parallel", "parallel"), nbytes),
        name="merge_up",
    )(oa, ob, oc, gates, gates, gates, w_up_a, w_up_b, w_up_c)


def _ffn_up_body(x_ref, wg_ref, wu_ref, o_ref):
    x = x_ref[...]
    a = _dot(x, wg_ref[...].astype(BF16))
    o_ref[...] = (a * _sigmoid(a) * _dot(x, wu_ref[...].astype(BF16))).astype(o_ref.dtype)


def _ffn_up(x, wg, wu, layer, *, tm, tn):
    m, k = x.shape
    n = wg.shape[2]
    assert m % tm == 0 and n % tn == 0
    nbytes = 2 * (tm * k * 2 + 2 * k * tn * 4 + tm * tn * 2) + 2 * k * tn * 2 + 3 * tm * tn * 4
    return pl.pallas_call(
        _ffn_up_body,
        out_shape=jax.ShapeDtypeStruct((m, n), BF16),
        grid=(m // tm, n // tn),
        in_specs=[pl.BlockSpec((tm, k), lambda i, j: (i, 0)), pl.BlockSpec((None, k, tn), lambda i, j: (layer, 0, j)),
                  pl.BlockSpec((None, k, tn), lambda i, j: (layer, 0, j))],
        out_specs=pl.BlockSpec((tm, tn), lambda i, j: (i, j)),
        compiler_params=_params(("parallel", "parallel"), nbytes),
        name="ffn_up",
    )(x, wg, wu)


def _gla_body(*refs, rows, n_t, t_valid, heads, has_bsum, has_s0, has_alias):
    it = iter(refs)
    q_ref, k_ref, v_ref, r_ref, la_ref, g_ref = [next(it) for _ in range(6)]
    bsum_ref = next(it) if has_bsum else None
    s0_ref = next(it) if has_s0 else None
    if has_alias:
        next(it)
    o_ref, sout_ref, st_scr, b_scr = [next(it) for _ in range(4)]
    for hh in range(heads):
        kk = slice(hh * DK_A, (hh + 1) * DK_A)
        vv = slice(hh * DV_A, (hh + 1) * DV_A)
        _gla_head(q_ref.at[:, kk], k_ref.at[:, kk], v_ref.at[:, vv], r_ref.at[:, vv], la_ref.at[:, kk], g_ref,
                  None if bsum_ref is None else bsum_ref.at[:, kk], None if s0_ref is None else s0_ref.at[hh],
                  o_ref.at[:, vv], sout_ref.at[hh], st_scr.at[hh], b_scr.at[hh], rows=rows, n_t=n_t, t_valid=t_valid)


def _gla_head(q_ref, k_ref, v_ref, r_ref, la_ref, g_ref, bsum_ref, s0_ref, o_ref, sout_ref, st_scr, b_scr, *,
              rows, n_t, t_valid):
    t = pl.program_id(1)

    @pl.when(t == 0)
    def _():
        if s0_ref is not None:
            st_scr[...] = s0_ref[...].T
        else:
            st_scr[...] = jnp.zeros_like(st_scr)

    la = la_ref[...]
    row_k = _iota((rows, DK_A), 0)
    if t_valid is not None:
        la = jnp.where(row_k < t_valid, la, 0.0)
    if bsum_ref is not None:
        b = bsum_ref[...]
    else:
        tri = (_iota((rows, rows), 0) >= _iota((rows, rows), 1)).astype(F32)
        b = jnp.dot(tri, la, precision=HIGHEST, preferred_element_type=F32)
    b_scr[...] = b

    q = q_ref[...] * (DK_A ** -0.5)
    k = k_ref[...]
    v16 = v_ref[...].astype(BF16)
    st = st_scr[...]
    o = _dot_nt((q * jnp.exp(b)).astype(BF16), st.astype(BF16))

    t_idx = _iota((rows, rows), 0)
    s_idx = _iota((rows, rows), 1)
    scores = jnp.where(t_idx == s_idx, _dot_nt(q.astype(BF16), k.astype(BF16)), 0.0)
    m = rows // 2
    while m >= 1:
        sub = row_k & (2 * m - 1)
        if m >= 4:
            pieces = [jnp.broadcast_to(b_scr[pl.ds(p * 2 * m + m - 1, 1), :], (2 * m, DK_A)) for p in range(rows // (2 * m))]
            log_fac = -jnp.abs(b - (pieces[0] if len(pieces) == 1 else jnp.concatenate(pieces, axis=0)))
        elif m == 2:
            la_next = pltpu.roll(la, rows - 1, 0)
            la_prev = pltpu.roll(la, 1, 0)
            log_fac = jnp.where(sub == 0, la_next, jnp.where(sub == 1, 0.0, jnp.where(sub == 2, la, la + la_prev)))
        else:
            log_fac = jnp.where(sub == 1, la, 0.0)
        fac = jnp.exp(log_fac)
        upper = sub >= m
        qs = jnp.where(upper, q * fac, 0.0)
        ks = jnp.where(upper, 0.0, k * fac)
        part = _dot_nt(qs.astype(BF16), ks.astype(BF16))
        if 2 * m < rows:
            shift = (2 * m).bit_length() - 1
            part = jnp.where((t_idx >> shift) == (s_idx >> shift), part, 0.0)
        scores = scores + part
        m //= 2
    o = o + _dot(scores.astype(BF16), v16)

    b_last = b_scr[pl.ds(rows - 1, 1), :]
    k_st = k if t_valid is None else jnp.where(row_k < t_valid, k, 0.0)
    st_new = st * jnp.exp(b_last) + _dot_tn(v16, (k_st * jnp.exp(b_last - b)).astype(BF16))
    st_scr[...] = st_new

    @pl.when(t == n_t - 1)
    def _():
        sout_ref[...] = st_new.T

    ms = jnp.mean(o * o, axis=-1, keepdims=True)
    r = r_ref[...]
    o_ref[...] = (r * _sigmoid(r) * (o * lax.rsqrt(ms + EPS) * g_ref[...])).astype(o_ref.dtype)


def _gla(h, la, bsum, g_gla, *, n_seq, seq_rows, rows, row0, t_valid, state0, out_rows, alias, heads=H_A):
    n_t = seq_rows // rows
    rb0 = row0 // rows
    groups = H_A // heads
    wk, wv = heads * DK_A, heads * DV_A

    def rmap(blk0):
        return lambda s, t: (rb0 + (s // groups) * n_t + t, blk0 + s % groups)

    smap = lambda s, t: (s // groups, s % groups, 0, 0)
    in_specs = [
        pl.BlockSpec((rows, wk), rmap(OFF_QA // wk)),
        pl.BlockSpec((rows, wk), rmap(OFF_KA // wk)),
        pl.BlockSpec((rows, wv), rmap(OFF_VA // wv)),
        pl.BlockSpec((rows, wv), rmap(OFF_RA // wv)),
        pl.BlockSpec((rows, wk), rmap(0)),
        pl.BlockSpec((1, DV_A), lambda s, t: (0, 0)),
    ]
    args = [h, h, h, h, la, g_gla.reshape(1, DV_A)]
    if bsum is not None:
        in_specs.append(pl.BlockSpec((rows, wk), rmap(0)))
        args.append(bsum)
    if state0 is not None:
        in_specs.append(pl.BlockSpec((None, heads, DK_A, DV_A), smap))
        args.append(state0)
    aliases = {}
    if alias is not None:
        in_specs.append(pl.BlockSpec(memory_space=pl.ANY))
        aliases = {len(args): 0}
        args.append(alias)
    nbytes = heads * (2 * rows * (4 * DK_A + 2 * DV_A) * 4 + 2 * rows * DV_A * 2 + 5 * DK_A * DV_A * 4
                      + rows * DK_A * 4 + 16 * rows * DV_A * 4)
    return pl.pallas_call(
        functools.partial(_gla_body, rows=rows, n_t=n_t, t_valid=t_valid, heads=heads, has_bsum=bsum is not None,
                          has_s0=state0 is not None, has_alias=alias is not None),
        out_shape=(jax.ShapeDtypeStruct((out_rows, V_A), BF16), jax.ShapeDtypeStruct((n_seq, H_A, DK_A, DV_A), F32)),
        grid=(n_seq * groups, n_t),
        in_specs=in_specs,
        out_specs=(pl.BlockSpec((rows, wv), rmap(0)), pl.BlockSpec((None, heads, DK_A, DV_A), smap)),
        scratch_shapes=[pltpu.VMEM((heads, DV_A, DK_A), F32), pltpu.VMEM((heads, rows, DK_A), F32)],
        input_output_aliases=aliases,
        compiler_params=_params(("parallel", "arbitrary"), nbytes),
        name="gla",
    )(*args)


def _gate_body(x_ref, w_ref, wa_hi_ref, wa_lo_ref, ba_ref, la_ref, bsum_ref, *, block):
    za = _dot_nt(x_ref[...], w_ref[0].astype(BF16))
    za_hi = za.astype(BF16)
    za_lo = (za - za_hi.astype(F32)).astype(BF16)
    wa_hi = wa_hi_ref[...]
    z = _dot(za_hi, wa_hi) + (_dot(za_lo, wa_hi) + _dot(za_hi, wa_lo_ref[...])) + ba_ref[...]
    la = (jnp.minimum(z, 0.0) - jnp.log(1.0 + jnp.exp(-jnp.abs(z)))) * (1.0 / GATE_TAU)
    la_ref[...] = la
    tri = (_iota((block, block), 0) >= _iota((block, block), 1)).astype(BF16)
    la_hi = la.astype(BF16)
    rest = la - la_hi.astype(F32)
    la_mid = rest.astype(BF16)
    la_lo = (rest - la_mid.astype(F32)).astype(BF16)
    for p in range(x_ref.shape[0] // block):
        rows = slice(p * block, (p + 1) * block)
        bsum_ref[rows, :] = _dot(tri, la_hi[rows]) + (_dot(tri, la_mid[rows]) + _dot(tri, la_lo[rows]))


def _gate(x, w_in_t, layer, w_alpha2, b_alpha, *, tm, block):
    m, k = x.shape
    assert m % tm == 0 and tm % block == 0 and OFF_B % SUBLANES == 0
    wa = jnp.pad(w_alpha2, ((0, LANES - GATE_RANK), (0, 0)))
    wa_hi = wa.astype(BF16)
    wa_lo = (wa - wa_hi.astype(F32)).astype(BF16)
    full = lambda shape: pl.BlockSpec(shape, lambda i: (0, 0))
    nbytes = 2 * (tm * k * 2 + k * LANES * 4 + 2 * tm * QK_A * 4) + k * LANES * 2 + 8 * tm * QK_A * 4
    return pl.pallas_call(
        functools.partial(_gate_body, block=block),
        out_shape=(jax.ShapeDtypeStruct((m, QK_A), F32), jax.ShapeDtypeStruct((m, QK_A), F32)),
        grid=(m // tm,),
        in_specs=[pl.BlockSpec((tm, k), lambda i: (i, 0)),
                  pl.BlockSpec((pl.Element(1), pl.Element(LANES), pl.Element(k)), lambda i: (layer, OFF_B, 0)),
                  full((LANES, QK_A)), full((LANES, QK_A)), full((1, QK_A))],
        out_specs=(pl.BlockSpec((tm, QK_A), lambda i: (i, 0)), pl.BlockSpec((tm, QK_A), lambda i: (i, 0))),
        compiler_params=_params(("parallel",), nbytes),
        name="gla_gate",
    )(x, w_in_t, wa_hi, wa_lo, b_alpha.reshape(1, QK_A))


def _ret_body(*refs, rows, n_t, t_valid, heads, has_s0, has_alias):
    it = iter(refs)
    q_ref, k_ref, v_ref, g_ref, cos_ref, sin_ref, lg_ref = [next(it) for _ in range(7)]
    s0_ref = next(it) if has_s0 else None
    if has_alias:
        next(it)
    o_ref, sout_ref, st_scr, rel_scr, dq_scr, dk_scr = [next(it) for _ in range(6)]
    head0 = (pl.program_id(0) % (H_B // heads)) * heads
    for hh in range(heads):
        kk = slice(hh * DK_B, (hh + 1) * DK_B)
        vv = slice(hh * DV_B, (hh + 1) * DV_B)
        _ret_head(q_ref.at[:, kk], k_ref.at[:, kk], v_ref.at[:, vv], g_ref.at[:, vv], cos_ref, sin_ref,
                  lg_ref.at[pl.ds(head0 + hh, 1), :], None if s0_ref is None else s0_ref.at[hh], o_ref.at[:, vv],
                  sout_ref.at[hh], st_scr.at[hh], rel_scr.at[hh], dq_scr.at[hh], dk_scr.at[hh],
                  rows=rows, n_t=n_t, t_valid=t_valid)


def _ret_head(q_ref, k_ref, v_ref, g_ref, cos_ref, sin_ref, lg_ref, s0_ref, o_ref, sout_ref, st_scr, rel_scr, dq_scr,
              dk_scr, *, rows, n_t, t_valid):
    t = pl.program_id(1)
    lg = lg_ref[...]
    n_valid = rows if t_valid is None else t_valid
    row_k = _iota((rows, DK_B), 0)

    @pl.when(t == 0)
    def _():
        if s0_ref is not None:
            st_scr[...] = s0_ref[...]
        else:
            st_scr[...] = jnp.zeros_like(st_scr)
        steps = jnp.minimum(row_k + 1, n_valid).astype(F32)
        dq_scr[...] = jnp.exp(lg[:, :DK_B] * steps)
        dk_scr[...] = jnp.exp(lg[:, :DK_B] * (n_valid - steps))
        st = jnp.minimum(_iota((rows, rows), 0) + 1, n_valid)
        ss = jnp.minimum(_iota((rows, rows), 1) + 1, n_valid)
        causal = _iota((rows, rows), 0) >= _iota((rows, rows), 1)
        rel_scr[...] = jnp.exp(jnp.where(causal, lg[:, :rows] * (st - ss).astype(F32), NEG))

    cos = cos_ref[...]
    sin = sin_ref[...]

    def rot(x):
        return x * cos + pltpu.roll(x, DK_B // 2, 1) * sin

    q = rot(q_ref[...])
    k = rot(k_ref[...]) * (DK_B ** -0.5)
    if t_valid is not None:
        k = jnp.where(row_k < t_valid, k, 0.0)
    decay_q = dq_scr[...]
    decay_k = dk_scr[...]

    v16 = v_ref[...].astype(BF16)
    state = st_scr[...]
    scores = _dot_nt(q.astype(BF16), k.astype(BF16)) * rel_scr[...]
    o = _dot(scores.astype(BF16), v16) + _dot((q * decay_q).astype(BF16), state.astype(BF16))
    state_new = state * jnp.exp(lg[:, :DV_B] * float(n_valid)) + _dot_tn((k * decay_k).astype(BF16), v16)
    st_scr[...] = state_new

    @pl.when(t == n_t - 1)
    def _():
        sout_ref[...] = state_new

    ms = jnp.mean(o * o, axis=-1, keepdims=True)
    g = g_ref[...]
    o_ref[...] = (g * _sigmoid(g) * (o * lax.rsqrt(ms + EPS))).astype(o_ref.dtype)


def _ret(h, cos, sin, lg_tab, *, n_seq, seq_rows, rows, row0, t_valid, state0, out_rows, alias, heads=RET_HEADS):
    n_t = seq_rows // rows
    rb0 = row0 // rows
    groups = H_B // heads
    wk, wv = heads * DK_B, heads * DV_B

    def rmap(blk0):
        return lambda s, t: (rb0 + (s // groups) * n_t + t, blk0 + s % groups)

    smap = lambda s, t: (s // groups, s % groups, 0, 0)
    in_specs = [
        pl.BlockSpec((rows, wk), rmap(OFF_QB // wk)),
        pl.BlockSpec((rows, wk), rmap(OFF_KB // wk)),
        pl.BlockSpec((rows, wv), rmap(OFF_VB // wv)),
        pl.BlockSpec((rows, wv), rmap(OFF_GB // wv)),
        pl.BlockSpec((rows, DK_B), lambda s, t: (t, 0)),
        pl.BlockSpec((rows, DK_B), lambda s, t: (t, 0)),
        pl.BlockSpec((H_B, RET_ROWS), lambda s, t: (0, 0)),
    ]
    args = [h, h, h, h, cos, sin, lg_tab]
    if state0 is not None:
        in_specs.append(pl.BlockSpec((None, heads, DK_B, DV_B), smap))
        args.append(state0)
    aliases = {}
    if alias is not None:
        in_specs.append(pl.BlockSpec(memory_space=pl.ANY))
        aliases = {len(args): 0}
        args.append(alias)
    nbytes = heads * (2 * rows * (2 * DK_B + 2 * DV_B) * 4 + 2 * rows * DV_B * 2 + 5 * DK_B * DV_B * 4
                      + 6 * rows * rows * 4 + 8 * rows * DV_B * 4) + 4 * rows * DK_B * 4
    return pl.pallas_call(
        functools.partial(_ret_body, rows=rows, n_t=n_t, t_valid=t_valid, heads=heads, has_s0=state0 is not None,
                          has_alias=alias is not None),
        out_shape=(jax.ShapeDtypeStruct((out_rows, V_B), BF16), jax.ShapeDtypeStruct((n_seq, H_B, DK_B, DV_B), F32)),
        grid=(n_seq * groups, n_t),
        in_specs=in_specs,
        out_specs=(pl.BlockSpec((rows, wv), rmap(0)), pl.BlockSpec((None, heads, DK_B, DV_B), smap)),
        scratch_shapes=[pltpu.VMEM((heads, DK_B, DV_B), F32), pltpu.VMEM((heads, rows, rows), F32),
                        pltpu.VMEM((heads, rows, DK_B), F32), pltpu.VMEM((heads, rows, DK_B), F32)],
        input_output_aliases=aliases,
        compiler_params=_params(("parallel", "arbitrary"), nbytes),
        name="retention",
    )(*args)


def _dil_prefill_body(slopes_ref, *refs, seq):
    qkv = refs[:9]
    o_ref, og_scr, lse_scr = refs[9:]
    slot = pl.program_id(1)
    scale = HD_C ** -0.5
    rr = _iota((BAND, BAND), 0)
    cc = _iota((BAND, BAND), 1)
    dist_cur = (rr - cc).astype(F32)
    ok_cur = rr >= cc
    ok_prev = cc >= rr
    n_units = seq // BAND
    for g, d in enumerate(DILATIONS):
        q_ref, k_ref, v_ref = qkv[3 * g:3 * g + 3]
        pen = slopes_ref[g * G_C + slot] * float(d)
        shift = d.bit_length() - 1

        def rows_of(start, d=d):
            return pl.ds(start, BAND) if d == 1 else pl.ds(start, BAND, stride=d)

        has_prev = seq // (BAND * d) > 1

        def unit(u, carry, q_ref=q_ref, k_ref=k_ref, v_ref=v_ref, pen=pen, d=d, shift=shift, rows_of=rows_of, g=g,
                 has_prev=has_prev):
            blk = u >> shift
            start = (u & (d - 1)) + blk * (BAND * d)
            q = q_ref[rows_of(start), :].astype(BF16)
            keys = k_ref[rows_of(start), :].astype(BF16)
            vals = v_ref[rows_of(start), :].astype(BF16)
            if has_prev:
                prev = jnp.maximum(start - BAND * d, 0)
                keys = jnp.concatenate([k_ref[rows_of(prev), :].astype(BF16), keys], axis=0)
                vals = jnp.concatenate([v_ref[rows_of(prev), :].astype(BF16), vals], axis=0)
                bias = jnp.concatenate([jnp.where(ok_prev, -pen * (dist_cur + float(BAND)), NEG)
                                        + jnp.where(blk > 0, 0.0, NEG),
                                        jnp.where(ok_cur, -pen * dist_cur, NEG)], axis=1)
            else:
                bias = jnp.where(ok_cur, -pen * dist_cur, NEG)
            s = _dot_nt(q, keys) * scale + bias
            m = jnp.max(s, axis=-1, keepdims=True)
            p = jnp.exp(s - m)
            l = jnp.sum(p, axis=-1, keepdims=True)
            acc = _dot(p.astype(BF16), vals)
            og_scr[g, rows_of(start), :] = acc / l
            lse_scr[g, rows_of(start), :] = jnp.broadcast_to(m + jnp.log(l), (BAND, HD_C))
            return carry

        lax.fori_loop(0, n_units, unit, 0, unroll=2)

    lse = [lse_scr[g] for g in range(N_GROUPS_C)]
    top = jnp.maximum(jnp.maximum(lse[0], lse[1]), lse[2])
    w = [jnp.exp(x - top) for x in lse]
    total = w[0] + w[1] + w[2]
    o_ref[...] = ((w[0] * og_scr[0] + w[1] * og_scr[1] + w[2] * og_scr[2]) / total).astype(o_ref.dtype)


def _dil_prefill(h, slopes, *, n_seq, seq, out_rows):
    def cmap(g, j):
        return lambda b, s: (b, (OFF_C + g * 3 * V_C + j * V_C) // HD_C + s)

    in_specs = [pl.BlockSpec(memory_space=pltpu.SMEM)]
    in_specs += [pl.BlockSpec((seq, HD_C), cmap(g, j)) for g in range(N_GROUPS_C) for j in range(3)]
    nbytes = 2 * 9 * seq * HD_C * 4 + 2 * seq * HD_C * 2 + 2 * N_GROUPS_C * seq * HD_C * 4 + 8 * seq * HD_C * 4
    return pl.pallas_call(
        functools.partial(_dil_prefill_body, seq=seq),
        out_shape=jax.ShapeDtypeStruct((out_rows, V_C), BF16),
        grid=(n_seq, G_C),
        in_specs=in_specs,
        out_specs=pl.BlockSpec((seq, HD_C), lambda b, s: (b, s)),
        scratch_shapes=[pltpu.VMEM((N_GROUPS_C, seq, HD_C), F32), pltpu.VMEM((N_GROUPS_C, seq, HD_C), F32)],
        compiler_params=_params(("parallel", "parallel"), nbytes),
        name="dilated_prefill",
    )(slopes, *([h] * 9))


def _dil_step_body(slopes_ref, *refs):
    new = refs[:9]
    caches = refs[9:12]
    o_ref = refs[13]
    scale = HD_C ** -0.5
    col = _iota((SEQ_PAD, BAND * G_C), 1)
    col_slot = col & (G_C - 1)
    steps = (BAND - (col >> (G_C.bit_length() - 1))).astype(F32)
    keys = [c[0].reshape(BAND * G_C, HD_C).astype(BF16) for c in caches]
    vals = [c[1].reshape(BAND * G_C, HD_C).astype(BF16) for c in caches]
    for slot in range(G_C):
        lanes = slice(slot * HD_C, (slot + 1) * HD_C)
        scores, self_scores = [], []
        for g, d in enumerate(DILATIONS):
            q_ref, k_ref, _ = new[3 * g:3 * g + 3]
            q = q_ref[:, lanes]
            s = _dot_nt(q.astype(BF16), keys[g]) * scale
            pen = slopes_ref[g * G_C + slot] * float(d)
            scores.append(jnp.where(col_slot == slot, s - pen * steps, NEG))
            self_scores.append(jnp.sum(q * k_ref[:, lanes], axis=-1, keepdims=True) * scale)
        m = self_scores[0]
        for x in self_scores[1:]:
            m = jnp.maximum(m, x)
        for x in scores:
            m = jnp.maximum(m, jnp.max(x, axis=-1, keepdims=True))
        acc = jnp.zeros((SEQ_PAD, HD_C), F32)
        total = jnp.zeros((SEQ_PAD, 1), F32)
        for g in range(N_GROUPS_C):
            p = jnp.exp(scores[g] - m)
            p_self = jnp.exp(self_scores[g] - m)
            total += jnp.sum(p, axis=-1, keepdims=True) + p_self
            acc += _dot(p.astype(BF16), vals[g]) + p_self * new[3 * g + 2][:, lanes]
        o_ref[:, lanes] = (acc / total).astype(o_ref.dtype)


def _dil_step(h, slopes, caches, layer, oc, *, n_seq, row0):
    rb0 = row0 // SEQ_PAD

    def cmap(g, j):
        return lambda b: (rb0 + b, (OFF_C + g * 3 * V_C + j * V_C) // V_C)

    in_specs = [pl.BlockSpec(memory_space=pltpu.SMEM)]
    in_specs += [pl.BlockSpec((SEQ_PAD, V_C), cmap(g, j)) for g in range(N_GROUPS_C) for j in range(3)]
    strided = []
    for c, d in zip(caches, DILATIONS):
        assert c.shape[3] == BAND * d
        strided.append(c.reshape(c.shape[0], c.shape[1], 2, BAND, d, G_C, HD_C))
        in_specs.append(pl.BlockSpec((None, None, 2, BAND, None, G_C, HD_C), lambda b: (layer, b, 0, 0, 0, 0, 0)))
    in_specs.append(pl.BlockSpec(memory_space=pl.ANY))
    nbytes = 2 * N_GROUPS_C * 2 * BAND * V_C * 4 + 64 * SEQ_PAD * BAND * G_C * 4
    return pl.pallas_call(
        _dil_step_body,
        out_shape=jax.ShapeDtypeStruct(oc.shape, oc.dtype),
        grid=(n_seq,),
        in_specs=in_specs,
        out_specs=pl.BlockSpec((SEQ_PAD, V_C), lambda b: (rb0 + b, 0)),
        input_output_aliases={13: 0},
        compiler_params=_params(("parallel",), nbytes),
        name="dilated_step",
    )(slopes, *([h] * 9), *strided, oc)


def _rope_tables(positions):
    half = DK_B // 2
    inv = ROPE_BASE ** (-np.arange(half, dtype=np.float64) / half)
    ang = np.asarray(positions, np.float64)[:, None] * inv[None, :]
    cos = np.concatenate([np.cos(ang), np.cos(ang)], axis=-1)
    sin = np.concatenate([-np.sin(ang), np.sin(ang)], axis=-1)
    return jnp.asarray(cos, F32), jnp.asarray(sin, F32)


def _layer(x, l, w, lw, state_gla, state_ret, caches, tabs, dims):
    n_p, seq, n_s, m_p, m = dims
    tm = _pick_tile(m, 704)
    tm_big = _pick_tile(m, 1100)
    xn = _rmsnorm(x, lw["g_mix"], BF16)
    h = _in_proj(xn, w["w_in_t"], l, tm=tm_big, tn=768)
    la, bsum = _gate(xn, w["w_in_t"], l, lw["w_alpha2"], lw["b_alpha"], tm=_pick_tile(m, 704, GLA_ROWS), block=GLA_ROWS)

    gla = functools.partial(_gla, h, la, g_gla=lw["g_gla"], out_rows=m)
    oa, sa_p = gla(bsum, n_seq=n_p, seq_rows=seq, rows=GLA_ROWS, row0=0, t_valid=None, state0=None, alias=None)
    oa, sa_s = gla(None, n_seq=n_s, seq_rows=SEQ_PAD, rows=SEQ_PAD, row0=m_p, t_valid=1, state0=state_gla, alias=oa)

    ret = functools.partial(_ret, h, out_rows=m)
    ob, sb_p = ret(tabs["cos_p"], tabs["sin_p"], tabs["lg"], n_seq=n_p, seq_rows=seq, rows=RET_ROWS, row0=0,
                   t_valid=None, state0=None, alias=None)
    ob, sb_s = ret(tabs["cos_s"], tabs["sin_s"], tabs["lg"], n_seq=n_s, seq_rows=SEQ_PAD, rows=SEQ_PAD, row0=m_p,
                   t_valid=1, state0=state_ret, alias=ob)

    oc = _dil_prefill(h, tabs["slopes"], n_seq=n_p, seq=seq, out_rows=m)
    oc = _dil_step(h, tabs["slopes"], caches, l, oc, n_seq=n_s, row0=m_p)

    gates = _matmul_ws(xn, w["w_merge"], l, tm=tm_big, tn=768, out_dtype=BF16, act=_sigmoid, name="merge_gates")
    merged = _merge_up(oa, ob, oc, gates, w["w_up_a"], w["w_up_b"], w["w_up_c"], l, tm=tm_big, tn=512)
    x = _matmul_ws(merged, w["w_out"], l, tm=tm_big, tn=512, residual=x, name="out_proj")
    xf = _rmsnorm(x, lw["g_ffn"], BF16)
    hid = _ffn_up(xf, w["w_ffn_gate"], w["w_ffn_up"], l, tm=_pick_tile(m, 1728), tn=256)
    d_ff = hid.shape[1]
    x = _matmul_ksplit_res(hid, w["w_ffn_down"], l, x, tm=tm, tn=512, tk=d_ff // 2, name="ffn_down")

    win_p, new_s = [], []
    for g in range(N_GROUPS_C):
        off_k = OFF_C + g * 3 * V_C + V_C
        keep = min(BAND * DILATIONS[g], seq)
        kv_p = jnp.stack([lax.slice(h, ((b + 1) * seq - keep, off_k), ((b + 1) * seq, off_k + 2 * V_C))
                          for b in range(n_p)])
        win_p.append(kv_p.reshape(n_p, keep, 2, G_C, HD_C).transpose(0, 2, 1, 3, 4))
        kv_s = lax.slice(h, (m_p, off_k), (m, off_k + 2 * V_C)).reshape(n_s, SEQ_PAD, 2, G_C, HD_C)[:, :1]
        new_s.append(kv_s.transpose(0, 2, 1, 3, 4))
    return x, (sa_p, sa_s, sb_p, sb_s), win_p, new_s


def kernel(x_prompt, x_sample, state_gla, state_ret, cache_win0, cache_win1, cache_win2, g_mix, w_in, w_alpha2,
           b_alpha, g_gla, w_merge, w_up_a, w_up_b, w_up_c, w_out, g_ffn, w_ffn_gate, w_ffn_up, w_ffn_down, g_final):
    n_p, seq, d = x_prompt.shape
    n_s = x_sample.shape[0]
    depth = g_mix.shape[0]
    assert d == D_MODEL and x_sample.shape[1] == 1 and seq % (BAND * DILATIONS[-1]) == 0 and seq % RET_ROWS == 0
    m_p = n_p * seq
    m = m_p + n_s * SEQ_PAD
    dims = (n_p, seq, n_s, m_p, m)

    xs = jnp.pad(x_sample, ((0, 0), (0, SEQ_PAD - 1), (0, 0)))
    x = jnp.concatenate([x_prompt.reshape(m_p, d), xs.reshape(n_s * SEQ_PAD, d)], axis=0)

    cos_p, sin_p = _rope_tables(np.arange(seq))
    cos_s, sin_s = _rope_tables(PAST_LEN + np.arange(SEQ_PAD))
    log_gamma = np.log1p(-(2.0 ** (-5.0 - np.arange(H_B, dtype=np.float64))))
    heads = np.arange(1, N_GROUPS_C * G_C + 1, dtype=np.float64)
    tabs = {
        "cos_p": cos_p, "sin_p": sin_p, "cos_s": cos_s, "sin_s": sin_s,
        "lg": jnp.asarray(np.broadcast_to(log_gamma[:, None], (H_B, RET_ROWS)), F32),
        "slopes": jnp.asarray(2.0 ** (-8.0 * heads / (N_GROUPS_C * G_C)), F32),
    }

    assert w_in.shape[2] == N_MAIN + GATE_RANK
    w = {"w_in_t": jnp.swapaxes(w_in, 1, 2), "w_merge": w_merge, "w_up_a": w_up_a, "w_up_b": w_up_b, "w_up_c": w_up_c,
         "w_out": w_out, "w_ffn_gate": w_ffn_gate, "w_ffn_up": w_ffn_up, "w_ffn_down": w_ffn_down.astype(BF16)}
    caches = (cache_win0, cache_win1, cache_win2)
    outs = {k: [] for k in ("gla_p", "gla_s", "ret_p", "ret_s")}
    win_p = [[] for _ in range(N_GROUPS_C)]
    win_s = [[] for _ in range(N_GROUPS_C)]
    for l in range(depth):
        lw = {
            "g_mix": g_mix[l], "g_gla": g_gla[l], "g_ffn": g_ffn[l], "b_alpha": b_alpha[l],
            "w_alpha2": w_alpha2[l],
        }
        x, (sa_p, sa_s, sb_p, sb_s), wp, ws = _layer(x, l, w, lw, state_gla[l], state_ret[l], caches, tabs, dims)
        outs["gla_p"].append(sa_p)
        outs["gla_s"].append(sa_s)
        outs["ret_p"].append(sb_p)
        outs["ret_s"].append(sb_s)
        for g in range(N_GROUPS_C):
            win_p[g].append(wp[g])
            win_s[g].append(ws[g])

    y_prompt = _rmsnorm(x, g_final, F32, rows=m_p).reshape(n_p, seq, d)
    y_sample = _rmsnorm(x[m_p:], g_final, F32).reshape(n_s, SEQ_PAD, d)[:, :1]
    win_s = [jnp.concatenate([caches[g][:, :, :, 1:], jnp.stack(win_s[g])], axis=3) for g in range(N_GROUPS_C)]
    return (y_prompt, y_sample, jnp.stack(outs["gla_p"]), jnp.stack(outs["gla_s"]), jnp.stack(outs["ret_p"]),
            jnp.stack(outs["ret_s"]), jnp.stack(win_p[0]), win_s[0], jnp.stack(win_p[1]), win_s[1],
            jnp.stack(win_p[2]), win_s[2])
```

```python
import functools

import numpy as np
import jax
import jax.numpy as jnp
from jax import lax
from jax.experimental import pallas as pl
from jax.experimental.pallas import tpu as pltpu

F32 = jnp.float32
BF16 = jnp.bfloat16
HIGHEST = lax.Precision.HIGHEST

D_MODEL = 4096
H_A, DK_A, DV_A = 4, 256, 512
GATE_RANK, GATE_TAU = 16, 16.0
H_B, DK_B, DV_B = 8, 128, 256
ROPE_BASE = 10000.0
DILATIONS = (1, 4, 16)
N_GROUPS_C, G_C, HD_C = 3, 4, 128
BAND = 128
PAST_LEN = 16384
QK_A, V_A = H_A * DK_A, H_A * DV_A
QK_B, V_B = H_B * DK_B, H_B * DV_B
V_C = G_C * HD_C
C_COLS = N_GROUPS_C * 3 * V_C
EPS = 1e-6
NEG = -1e30

OFF_QA, OFF_KA, OFF_VA, OFF_RA = 0, QK_A, 2 * QK_A, 2 * QK_A + V_A
OFF_B = 2 * QK_A + 2 * V_A
OFF_QB, OFF_KB, OFF_VB, OFF_GB = OFF_B, OFF_B + QK_B, OFF_B + 2 * QK_B, OFF_B + 2 * QK_B + V_B
OFF_C = OFF_B + 2 * QK_B + 2 * V_B
N_MAIN = OFF_C + C_COLS

LANES = 128
SEQ_PAD = 16
GLA_ROWS = 128
RET_ROWS = 256
RET_HEADS = 8
SUBLANES = 8
VMEM_CAP = 56 * 1024 * 1024


def _vmem(nbytes):
    return int(min(VMEM_CAP, nbytes + (6 << 20)))


def _params(sem, nbytes):
    return pltpu.CompilerParams(dimension_semantics=sem, vmem_limit_bytes=_vmem(nbytes))


def _pick_tile(m, cap, align=16):
    best = align
    for t in range(align, cap + 1, align):
        if m % t == 0:
            best = t
    assert m % best == 0, (m, cap, align)
    return best


def _iota(shape, dim):
    return lax.broadcasted_iota(jnp.int32, shape, dim)


def _dot(a, b):
    return jnp.dot(a, b, preferred_element_type=F32)


def _dot_nt(a, b):
    return lax.dot_general(a, b, (((1,), (1,)), ((), ())), preferred_element_type=F32)


def _dot_tn(a, b):
    return lax.dot_general(a, b, (((0,), (0,)), ((), ())), preferred_element_type=F32)


def _sigmoid(x):
    return 1.0 / (1.0 + jnp.exp(-x))


def _rmsnorm_body(x_ref, g_ref, o_ref):
    x = x_ref[...]
    ms = jnp.mean(x * x, axis=-1, keepdims=True)
    o_ref[...] = (x * lax.rsqrt(ms + EPS) * g_ref[...]).astype(o_ref.dtype)


def _rmsnorm(x, g, out_dtype, rows=None):
    m, d = x.shape
    m = m if rows is None else rows
    tm = _pick_tile(m, 512)
    nbytes = 2 * tm * d * (4 + jnp.dtype(out_dtype).itemsize) + 2 * tm * d * 4
    return pl.pallas_call(
        _rmsnorm_body,
        out_shape=jax.ShapeDtypeStruct((m, d), out_dtype),
        grid=(m // tm,),
        in_specs=[pl.BlockSpec((tm, d), lambda i: (i, 0)), pl.BlockSpec((1, d), lambda i: (0, 0))],
        out_specs=pl.BlockSpec((tm, d), lambda i: (i, 0)),
        compiler_params=_params(("parallel",), nbytes),
        name="rmsnorm",
    )(x, g.reshape(1, d))


def _mm_ws_body(*refs, act, has_res, transposed):
    x_ref, w_ref = refs[:2]
    r_ref = refs[2] if has_res else None
    o_ref = refs[-1]
    if transposed:
        y = _dot_nt(x_ref[...], w_ref[0].astype(BF16))
    else:
        y = _dot(x_ref[...], w_ref[...].astype(BF16))
    if act is not None:
        y = act(y)
    if has_res:
        y = r_ref[...] + y
    o_ref[...] = y.astype(o_ref.dtype)


def _matmul_ws(x, w, layer, *, tm, tn, n=None, row_of=None, out_dtype=F32, act=None, residual=None, name="matmul"):
    m, k = x.shape
    transposed = row_of is not None
    n = w.shape[2] if n is None else n
    assert m % tm == 0 and n % tn == 0
    if transposed:
        w_spec = pl.BlockSpec((pl.Element(1), pl.Element(tn), pl.Element(k)),
                              lambda j, i: (layer, pl.multiple_of(row_of(j), SUBLANES), 0))
    else:
        w_spec = pl.BlockSpec((None, k, tn), lambda j, i: (layer, 0, j))
    in_specs = [pl.BlockSpec((tm, k), lambda j, i: (i, 0)), w_spec]
    args = [x, w]
    osz = jnp.dtype(out_dtype).itemsize
    nbytes = 2 * (tm * k * 2 + k * tn * 4 + tm * tn * osz) + k * tn * 2 + 2 * tm * tn * 4
    if residual is not None:
        in_specs.append(pl.BlockSpec((tm, tn), lambda j, i: (i, j)))
        args.append(residual)
        nbytes += 2 * tm * tn * 4
    return pl.pallas_call(
        functools.partial(_mm_ws_body, act=act, has_res=residual is not None, transposed=transposed),
        out_shape=jax.ShapeDtypeStruct((m, n), out_dtype),
        grid=(n // tn, m // tm),
        in_specs=in_specs,
        out_specs=pl.BlockSpec((tm, tn), lambda j, i: (i, j)),
        compiler_params=_params(("parallel", "parallel"), nbytes),
        name=name,
    )(*args)


def _in_proj(x, w_in_t, layer, *, tm, tn):
    assert OFF_B % tn == 0
    return _matmul_ws(x, w_in_t, layer, tm=tm, tn=tn, n=N_MAIN, name="in_proj",
                      row_of=lambda j: j * tn + jnp.where(j >= OFF_B // tn, GATE_RANK, 0))


def _mm_ksplit_res_body(x_ref, w_ref, r_ref, o_ref, acc_ref, *, tm, n_k):
    kk = pl.program_id(1)
    rows = pl.ds(pl.multiple_of(pl.program_id(2) * tm, tm), tm)
    part = _dot(x_ref[...], w_ref[...])

    @pl.when(kk == 0)
    def _():
        acc_ref[rows, :] = r_ref[...] + part

    @pl.when(jnp.logical_and(kk > 0, kk < n_k - 1))
    def _():
        acc_ref[rows, :] += part

    @pl.when(kk == n_k - 1)
    def _():
        o_ref[...] = acc_ref[rows, :] + part


def _matmul_ksplit_res(x, w, layer, residual, *, tm, tn, tk, name):
    m, k = x.shape
    n = w.shape[2]
    assert m % tm == 0 and n % tn == 0 and k % tk == 0 and k // tk >= 2
    n_k, n_i = k // tk, m // tm
    nbytes = 2 * (tm * tk * 2 + tk * tn * 2 + 2 * tm * tn * 4) + m * tn * 4 + 2 * tm * tn * 4
    return pl.pallas_call(
        functools.partial(_mm_ksplit_res_body, tm=tm, n_k=n_k),
        out_shape=jax.ShapeDtypeStruct((m, n), F32),
        grid=(n // tn, n_k, n_i),
        in_specs=[
            pl.BlockSpec((tm, tk), lambda j, kk, i: (i, kk)),
            pl.BlockSpec((None, tk, tn), lambda j, kk, i: (layer, kk, j)),
            pl.BlockSpec((tm, tn), lambda j, kk, i: (jnp.where(kk == 0, i, n_i - 1), j)),
        ],
        out_specs=pl.BlockSpec((tm, tn), lambda j, kk, i: (jnp.where(kk == n_k - 1, i, 0), j)),
        scratch_shapes=[pltpu.VMEM((m, tn), F32)],
        compiler_params=_params(("parallel", "arbitrary", "arbitrary"), nbytes),
        name=name,
    )(x, w, residual)


def _merge_up_body(oa_ref, ob_ref, oc_ref, ga_ref, gb_ref, gc_ref, wa_ref, wb_ref, wc_ref, o_ref):
    acc = ga_ref[...].astype(F32) * _dot(oa_ref[...], wa_ref[...].astype(BF16))
    acc += gb_ref[...].astype(F32) * _dot(ob_ref[...], wb_ref[...].astype(BF16))
    acc += gc_ref[...].astype(F32) * _dot(oc_ref[...], wc_ref[...].astype(BF16))
    o_ref[...] = acc.astype(o_ref.dtype)


def _merge_up(oa, ob, oc, gates, w_up_a, w_up_b, w_up_c, layer, *, tm, tn):
    m = oa.shape[0]
    d = w_up_a.shape[2]
    nt = d // tn
    assert m % tm == 0 and d % tn == 0

    def row(width):
        return pl.BlockSpec((tm, width), lambda j, i: (i, 0))

    def gate(branch):
        return pl.BlockSpec((tm, tn), lambda j, i: (i, branch * nt + j))

    def panel(kdim):
        return pl.BlockSpec((None, kdim, tn), lambda j, i: (layer, 0, j))

    kk = V_A + V_B + V_C
    nbytes = 2 * (tm * kk * 2 + 3 * tm * tn * 2 + kk * tn * 4 + tm * tn * 2) + kk * tn * 2 + 6 * tm * tn * 4
    return pl.pallas_call(
        _merge_up_body,
        out_shape=jax.ShapeDtypeStruct((m, d), BF16),
        grid=(nt, m // tm),
        in_specs=[row(V_A), row(V_B), row(V_C), gate(0), gate(1), gate(2), panel(V_A), panel(V_B), panel(V_C)],
        out_specs=pl.BlockSpec((tm, tn), lambda j, i: (i, j)),
        compiler_params=_params(("parallel", "parallel"), nbytes),
        name="merge_up",
    )(oa, ob, oc, gates, gates, gates, w_up_a, w_up_b, w_up_c)


def _ffn_up_body(x_ref, wg_ref, wu_ref, o_ref):
    x = x_ref[...]
    a = _dot(x, wg_ref[...].astype(BF16))
    o_ref[...] = (a * _sigmoid(a) * _dot(x, wu_ref[...].astype(BF16))).astype(o_ref.dtype)


def _ffn_up(x, wg, wu, layer, *, tm, tn):
    m, k = x.shape
    n = wg.shape[2]
    assert m % tm == 0 and n % tn == 0
    nbytes = 2 * (tm * k * 2 + 2 * k * tn * 4 + tm * tn * 2) + 2 * k * tn * 2 + 3 * tm * tn * 4
    return pl.pallas_call(
        _ffn_up_body,
        out_shape=jax.ShapeDtypeStruct((m, n), BF16),
        grid=(m // tm, n // tn),
        in_specs=[pl.BlockSpec((tm, k), lambda i, j: (i, 0)), pl.BlockSpec((None, k, tn), lambda i, j: (layer, 0, j)),
                  pl.BlockSpec((None, k, tn), lambda i, j: (layer, 0, j))],
        out_specs=pl.BlockSpec((tm, tn), lambda i, j: (i, j)),
        compiler_params=_params(("parallel", "parallel"), nbytes),
        name="ffn_up",
    )(x, wg, wu)


def _gla_body(*refs, rows, n_t, t_valid, heads, has_bsum, has_s0, has_alias):
    it = iter(refs)
    q_ref, k_ref, v_ref, r_ref, la_ref, g_ref = [next(it) for _ in range(6)]
    bsum_ref = next(it) if has_bsum else None
    s0_ref = next(it) if has_s0 else None
    if has_alias:
        next(it)
    o_ref, sout_ref, st_scr, b_scr = [next(it) for _ in range(4)]
    t = pl.program_id(1)

    @pl.when(t == 0)
    def _():
        for hh in range(heads):
            st_scr[hh] = s0_ref[hh].T if has_s0 else jnp.zeros(st_scr.shape[1:], F32)

    for hh in range(heads):
        kk = slice(hh * DK_A, (hh + 1) * DK_A)
        vv = slice(hh * DV_A, (hh + 1) * DV_A)
        _gla_head(q_ref.at[:, kk], k_ref.at[:, kk], v_ref.at[:, vv], r_ref.at[:, vv], la_ref.at[:, kk], g_ref,
                  None if bsum_ref is None else bsum_ref.at[:, kk], o_ref.at[:, vv], st_scr.at[hh], b_scr.at[hh],
                  rows=rows, t_valid=t_valid)

    @pl.when(t == n_t - 1)
    def _():
        for hh in range(heads):
            sout_ref[hh] = st_scr[hh].T


def _gla_head(q_ref, k_ref, v_ref, r_ref, la_ref, g_ref, bsum_ref, o_ref, st_scr, b_scr, *, rows, t_valid):
    la = la_ref[...]
    row_k = _iota((rows, DK_A), 0)
    if t_valid is not None:
        la = jnp.where(row_k < t_valid, la, 0.0)
    if bsum_ref is not None:
        b = bsum_ref[...]
    else:
        tri = (_iota((rows, rows), 0) >= _iota((rows, rows), 1)).astype(F32)
        b = jnp.dot(tri, la, precision=HIGHEST, preferred_element_type=F32)
    b_scr[...] = b

    q = q_ref[...] * (DK_A ** -0.5)
    k = k_ref[...]
    v16 = v_ref[...].astype(BF16)
    st = st_scr[...]
    o = _dot_nt((q * jnp.exp(b)).astype(BF16), st.astype(BF16))

    t_idx = _iota((rows, rows), 0)
    s_idx = _iota((rows, rows), 1)
    scores = jnp.where(t_idx == s_idx, _dot_nt(q.astype(BF16), k.astype(BF16)), 0.0)
    m = rows // 2
    while m >= 1:
        sub = row_k & (2 * m - 1)
        if m >= 4:
            pieces = [jnp.broadcast_to(b_scr[pl.ds(p * 2 * m + m - 1, 1), :], (2 * m, DK_A)) for p in range(rows // (2 * m))]
            log_fac = -jnp.abs(b - (pieces[0] if len(pieces) == 1 else jnp.concatenate(pieces, axis=0)))
        elif m == 2:
            la_next = pltpu.roll(la, rows - 1, 0)
            la_prev = pltpu.roll(la, 1, 0)
            log_fac = jnp.where(sub == 0, la_next, jnp.where(sub == 1, 0.0, jnp.where(sub == 2, la, la + la_prev)))
        else:
            log_fac = jnp.where(sub == 1, la, 0.0)
        fac = jnp.exp(log_fac)
        upper = sub >= m
        qs = jnp.where(upper, q * fac, 0.0)
        ks = jnp.where(upper, 0.0, k * fac)
        part = _dot_nt(qs.astype(BF16), ks.astype(BF16))
        if 2 * m < rows:
            shift = (2 * m).bit_length() - 1
            part = jnp.where((t_idx >> shift) == (s_idx >> shift), part, 0.0)
        scores = scores + part
        m //= 2
    o = o + _dot(scores.astype(BF16), v16)

    b_last = b_scr[pl.ds(rows - 1, 1), :]
    k_st = k if t_valid is None else jnp.where(row_k < t_valid, k, 0.0)
    st_new = st * jnp.exp(b_last) + _dot_tn(v16, (k_st * jnp.exp(b_last - b)).astype(BF16))
    st_scr[...] = st_new

    ms = jnp.mean(o * o, axis=-1, keepdims=True)
    r = r_ref[...]
    o_ref[...] = (r * _sigmoid(r) * (o * lax.rsqrt(ms + EPS) * g_ref[...])).astype(o_ref.dtype)


def _gla(h, la, bsum, g_gla, *, n_seq, seq_rows, rows, row0, t_valid, state0, out_rows, alias, heads=H_A):
    n_t = seq_rows // rows
    rb0 = row0 // rows
    groups = H_A // heads
    wk, wv = heads * DK_A, heads * DV_A

    def rmap(blk0):
        return lambda s, t: (rb0 + (s // groups) * n_t + t, blk0 + s % groups)

    smap = lambda s, t: (s // groups, s % groups, 0, 0)
    in_specs = [
        pl.BlockSpec((rows, wk), rmap(OFF_QA // wk)),
        pl.BlockSpec((rows, wk), rmap(OFF_KA // wk)),
        pl.BlockSpec((rows, wv), rmap(OFF_VA // wv)),
        pl.BlockSpec((rows, wv), rmap(OFF_RA // wv)),
        pl.BlockSpec((rows, wk), rmap(0)),
        pl.BlockSpec((1, DV_A), lambda s, t: (0, 0)),
    ]
    args = [h, h, h, h, la, g_gla.reshape(1, DV_A)]
    if bsum is not None:
        in_specs.append(pl.BlockSpec((rows, wk), rmap(0)))
        args.append(bsum)
    if state0 is not None:
        in_specs.append(pl.BlockSpec((None, heads, DK_A, DV_A), smap))
        args.append(state0)
    aliases = {}
    if alias is not None:
        in_specs.append(pl.BlockSpec(memory_space=pl.ANY))
        aliases = {len(args): 0}
        args.append(alias)
    nbytes = heads * (2 * rows * (4 * DK_A + 2 * DV_A) * 4 + 2 * rows * DV_A * 2 + 5 * DK_A * DV_A * 4
                      + rows * DK_A * 4 + 16 * rows * DV_A * 4)
    return pl.pallas_call(
        functools.partial(_gla_body, rows=rows, n_t=n_t, t_valid=t_valid, heads=heads, has_bsum=bsum is not None,
                          has_s0=state0 is not None, has_alias=alias is not None),
        out_shape=(jax.ShapeDtypeStruct((out_rows, V_A), BF16), jax.ShapeDtypeStruct((n_seq, H_A, DK_A, DV_A), F32)),
        grid=(n_seq * groups, n_t),
        in_specs=in_specs,
        out_specs=(pl.BlockSpec((rows, wv), rmap(0)), pl.BlockSpec((None, heads, DK_A, DV_A), smap)),
        scratch_shapes=[pltpu.VMEM((heads, DV_A, DK_A), F32), pltpu.VMEM((heads, rows, DK_A), F32)],
        input_output_aliases=aliases,
        compiler_params=_params(("parallel", "arbitrary"), nbytes),
        name="gla",
    )(*args)


def _gate_body(x_ref, w_ref, wa_hi_ref, wa_lo_ref, ba_ref, la_ref, bsum_ref, *, block):
    za = _dot_nt(x_ref[...], w_ref[0].astype(BF16))
    za_hi = za.astype(BF16)
    za_lo = (za - za_hi.astype(F32)).astype(BF16)
    wa_hi = wa_hi_ref[...]
    z = _dot(za_hi, wa_hi) + (_dot(za_lo, wa_hi) + _dot(za_hi, wa_lo_ref[...])) + ba_ref[...]
    la = (jnp.minimum(z, 0.0) - jnp.log(1.0 + jnp.exp(-jnp.abs(z)))) * (1.0 / GATE_TAU)
    la_ref[...] = la
    tri = (_iota((block, block), 0) >= _iota((block, block), 1)).astype(BF16)
    la_hi = la.astype(BF16)
    rest = la - la_hi.astype(F32)
    la_mid = rest.astype(BF16)
    la_lo = (rest - la_mid.astype(F32)).astype(BF16)
    for p in range(x_ref.shape[0] // block):
        rows = slice(p * block, (p + 1) * block)
        bsum_ref[rows, :] = _dot(tri, la_hi[rows]) + (_dot(tri, la_mid[rows]) + _dot(tri, la_lo[rows]))


def _gate(x, w_in_t, layer, w_alpha2, b_alpha, *, tm, block):
    m, k = x.shape
    assert m % tm == 0 and tm % block == 0 and OFF_B % SUBLANES == 0
    wa = jnp.pad(w_alpha2, ((0, LANES - GATE_RANK), (0, 0)))
    wa_hi = wa.astype(BF16)
    wa_lo = (wa - wa_hi.astype(F32)).astype(BF16)
    full = lambda shape: pl.BlockSpec(shape, lambda i: (0, 0))
    nbytes = 2 * (tm * k * 2 + k * LANES * 4 + 2 * tm * QK_A * 4) + k * LANES * 2 + 8 * tm * QK_A * 4
    return pl.pallas_call(
        functools.partial(_gate_body, block=block),
        out_shape=(jax.ShapeDtypeStruct((m, QK_A), F32), jax.ShapeDtypeStruct((m, QK_A), F32)),
        grid=(m // tm,),
        in_specs=[pl.BlockSpec((tm, k), lambda i: (i, 0)),
                  pl.BlockSpec((pl.Element(1), pl.Element(LANES), pl.Element(k)), lambda i: (layer, OFF_B, 0)),
                  full((LANES, QK_A)), full((LANES, QK_A)), full((1, QK_A))],
        out_specs=(pl.BlockSpec((tm, QK_A), lambda i: (i, 0)), pl.BlockSpec((tm, QK_A), lambda i: (i, 0))),
        compiler_params=_params(("parallel",), nbytes),
        name="gla_gate",
    )(x, w_in_t, wa_hi, wa_lo, b_alpha.reshape(1, QK_A))


def _ret_body(*refs, rows, n_t, t_valid, heads, has_s0, has_alias):
    it = iter(refs)
    q_ref, k_ref, v_ref, g_ref, cos_ref, sin_ref, lg_ref = [next(it) for _ in range(7)]
    s0_ref = next(it) if has_s0 else None
    if has_alias:
        next(it)
    o_ref, sout_ref, st_scr, rel_scr, dq_scr, dk_scr = [next(it) for _ in range(6)]
    t = pl.program_id(1)
    head0 = (pl.program_id(0) % (H_B // heads)) * heads
    n_valid = rows if t_valid is None else t_valid
    row_k = _iota((rows, DK_B), 0)

    @pl.when(t == 0)
    def _():
        steps = jnp.minimum(row_k + 1, n_valid).astype(F32)
        st = jnp.minimum(_iota((rows, rows), 0) + 1, n_valid)
        ss = jnp.minimum(_iota((rows, rows), 1) + 1, n_valid)
        causal = _iota((rows, rows), 0) >= _iota((rows, rows), 1)
        for hh in range(heads):
            st_scr[hh] = s0_ref[hh] if has_s0 else jnp.zeros(st_scr.shape[1:], F32)
            lg = lg_ref[pl.ds(head0 + hh, 1), :]
            dq_scr[hh] = jnp.exp(lg[:, :DK_B] * steps)
            dk_scr[hh] = jnp.exp(lg[:, :DK_B] * (n_valid - steps))
            rel_scr[hh] = jnp.exp(jnp.where(causal, lg[:, :rows] * (st - ss).astype(F32), NEG))

    for hh in range(heads):
        kk = slice(hh * DK_B, (hh + 1) * DK_B)
        vv = slice(hh * DV_B, (hh + 1) * DV_B)
        _ret_head(q_ref.at[:, kk], k_ref.at[:, kk], v_ref.at[:, vv], g_ref.at[:, vv], cos_ref, sin_ref,
                  lg_ref.at[pl.ds(head0 + hh, 1), :], o_ref.at[:, vv], st_scr.at[hh], rel_scr.at[hh], dq_scr.at[hh],
                  dk_scr.at[hh], rows=rows, t_valid=t_valid)

    @pl.when(t == n_t - 1)
    def _():
        for hh in range(heads):
            sout_ref[hh] = st_scr[hh]


def _ret_head(q_ref, k_ref, v_ref, g_ref, cos_ref, sin_ref, lg_ref, o_ref, st_scr, rel_scr, dq_scr, dk_scr, *,
              rows, t_valid):
    lg = lg_ref[...]
    n_valid = rows if t_valid is None else t_valid
    row_k = _iota((rows, DK_B), 0)
    cos = cos_ref[...]
    sin = sin_ref[...]

    def rot(x):
        return x * cos + pltpu.roll(x, DK_B // 2, 1) * sin

    q = rot(q_ref[...])
    k = rot(k_ref[...]) * (DK_B ** -0.5)
    if t_valid is not None:
        k = jnp.where(row_k < t_valid, k, 0.0)
    decay_q = dq_scr[...]
    decay_k = dk_scr[...]

    v16 = v_ref[...].astype(BF16)
    state = st_scr[...]
    scores = _dot_nt(q.astype(BF16), k.astype(BF16)) * rel_scr[...]
    o = _dot(scores.astype(BF16), v16) + _dot((q * decay_q).astype(BF16), state.astype(BF16))
    state_new = state * jnp.exp(lg[:, :DV_B] * float(n_valid)) + _dot_tn((k * decay_k).astype(BF16), v16)
    st_scr[...] = state_new

    ms = jnp.mean(o * o, axis=-1, keepdims=True)
    g = g_ref[...]
    o_ref[...] = (g * _sigmoid(g) * (o * lax.rsqrt(ms + EPS))).astype(o_ref.dtype)


def _ret(h, cos, sin, lg_tab, *, n_seq, seq_rows, rows, row0, t_valid, state0, out_rows, alias, heads=RET_HEADS):
    n_t = seq_rows // rows
    rb0 = row0 // rows
    groups = H_B // heads
    wk, wv = heads * DK_B, heads * DV_B

    def rmap(blk0):
        return lambda s, t: (rb0 + (s // groups) * n_t + t, blk0 + s % groups)

    smap = lambda s, t: (s // groups, s % groups, 0, 0)
    in_specs = [
        pl.BlockSpec((rows, wk), rmap(OFF_QB // wk)),
        pl.BlockSpec((rows, wk), rmap(OFF_KB // wk)),
        pl.BlockSpec((rows, wv), rmap(OFF_VB // wv)),
        pl.BlockSpec((rows, wv), rmap(OFF_GB // wv)),
        pl.BlockSpec((rows, DK_B), lambda s, t: (t, 0)),
        pl.BlockSpec((rows, DK_B), lambda s, t: (t, 0)),
        pl.BlockSpec((H_B, RET_ROWS), lambda s, t: (0, 0)),
    ]
    args = [h, h, h, h, cos, sin, lg_tab]
    if state0 is not None:
        in_specs.append(pl.BlockSpec((None, heads, DK_B, DV_B), smap))
        args.append(state0)
    aliases = {}
    if alias is not None:
        in_specs.append(pl.BlockSpec(memory_space=pl.ANY))
        aliases = {len(args): 0}
        args.append(alias)
    nbytes = heads * (2 * rows * (2 * DK_B + 2 * DV_B) * 4 + 2 * rows * DV_B * 2 + 5 * DK_B * DV_B * 4
                      + 6 * rows * rows * 4 + 8 * rows * DV_B * 4) + 4 * rows * DK_B * 4
    return pl.pallas_call(
        functools.partial(_ret_body, rows=rows, n_t=n_t, t_valid=t_valid, heads=heads, has_s0=state0 is not None,
                          has_alias=alias is not None),
        out_shape=(jax.ShapeDtypeStruct((out_rows, V_B), BF16), jax.ShapeDtypeStruct((n_seq, H_B, DK_B, DV_B), F32)),
        grid=(n_seq * groups, n_t),
        in_specs=in_specs,
        out_specs=(pl.BlockSpec((rows, wv), rmap(0)), pl.BlockSpec((None, heads, DK_B, DV_B), smap)),
        scratch_shapes=[pltpu.VMEM((heads, DK_B, DV_B), F32), pltpu.VMEM((heads, rows, rows), F32),
                        pltpu.VMEM((heads, rows, DK_B), F32), pltpu.VMEM((heads, rows, DK_B), F32)],
        input_output_aliases=aliases,
        compiler_params=_params(("parallel", "arbitrary"), nbytes),
        name="retention",
    )(*args)


def _dil_prefill_body(slopes_ref, *refs, seq):
    qkv = refs[:9]
    o_ref, og_scr, lse_scr = refs[9:]
    slot = pl.program_id(1)
    scale = HD_C ** -0.5
    rr = _iota((BAND, BAND), 0)
    cc = _iota((BAND, BAND), 1)
    dist_cur = (rr - cc).astype(F32)
    ok_cur = rr >= cc
    ok_prev = cc >= rr
    n_units = seq // BAND
    for g, d in enumerate(DILATIONS):
        q_ref, k_ref, v_ref = qkv[3 * g:3 * g + 3]
        pen = slopes_ref[g * G_C + slot] * float(d)
        shift = d.bit_length() - 1

        def rows_of(start, d=d):
            return pl.ds(start, BAND) if d == 1 else pl.ds(start, BAND, stride=d)

        has_prev = seq // (BAND * d) > 1

        def unit(u, carry, q_ref=q_ref, k_ref=k_ref, v_ref=v_ref, pen=pen, d=d, shift=shift, rows_of=rows_of, g=g,
                 has_prev=has_prev):
            blk = u >> shift
            start = (u & (d - 1)) + blk * (BAND * d)
            q = q_ref[rows_of(start), :].astype(BF16)
            keys = k_ref[rows_of(start), :].astype(BF16)
            vals = v_ref[rows_of(start), :].astype(BF16)
            if has_prev:
                prev = jnp.maximum(start - BAND * d, 0)
                keys = jnp.concatenate([k_ref[rows_of(prev), :].astype(BF16), keys], axis=0)
                vals = jnp.concatenate([v_ref[rows_of(prev), :].astype(BF16), vals], axis=0)
                bias = jnp.concatenate([jnp.where(ok_prev, -pen * (dist_cur + float(BAND)), NEG)
                                        + jnp.where(blk > 0, 0.0, NEG),
                                        jnp.where(ok_cur, -pen * dist_cur, NEG)], axis=1)
            else:
                bias = jnp.where(ok_cur, -pen * dist_cur, NEG)
            s = _dot_nt(q, keys) * scale + bias
            m = jnp.max(s, axis=-1, keepdims=True)
            p = jnp.exp(s - m)
            l = jnp.sum(p, axis=-1, keepdims=True)
            acc = _dot(p.astype(BF16), vals)
            og_scr[g, rows_of(start), :] = acc / l
            lse_scr[g, rows_of(start), :] = jnp.broadcast_to(m + jnp.log(l), (BAND, HD_C))
            return carry

        lax.fori_loop(0, n_units, unit, 0, unroll=True)

    lse = [lse_scr[g] for g in range(N_GROUPS_C)]
    top = jnp.maximum(jnp.maximum(lse[0], lse[1]), lse[2])
    w = [jnp.exp(x - top) for x in lse]
    total = w[0] + w[1] + w[2]
    o_ref[...] = ((w[0] * og_scr[0] + w[1] * og_scr[1] + w[2] * og_scr[2]) / total).astype(o_ref.dtype)


def _dil_prefill(h, slopes, *, n_seq, seq, out_rows):
    def cmap(g, j):
        return lambda b, s: (b, (OFF_C + g * 3 * V_C + j * V_C) // HD_C + s)

    in_specs = [pl.BlockSpec(memory_space=pltpu.SMEM)]
    in_specs += [pl.BlockSpec((seq, HD_C), cmap(g, j)) for g in range(N_GROUPS_C) for j in range(3)]
    nbytes = 2 * 9 * seq * HD_C * 4 + 2 * seq * HD_C * 2 + 2 * N_GROUPS_C * seq * HD_C * 4 + 8 * seq * HD_C * 4
    return pl.pallas_call(
        functools.partial(_dil_prefill_body, seq=seq),
        out_shape=jax.ShapeDtypeStruct((out_rows, V_C), BF16),
        grid=(n_seq, G_C),
        in_specs=in_specs,
        out_specs=pl.BlockSpec((seq, HD_C), lambda b, s: (b, s)),
        scratch_shapes=[pltpu.VMEM((N_GROUPS_C, seq, HD_C), F32), pltpu.VMEM((N_GROUPS_C, seq, HD_C), F32)],
        compiler_params=_params(("parallel", "parallel"), nbytes),
        name="dilated_prefill",
    )(slopes, *([h] * 9))


def _dil_step_body(slopes_ref, *refs):
    new = refs[:9]
    caches = refs[9:12]
    o_ref = refs[13]
    scale = HD_C ** -0.5
    col = _iota((SEQ_PAD, BAND * G_C), 1)
    col_slot = col & (G_C - 1)
    steps = (BAND - (col >> (G_C.bit_length() - 1))).astype(F32)
    keys = [c[0].reshape(BAND * G_C, HD_C).astype(BF16) for c in caches]
    vals = [c[1].reshape(BAND * G_C, HD_C).astype(BF16) for c in caches]
    for slot in range(G_C):
        lanes = slice(slot * HD_C, (slot + 1) * HD_C)
        scores, self_scores = [], []
        for g, d in enumerate(DILATIONS):
            q_ref, k_ref, _ = new[3 * g:3 * g + 3]
            q = q_ref[:, lanes]
            s = _dot_nt(q.astype(BF16), keys[g]) * scale
            pen = slopes_ref[g * G_C + slot] * float(d)
            scores.append(jnp.where(col_slot == slot, s - pen * steps, NEG))
            self_scores.append(jnp.sum(q * k_ref[:, lanes], axis=-1, keepdims=True) * scale)
        m = self_scores[0]
        for x in self_scores[1:]:
            m = jnp.maximum(m, x)
        for x in scores:
            m = jnp.maximum(m, jnp.max(x, axis=-1, keepdims=True))
        acc = jnp.zeros((SEQ_PAD, HD_C), F32)
        total = jnp.zeros((SEQ_PAD, 1), F32)
        for g in range(N_GROUPS_C):
            p = jnp.exp(scores[g] - m)
            p_self = jnp.exp(self_scores[g] - m)
            total += jnp.sum(p, axis=-1, keepdims=True) + p_self
            acc += _dot(p.astype(BF16), vals[g]) + p_self * new[3 * g + 2][:, lanes]
        o_ref[:, lanes] = (acc / total).astype(o_ref.dtype)


def _dil_step(h, slopes, caches, layer, oc, *, n_seq, row0):
    rb0 = row0 // SEQ_PAD

    def cmap(g, j):
        return lambda b: (rb0 + b, (OFF_C + g * 3 * V_C + j * V_C) // V_C)

    in_specs = [pl.BlockSpec(memory_space=pltpu.SMEM)]
    in_specs += [pl.BlockSpec((SEQ_PAD, V_C), cmap(g, j)) for g in range(N_GROUPS_C) for j in range(3)]
    strided = []
    for c, d in zip(caches, DILATIONS):
        assert c.shape[3] == BAND * d
        strided.append(c.reshape(c.shape[0], c.shape[1], 2, BAND, d, G_C, HD_C))
        in_specs.append(pl.BlockSpec((None, None, 2, BAND, None, G_C, HD_C), lambda b: (layer, b, 0, 0, 0, 0, 0)))
    in_specs.append(pl.BlockSpec(memory_space=pl.ANY))
    nbytes = 2 * N_GROUPS_C * 2 * BAND * V_C * 4 + 64 * SEQ_PAD * BAND * G_C * 4
    return pl.pallas_call(
        _dil_step_body,
        out_shape=jax.ShapeDtypeStruct(oc.shape, oc.dtype),
        grid=(n_seq,),
        in_specs=in_specs,
        out_specs=pl.BlockSpec((SEQ_PAD, V_C), lambda b: (rb0 + b, 0)),
        input_output_aliases={13: 0},
        compiler_params=_params(("parallel",), nbytes),
        name="dilated_step",
    )(slopes, *([h] * 9), *strided, oc)


def _rope_tables(positions):
    half = DK_B // 2
    inv = ROPE_BASE ** (-np.arange(half, dtype=np.float64) / half)
    ang = np.asarray(positions, np.float64)[:, None] * inv[None, :]
    cos = np.concatenate([np.cos(ang), np.cos(ang)], axis=-1)
    sin = np.concatenate([-np.sin(ang), np.sin(ang)], axis=-1)
    return jnp.asarray(cos, F32), jnp.asarray(sin, F32)


def _layer(x, l, w, lw, state_gla, state_ret, caches, tabs, dims):
    n_p, seq, n_s, m_p, m = dims
    tm = _pick_tile(m, 704)
    tm_big = _pick_tile(m, 1100)
    xn = _rmsnorm(x, lw["g_mix"], BF16)
    h = _in_proj(xn, w["w_in_t"], l, tm=tm_big, tn=768)
    la, bsum = _gate(xn, w["w_in_t"], l, lw["w_alpha2"], lw["b_alpha"], tm=_pick_tile(m, 704, GLA_ROWS), block=GLA_ROWS)

    gla = functools.partial(_gla, h, la, g_gla=lw["g_gla"], out_rows=m)
    oa, sa_p = gla(bsum, n_seq=n_p, seq_rows=seq, rows=GLA_ROWS, row0=0, t_valid=None, state0=None, alias=None)
    oa, sa_s = gla(None, n_seq=n_s, seq_rows=SEQ_PAD, rows=SEQ_PAD, row0=m_p, t_valid=1, state0=state_gla, alias=oa)

    ret = functools.partial(_ret, h, out_rows=m)
    ob, sb_p = ret(tabs["cos_p"], tabs["sin_p"], tabs["lg"], n_seq=n_p, seq_rows=seq, rows=RET_ROWS, row0=0,
                   t_valid=None, state0=None, alias=None)
    ob, sb_s = ret(tabs["cos_s"], tabs["sin_s"], tabs["lg"], n_seq=n_s, seq_rows=SEQ_PAD, rows=SEQ_PAD, row0=m_p,
                   t_valid=1, state0=state_ret, alias=ob)

    oc = _dil_prefill(h, tabs["slopes"], n_seq=n_p, seq=seq, out_rows=m)
    oc = _dil_step(h, tabs["slopes"], caches, l, oc, n_seq=n_s, row0=m_p)

    gates = _matmul_ws(xn, w["w_merge"], l, tm=tm_big, tn=768, out_dtype=BF16, act=_sigmoid, name="merge_gates")
    merged = _merge_up(oa, ob, oc, gates, w["w_up_a"], w["w_up_b"], w["w_up_c"], l, tm=tm_big, tn=512)
    x = _matmul_ws(merged, w["w_out"], l, tm=tm_big, tn=512, residual=x, name="out_proj")
    xf = _rmsnorm(x, lw["g_ffn"], BF16)
    hid = _ffn_up(xf, w["w_ffn_gate"], w["w_ffn_up"], l, tm=_pick_tile(m, 1728), tn=256)
    d_ff = hid.shape[1]
    x = _matmul_ksplit_res(hid, w["w_ffn_down"], l, x, tm=tm, tn=512, tk=d_ff // 2, name="ffn_down")

    win_p, new_s = [], []
    for g in range(N_GROUPS_C):
        off_k = OFF_C + g * 3 * V_C + V_C
        keep = min(BAND * DILATIONS[g], seq)
        kv_p = jnp.stack([lax.slice(h, ((b + 1) * seq - keep, off_k), ((b + 1) * seq, off_k + 2 * V_C))
                          for b in range(n_p)])
        win_p.append(kv_p.reshape(n_p, keep, 2, G_C, HD_C).transpose(0, 2, 1, 3, 4))
        kv_s = lax.slice(h, (m_p, off_k), (m, off_k + 2 * V_C)).reshape(n_s, SEQ_PAD, 2, G_C, HD_C)[:, :1]
        new_s.append(kv_s.transpose(0, 2, 1, 3, 4))
    return x, (sa_p, sa_s, sb_p, sb_s), win_p, new_s


def kernel(x_prompt, x_sample, state_gla, state_ret, cache_win0, cache_win1, cache_win2, g_mix, w_in, w_alpha2,
           b_alpha, g_gla, w_merge, w_up_a, w_up_b, w_up_c, w_out, g_ffn, w_ffn_gate, w_ffn_up, w_ffn_down, g_final):
    n_p, seq, d = x_prompt.shape
    n_s = x_sample.shape[0]
    depth = g_mix.shape[0]
    assert d == D_MODEL and x_sample.shape[1] == 1 and seq % (BAND * DILATIONS[-1]) == 0 and seq % RET_ROWS == 0
    m_p = n_p * seq
    m = m_p + n_s * SEQ_PAD
    dims = (n_p, seq, n_s, m_p, m)

    xs = jnp.pad(x_sample, ((0, 0), (0, SEQ_PAD - 1), (0, 0)))
    x = jnp.concatenate([x_prompt.reshape(m_p, d), xs.reshape(n_s * SEQ_PAD, d)], axis=0)

    cos_p, sin_p = _rope_tables(np.arange(seq))
    cos_s, sin_s = _rope_tables(PAST_LEN + np.arange(SEQ_PAD))
    log_gamma = np.log1p(-(2.0 ** (-5.0 - np.arange(H_B, dtype=np.float64))))
    heads = np.arange(1, N_GROUPS_C * G_C + 1, dtype=np.float64)
    tabs = {
        "cos_p": cos_p, "sin_p": sin_p, "cos_s": cos_s, "sin_s": sin_s,
        "lg": jnp.asarray(np.broadcast_to(log_gamma[:, None], (H_B, RET_ROWS)), F32),
        "slopes": jnp.asarray(2.0 ** (-8.0 * heads / (N_GROUPS_C * G_C)), F32),
    }

    assert w_in.shape[2] == N_MAIN + GATE_RANK
    w = {"w_in_t": jnp.swapaxes(w_in, 1, 2), "w_merge": w_merge, "w_up_a": w_up_a, "w_up_b": w_up_b, "w_up_c": w_up_c,
         "w_out": w_out, "w_ffn_gate": w_ffn_gate, "w_ffn_up": w_ffn_up, "w_ffn_down": w_ffn_down.astype(BF16)}
    caches = (cache_win0, cache_win1, cache_win2)
    outs = {k: [] for k in ("gla_p", "gla_s", "ret_p", "ret_s")}
    win_p = [[] for _ in range(N_GROUPS_C)]
    win_s = [[] for _ in range(N_GROUPS_C)]
    for l in range(depth):
        lw = {
            "g_mix": g_mix[l], "g_gla": g_gla[l], "g_ffn": g_ffn[l], "b_alpha": b_alpha[l],
            "w_alpha2": w_alpha2[l],
        }
        x, (sa_p, sa_s, sb_p, sb_s), wp, ws = _layer(x, l, w, lw, state_gla[l], state_ret[l], caches, tabs, dims)
        outs["gla_p"].append(sa_p)
        outs["gla_s"].append(sa_s)
        outs["ret_p"].append(sb_p)
        outs["ret_s"].append(sb_s)
        for g in range(N_GROUPS_C):
            win_p[g].append(wp[g])
            win_s[g].append(ws[g])

    y_prompt = _rmsnorm(x, g_final, F32, rows=m_p).reshape(n_p, seq, d)
    y_sample = _rmsnorm(x[m_p:], g_final, F32).reshape(n_s, SEQ_PAD, d)[:, :1]
    win_s = [jnp.concatenate([caches[g][:, :, :, 1:], jnp.stack(win_s[g])], axis=3) for g in range(N_GROUPS_C)]
    return (y_prompt, y_sample, jnp.stack(outs["gla_p"]), jnp.stack(outs["gla_s"]), jnp.stack(outs["ret_p"]),
            jnp.stack(outs["ret_s"]), jnp.stack(win_p[0]), win_s[0], jnp.stack(win_p[1]), win_s[1],
            jnp.stack(win_p[2]), win_s[2])
```

```python
import functools

import numpy as np
import jax
import jax.numpy as jnp
from jax import lax
from jax.experimental import pallas as pl
from jax.experimental.pallas import tpu as pltpu

F32 = jnp.float32
BF16 = jnp.bfloat16
HIGHEST = lax.Precision.HIGHEST

D_MODEL = 4096
H_A, DK_A, DV_A = 4, 256, 512
GATE_RANK, GATE_TAU = 16, 16.0
H_B, DK_B, DV_B = 8, 128, 256
ROPE_BASE = 10000.0
DILATIONS = (1, 4, 16)
N_GROUPS_C, G_C, HD_C = 3, 4, 128
BAND = 128
PAST_LEN = 16384
QK_A, V_A = H_A * DK_A, H_A * DV_A
QK_B, V_B = H_B * DK_B, H_B * DV_B
V_C = G_C * HD_C
C_COLS = N_GROUPS_C * 3 * V_C
EPS = 1e-6
NEG = -1e30

OFF_QA, OFF_KA, OFF_VA, OFF_RA = 0, QK_A, 2 * QK_A, 2 * QK_A + V_A
OFF_B = 2 * QK_A + 2 * V_A
OFF_QB, OFF_KB, OFF_VB, OFF_GB = OFF_B, OFF_B + QK_B, OFF_B + 2 * QK_B, OFF_B + 2 * QK_B + V_B
OFF_C = OFF_B + 2 * QK_B + 2 * V_B
N_MAIN = OFF_C + C_COLS

LANES = 128
SEQ_PAD = 16
GLA_ROWS = 128
RET_ROWS = 256
RET_HEADS = 8
SUBLANES = 8
VMEM_CAP = 56 * 1024 * 1024


def _vmem(nbytes):
    return int(min(VMEM_CAP, nbytes + (6 << 20)))


def _params(sem, nbytes):
    return pltpu.CompilerParams(dimension_semantics=sem, vmem_limit_bytes=_vmem(nbytes))


def _pick_tile(m, cap, align=16):
    best = align
    for t in range(align, cap + 1, align):
        if m % t == 0:
            best = t
    assert m % best == 0, (m, cap, align)
    return best


def _iota(shape, dim):
    return lax.broadcasted_iota(jnp.int32, shape, dim)


def _dot(a, b):
    return jnp.dot(a, b, preferred_element_type=F32)


def _dot_nt(a, b):
    return lax.dot_general(a, b, (((1,), (1,)), ((), ())), preferred_element_type=F32)


def _dot_tn(a, b):
    return lax.dot_general(a, b, (((0,), (0,)), ((), ())), preferred_element_type=F32)


def _sigmoid(x):
    return 1.0 / (1.0 + jnp.exp(-x))


def _rmsnorm_body(x_ref, g_ref, o_ref):
    x = x_ref[...]
    ms = jnp.mean(x * x, axis=-1, keepdims=True)
    o_ref[...] = (x * lax.rsqrt(ms + EPS) * g_ref[...]).astype(o_ref.dtype)


def _rmsnorm(x, g, out_dtype, rows=None):
    m, d = x.shape
    m = m if rows is None else rows
    tm = _pick_tile(m, 512)
    nbytes = 2 * tm * d * (4 + jnp.dtype(out_dtype).itemsize) + 2 * tm * d * 4
    return pl.pallas_call(
        _rmsnorm_body,
        out_shape=jax.ShapeDtypeStruct((m, d), out_dtype),
        grid=(m // tm,),
        in_specs=[pl.BlockSpec((tm, d), lambda i: (i, 0)), pl.BlockSpec((1, d), lambda i: (0, 0))],
        out_specs=pl.BlockSpec((tm, d), lambda i: (i, 0)),
        compiler_params=_params(("parallel",), nbytes),
        name="rmsnorm",
    )(x, g.reshape(1, d))


def _row_scale(ss_ref):
    return lax.rsqrt(ss_ref[:, :1] * (1.0 / D_MODEL) + EPS)


def _emit_norm(y, g_ref, xg_ref, ss_ref, rows, first):
    xg_ref[...] = (y * g_ref[...]).astype(BF16)
    part = jnp.broadcast_to(jnp.sum(y * y, axis=-1, keepdims=True), (y.shape[0], LANES))

    @pl.when(first)
    def _():
        ss_ref[rows, :] = part

    @pl.when(jnp.logical_not(first))
    def _():
        ss_ref[rows, :] += part


def _mm_ws_body(*refs, act, has_res, transposed, has_scale, emit_norm, tm):
    it = iter(refs)
    x_ref, w_ref = next(it), next(it)
    s_ref = next(it) if has_scale else None
    r_ref = next(it) if has_res else None
    g_ref = next(it) if emit_norm else None
    o_ref = next(it)
    if transposed:
        y = _dot_nt(x_ref[...], w_ref[0].astype(BF16))
    else:
        y = _dot(x_ref[...], w_ref[...].astype(BF16))
    if has_scale:
        y = y * _row_scale(s_ref)
    if act is not None:
        y = act(y)
    if has_res:
        y = r_ref[...] + y
    o_ref[...] = y.astype(o_ref.dtype)
    if emit_norm:
        xg_ref, ss_ref = next(it), next(it)
        rows = pl.ds(pl.multiple_of(pl.program_id(1) * tm, tm), tm)
        _emit_norm(y, g_ref, xg_ref, ss_ref, rows, pl.program_id(0) == 0)


def _matmul_ws(x, w, layer, *, tm, tn, n=None, row_of=None, out_dtype=F32, act=None, residual=None, scale=None,
               norm_gain=None, name="matmul"):
    m, k = x.shape
    transposed = row_of is not None
    n = w.shape[2] if n is None else n
    assert m % tm == 0 and n % tn == 0
    if transposed:
        w_spec = pl.BlockSpec((pl.Element(1), pl.Element(tn), pl.Element(k)),
                              lambda j, i: (layer, pl.multiple_of(row_of(j), SUBLANES), 0))
    else:
        w_spec = pl.BlockSpec((None, k, tn), lambda j, i: (layer, 0, j))
    in_specs = [pl.BlockSpec((tm, k), lambda j, i: (i, 0)), w_spec]
    args = [x, w]
    osz = jnp.dtype(out_dtype).itemsize
    nbytes = 2 * (tm * k * 2 + k * tn * 4 + tm * tn * osz) + k * tn * 2 + 2 * tm * tn * 4
    if scale is not None:
        in_specs.append(pl.BlockSpec((tm, LANES), lambda j, i: (i, 0)))
        args.append(scale)
        nbytes += 2 * tm * LANES * 4
    if residual is not None:
        in_specs.append(pl.BlockSpec((tm, tn), lambda j, i: (i, j)))
        args.append(residual)
        nbytes += 2 * tm * tn * 4
    out_shape = jax.ShapeDtypeStruct((m, n), out_dtype)
    out_specs = pl.BlockSpec((tm, tn), lambda j, i: (i, j))
    sem = ("parallel", "parallel")
    if norm_gain is not None:
        in_specs.append(pl.BlockSpec((1, tn), lambda j, i: (0, j)))
        args.append(norm_gain.reshape(1, n))
        out_shape = (out_shape, jax.ShapeDtypeStruct((m, n), BF16), jax.ShapeDtypeStruct((m, LANES), F32))
        out_specs = (out_specs, pl.BlockSpec((tm, tn), lambda j, i: (i, j)), pl.BlockSpec((m, LANES), lambda j, i: (0, 0)))
        sem = ("arbitrary", "arbitrary")
        nbytes += 2 * tm * tn * 2 + 2 * m * LANES * 4 + 2 * tm * tn * 4
    return pl.pallas_call(
        functools.partial(_mm_ws_body, act=act, has_res=residual is not None, transposed=transposed,
                          has_scale=scale is not None, emit_norm=norm_gain is not None, tm=tm),
        out_shape=out_shape,
        grid=(n // tn, m // tm),
        in_specs=in_specs,
        out_specs=out_specs,
        compiler_params=_params(sem, nbytes),
        name=name,
    )(*args)


def _in_proj(x, w_in_t, layer, *, tm, tn):
    assert OFF_B % tn == 0
    return _matmul_ws(x, w_in_t, layer, tm=tm, tn=tn, n=N_MAIN, name="in_proj",
                      row_of=lambda j: j * tn + jnp.where(j >= OFF_B // tn, GATE_RANK, 0))


def _mm_ksplit_res_body(x_ref, w_ref, r_ref, o_ref, acc_ref, *, tm, n_k):
    kk = pl.program_id(1)
    rows = pl.ds(pl.multiple_of(pl.program_id(2) * tm, tm), tm)
    part = _dot(x_ref[...], w_ref[...])

    @pl.when(kk == 0)
    def _():
        acc_ref[rows, :] = r_ref[...] + part

    @pl.when(jnp.logical_and(kk > 0, kk < n_k - 1))
    def _():
        acc_ref[rows, :] += part

    @pl.when(kk == n_k - 1)
    def _():
        o_ref[...] = acc_ref[rows, :] + part


def _matmul_ksplit_res(x, w, layer, residual, *, tm, tn, tk, name):
    m, k = x.shape
    n = w.shape[2]
    assert m % tm == 0 and n % tn == 0 and k % tk == 0 and k // tk >= 2
    n_k, n_i = k // tk, m // tm
    nbytes = 2 * (tm * tk * 2 + tk * tn * 2 + 2 * tm * tn * 4) + m * tn * 4 + 2 * tm * tn * 4
    return pl.pallas_call(
        functools.partial(_mm_ksplit_res_body, tm=tm, n_k=n_k),
        out_shape=jax.ShapeDtypeStruct((m, n), F32),
        grid=(n // tn, n_k, n_i),
        in_specs=[
            pl.BlockSpec((tm, tk), lambda j, kk, i: (i, kk)),
            pl.BlockSpec((None, tk, tn), lambda j, kk, i: (layer, kk, j)),
            pl.BlockSpec((tm, tn), lambda j, kk, i: (jnp.where(kk == 0, i, n_i - 1), j)),
        ],
        out_specs=pl.BlockSpec((tm, tn), lambda j, kk, i: (jnp.where(kk == n_k - 1, i, 0), j)),
        scratch_shapes=[pltpu.VMEM((m, tn), F32)],
        compiler_params=_params(("parallel", "arbitrary", "arbitrary"), nbytes),
        name=name,
    )(x, w, residual)


def _merge_up_body(oa_ref, ob_ref, oc_ref, ga_ref, gb_ref, gc_ref, wa_ref, wb_ref, wc_ref, o_ref):
    acc = ga_ref[...].astype(F32) * _dot(oa_ref[...], wa_ref[...].astype(BF16))
    acc += gb_ref[...].astype(F32) * _dot(ob_ref[...], wb_ref[...].astype(BF16))
    acc += gc_ref[...].astype(F32) * _dot(oc_ref[...], wc_ref[...].astype(BF16))
    o_ref[...] = acc.astype(o_ref.dtype)


def _merge_up(oa, ob, oc, gates, w_up_a, w_up_b, w_up_c, layer, *, tm, tn):
    m = oa.shape[0]
    d = w_up_a.shape[2]
    nt = d // tn
    assert m % tm == 0 and d % tn == 0

    def row(width):
        return pl.BlockSpec((tm, width), lambda j, i: (i, 0))

    def gate(branch):
        return pl.BlockSpec((tm, tn), lambda j, i: (i, branch * nt + j))

    def panel(kdim):
        return pl.BlockSpec((None, kdim, tn), lambda j, i: (layer, 0, j))

    kk = V_A + V_B + V_C
    nbytes = 2 * (tm * kk * 2 + 3 * tm * tn * 2 + kk * tn * 4 + tm * tn * 2) + kk * tn * 2 + 6 * tm * tn * 4
    return pl.pallas_call(
        _merge_up_body,
        out_shape=jax.ShapeDtypeStruct((m, d), BF16),
        grid=(nt, m // tm),
        in_specs=[row(V_A), row(V_B), row(V_C), gate(0), gate(1), gate(2), panel(V_A), panel(V_B), panel(V_C)],
        out_specs=pl.BlockSpec((tm, tn), lambda j, i: (i, j)),
        compiler_params=_params(("parallel", "parallel"), nbytes),
        name="merge_up",
    )(oa, ob, oc, gates, gates, gates, w_up_a, w_up_b, w_up_c)


def _ffn_up_body(x_ref, s_ref, wg_ref, wu_ref, o_ref):
    x = x_ref[...]
    r = _row_scale(s_ref)
    a = _dot(x, wg_ref[...].astype(BF16)) * r
    o_ref[...] = (a * _sigmoid(a) * (_dot(x, wu_ref[...].astype(BF16)) * r)).astype(o_ref.dtype)


def _ffn_up(x, scale, wg, wu, layer, *, tm, tn):
    m, k = x.shape
    n = wg.shape[2]
    assert m % tm == 0 and n % tn == 0
    nbytes = 2 * (tm * k * 2 + 2 * k * tn * 4 + tm * tn * 2 + tm * LANES * 4) + 2 * k * tn * 2 + 3 * tm * tn * 4
    return pl.pallas_call(
        _ffn_up_body,
        out_shape=jax.ShapeDtypeStruct((m, n), BF16),
        grid=(m // tm, n // tn),
        in_specs=[pl.BlockSpec((tm, k), lambda i, j: (i, 0)), pl.BlockSpec((tm, LANES), lambda i, j: (i, 0)),
                  pl.BlockSpec((None, k, tn), lambda i, j: (layer, 0, j)),
                  pl.BlockSpec((None, k, tn), lambda i, j: (layer, 0, j))],
        out_specs=pl.BlockSpec((tm, tn), lambda i, j: (i, j)),
        compiler_params=_params(("parallel", "parallel"), nbytes),
        name="ffn_up",
    )(x, scale, wg, wu)


def _gla_body(*refs, rows, n_t, t_valid, heads, has_bsum, has_s0, has_alias):
    it = iter(refs)
    q_ref, k_ref, v_ref, r_ref, la_ref, g_ref = [next(it) for _ in range(6)]
    bsum_ref = next(it) if has_bsum else None
    s0_ref = next(it) if has_s0 else None
    if has_alias:
        next(it)
    o_ref, sout_ref, st_scr, b_scr = [next(it) for _ in range(4)]
    t = pl.program_id(1)

    @pl.when(t == 0)
    def _():
        for hh in range(heads):
            st_scr[hh] = s0_ref[hh].T if has_s0 else jnp.zeros(st_scr.shape[1:], F32)

    for hh in range(heads):
        kk = slice(hh * DK_A, (hh + 1) * DK_A)
        vv = slice(hh * DV_A, (hh + 1) * DV_A)
        _gla_head(q_ref.at[:, kk], k_ref.at[:, kk], v_ref.at[:, vv], r_ref.at[:, vv], la_ref.at[:, kk], g_ref,
                  None if bsum_ref is None else bsum_ref.at[:, kk], o_ref.at[:, vv], st_scr.at[hh], b_scr.at[hh],
                  rows=rows, t_valid=t_valid)

    @pl.when(t == n_t - 1)
    def _():
        for hh in range(heads):
            sout_ref[hh] = st_scr[hh].T


def _gla_head(q_ref, k_ref, v_ref, r_ref, la_ref, g_ref, bsum_ref, o_ref, st_scr, b_scr, *, rows, t_valid):
    la = la_ref[...]
    row_k = _iota((rows, DK_A), 0)
    if t_valid is not None:
        la = jnp.where(row_k < t_valid, la, 0.0)
    if bsum_ref is not None:
        b = bsum_ref[...]
    else:
        tri = (_iota((rows, rows), 0) >= _iota((rows, rows), 1)).astype(F32)
        b = jnp.dot(tri, la, precision=HIGHEST, preferred_element_type=F32)
    b_scr[...] = b

    q = q_ref[...] * (DK_A ** -0.5)
    k = k_ref[...]
    v16 = v_ref[...].astype(BF16)
    st = st_scr[...]
    o = _dot_nt((q * jnp.exp(b)).astype(BF16), st.astype(BF16))

    t_idx = _iota((rows, rows), 0)
    s_idx = _iota((rows, rows), 1)
    scores = jnp.where(t_idx == s_idx, _dot_nt(q.astype(BF16), k.astype(BF16)), 0.0)
    m = rows // 2
    while m >= 1:
        sub = row_k & (2 * m - 1)
        if m >= 4:
            pieces = [jnp.broadcast_to(b_scr[pl.ds(p * 2 * m + m - 1, 1), :], (2 * m, DK_A)) for p in range(rows // (2 * m))]
            log_fac = -jnp.abs(b - (pieces[0] if len(pieces) == 1 else jnp.concatenate(pieces, axis=0)))
        elif m == 2:
            la_next = pltpu.roll(la, rows - 1, 0)
            la_prev = pltpu.roll(la, 1, 0)
            log_fac = jnp.where(sub == 0, la_next, jnp.where(sub == 1, 0.0, jnp.where(sub == 2, la, la + la_prev)))
        else:
            log_fac = jnp.where(sub == 1, la, 0.0)
        fac = jnp.exp(log_fac)
        upper = sub >= m
        qs = jnp.where(upper, q * fac, 0.0)
        ks = jnp.where(upper, 0.0, k * fac)
        part = _dot_nt(qs.astype(BF16), ks.astype(BF16))
        if 2 * m < rows:
            shift = (2 * m).bit_length() - 1
            part = jnp.where((t_idx >> shift) == (s_idx >> shift), part, 0.0)
        scores = scores + part
        m //= 2
    o = o + _dot(scores.astype(BF16), v16)

    b_last = b_scr[pl.ds(rows - 1, 1), :]
    k_st = k if t_valid is None else jnp.where(row_k < t_valid, k, 0.0)
    st_new = st * jnp.exp(b_last) + _dot_tn(v16, (k_st * jnp.exp(b_last - b)).astype(BF16))
    st_scr[...] = st_new

    ms = jnp.mean(o * o, axis=-1, keepdims=True)
    r = r_ref[...]
    o_ref[...] = (r * _sigmoid(r) * (o * lax.rsqrt(ms + EPS) * g_ref[...])).astype(o_ref.dtype)


def _gla(h, la, bsum, g_gla, *, n_seq, seq_rows, rows, row0, t_valid, state0, out_rows, alias, heads=H_A):
    n_t = seq_rows // rows
    rb0 = row0 // rows
    groups = H_A // heads
    wk, wv = heads * DK_A, heads * DV_A

    def rmap(blk0):
        return lambda s, t: (rb0 + (s // groups) * n_t + t, blk0 + s % groups)

    smap = lambda s, t: (s // groups, s % groups, 0, 0)
    in_specs = [
        pl.BlockSpec((rows, wk), rmap(OFF_QA // wk)),
        pl.BlockSpec((rows, wk), rmap(OFF_KA // wk)),
        pl.BlockSpec((rows, wv), rmap(OFF_VA // wv)),
        pl.BlockSpec((rows, wv), rmap(OFF_RA // wv)),
        pl.BlockSpec((rows, wk), rmap(0)),
        pl.BlockSpec((1, DV_A), lambda s, t: (0, 0)),
    ]
    args = [h, h, h, h, la, g_gla.reshape(1, DV_A)]
    if bsum is not None:
        in_specs.append(pl.BlockSpec((rows, wk), rmap(0)))
        args.append(bsum)
    if state0 is not None:
        in_specs.append(pl.BlockSpec((None, heads, DK_A, DV_A), smap))
        args.append(state0)
    aliases = {}
    if alias is not None:
        in_specs.append(pl.BlockSpec(memory_space=pl.ANY))
        aliases = {len(args): 0}
        args.append(alias)
    nbytes = heads * (2 * rows * (4 * DK_A + 2 * DV_A) * 4 + 2 * rows * DV_A * 2 + 5 * DK_A * DV_A * 4
                      + rows * DK_A * 4 + 16 * rows * DV_A * 4)
    return pl.pallas_call(
        functools.partial(_gla_body, rows=rows, n_t=n_t, t_valid=t_valid, heads=heads, has_bsum=bsum is not None,
                          has_s0=state0 is not None, has_alias=alias is not None),
        out_shape=(jax.ShapeDtypeStruct((out_rows, V_A), BF16), jax.ShapeDtypeStruct((n_seq, H_A, DK_A, DV_A), F32)),
        grid=(n_seq * groups, n_t),
        in_specs=in_specs,
        out_specs=(pl.BlockSpec((rows, wv), rmap(0)), pl.BlockSpec((None, heads, DK_A, DV_A), smap)),
        scratch_shapes=[pltpu.VMEM((heads, DV_A, DK_A), F32), pltpu.VMEM((heads, rows, DK_A), F32)],
        input_output_aliases=aliases,
        compiler_params=_params(("parallel", "arbitrary"), nbytes),
        name="gla",
    )(*args)


def _gate_body(x_ref, w_ref, wa_hi_ref, wa_lo_ref, ba_ref, la_ref, bsum_ref, *, block):
    za = _dot_nt(x_ref[...], w_ref[0].astype(BF16))
    za_hi = za.astype(BF16)
    za_lo = (za - za_hi.astype(F32)).astype(BF16)
    wa_hi = wa_hi_ref[...]
    z = _dot(za_hi, wa_hi) + (_dot(za_lo, wa_hi) + _dot(za_hi, wa_lo_ref[...])) + ba_ref[...]
    la = (jnp.minimum(z, 0.0) - jnp.log(1.0 + jnp.exp(-jnp.abs(z)))) * (1.0 / GATE_TAU)
    la_ref[...] = la
    tri = (_iota((block, block), 0) >= _iota((block, block), 1)).astype(BF16)
    la_hi = la.astype(BF16)
    rest = la - la_hi.astype(F32)
    la_mid = rest.astype(BF16)
    la_lo = (rest - la_mid.astype(F32)).astype(BF16)
    for p in range(x_ref.shape[0] // block):
        rows = slice(p * block, (p + 1) * block)
        bsum_ref[rows, :] = _dot(tri, la_hi[rows]) + (_dot(tri, la_mid[rows]) + _dot(tri, la_lo[rows]))


def _gate(x, w_in_t, layer, w_alpha2, b_alpha, *, tm, block):
    m, k = x.shape
    assert m % tm == 0 and tm % block == 0 and OFF_B % SUBLANES == 0
    wa = jnp.pad(w_alpha2, ((0, LANES - GATE_RANK), (0, 0)))
    wa_hi = wa.astype(BF16)
    wa_lo = (wa - wa_hi.astype(F32)).astype(BF16)
    full = lambda shape: pl.BlockSpec(shape, lambda i: (0, 0))
    nbytes = 2 * (tm * k * 2 + k * LANES * 4 + 2 * tm * QK_A * 4) + k * LANES * 2 + 8 * tm * QK_A * 4
    return pl.pallas_call(
        functools.partial(_gate_body, block=block),
        out_shape=(jax.ShapeDtypeStruct((m, QK_A), F32), jax.ShapeDtypeStruct((m, QK_A), F32)),
        grid=(m // tm,),
        in_specs=[pl.BlockSpec((tm, k), lambda i: (i, 0)),
                  pl.BlockSpec((pl.Element(1), pl.Element(LANES), pl.Element(k)), lambda i: (layer, OFF_B, 0)),
                  full((LANES, QK_A)), full((LANES, QK_A)), full((1, QK_A))],
        out_specs=(pl.BlockSpec((tm, QK_A), lambda i: (i, 0)), pl.BlockSpec((tm, QK_A), lambda i: (i, 0))),
        compiler_params=_params(("parallel",), nbytes),
        name="gla_gate",
    )(x, w_in_t, wa_hi, wa_lo, b_alpha.reshape(1, QK_A))


def _ret_body(*refs, rows, n_t, t_valid, heads, has_s0, has_alias):
    it = iter(refs)
    q_ref, k_ref, v_ref, g_ref, cos_ref, sin_ref, lg_ref = [next(it) for _ in range(7)]
    s0_ref = next(it) if has_s0 else None
    if has_alias:
        next(it)
    o_ref, sout_ref, st_scr, rel_scr, dq_scr, dk_scr = [next(it) for _ in range(6)]
    t = pl.program_id(1)
    head0 = (pl.program_id(0) % (H_B // heads)) * heads
    n_valid = rows if t_valid is None else t_valid
    row_k = _iota((rows, DK_B), 0)

    @pl.when(t == 0)
    def _():
        steps = jnp.minimum(row_k + 1, n_valid).astype(F32)
        st = jnp.minimum(_iota((rows, rows), 0) + 1, n_valid)
        ss = jnp.minimum(_iota((rows, rows), 1) + 1, n_valid)
        causal = _iota((rows, rows), 0) >= _iota((rows, rows), 1)
        for hh in range(heads):
            st_scr[hh] = s0_ref[hh] if has_s0 else jnp.zeros(st_scr.shape[1:], F32)
            lg = lg_ref[pl.ds(head0 + hh, 1), :]
            dq_scr[hh] = jnp.exp(lg[:, :DK_B] * steps)
            dk_scr[hh] = jnp.exp(lg[:, :DK_B] * (n_valid - steps))
            rel_scr[hh] = jnp.exp(jnp.where(causal, lg[:, :rows] * (st - ss).astype(F32), NEG))

    for hh in range(heads):
        kk = slice(hh * DK_B, (hh + 1) * DK_B)
        vv = slice(hh * DV_B, (hh + 1) * DV_B)
        _ret_head(q_ref.at[:, kk], k_ref.at[:, kk], v_ref.at[:, vv], g_ref.at[:, vv], cos_ref, sin_ref,
                  lg_ref.at[pl.ds(head0 + hh, 1), :], o_ref.at[:, vv], st_scr.at[hh], rel_scr.at[hh], dq_scr.at[hh],
                  dk_scr.at[hh], rows=rows, t_valid=t_valid)

    @pl.when(t == n_t - 1)
    def _():
        for hh in range(heads):
            sout_ref[hh] = st_scr[hh]


def _ret_head(q_ref, k_ref, v_ref, g_ref, cos_ref, sin_ref, lg_ref, o_ref, st_scr, rel_scr, dq_scr, dk_scr, *,
              rows, t_valid):
    lg = lg_ref[...]
    n_valid = rows if t_valid is None else t_valid
    row_k = _iota((rows, DK_B), 0)
    cos = cos_ref[...]
    sin = sin_ref[...]

    def rot(x):
        return x * cos + pltpu.roll(x, DK_B // 2, 1) * sin

    q = rot(q_ref[...])
    k = rot(k_ref[...]) * (DK_B ** -0.5)
    if t_valid is not None:
        k = jnp.where(row_k < t_valid, k, 0.0)
    decay_q = dq_scr[...]
    decay_k = dk_scr[...]

    v16 = v_ref[...].astype(BF16)
    state = st_scr[...]
    scores = _dot_nt(q.astype(BF16), k.astype(BF16)) * rel_scr[...]
    o = _dot(scores.astype(BF16), v16) + _dot((q * decay_q).astype(BF16), state.astype(BF16))
    state_new = state * jnp.exp(lg[:, :DV_B] * float(n_valid)) + _dot_tn((k * decay_k).astype(BF16), v16)
    st_scr[...] = state_new

    ms = jnp.mean(o * o, axis=-1, keepdims=True)
    g = g_ref[...]
    o_ref[...] = (g * _sigmoid(g) * (o * lax.rsqrt(ms + EPS))).astype(o_ref.dtype)


def _ret(h, cos, sin, lg_tab, *, n_seq, seq_rows, rows, row0, t_valid, state0, out_rows, alias, heads=RET_HEADS):
    n_t = seq_rows // rows
    rb0 = row0 // rows
    groups = H_B // heads
    wk, wv = heads * DK_B, heads * DV_B

    def rmap(blk0):
        return lambda s, t: (rb0 + (s // groups) * n_t + t, blk0 + s % groups)

    smap = lambda s, t: (s // groups, s % groups, 0, 0)
    in_specs = [
        pl.BlockSpec((rows, wk), rmap(OFF_QB // wk)),
        pl.BlockSpec((rows, wk), rmap(OFF_KB // wk)),
        pl.BlockSpec((rows, wv), rmap(OFF_VB // wv)),
        pl.BlockSpec((rows, wv), rmap(OFF_GB // wv)),
        pl.BlockSpec((rows, DK_B), lambda s, t: (t, 0)),
        pl.BlockSpec((rows, DK_B), lambda s, t: (t, 0)),
        pl.BlockSpec((H_B, RET_ROWS), lambda s, t: (0, 0)),
    ]
    args = [h, h, h, h, cos, sin, lg_tab]
    if state0 is not None:
        in_specs.append(pl.BlockSpec((None, heads, DK_B, DV_B), smap))
        args.append(state0)
    aliases = {}
    if alias is not None:
        in_specs.append(pl.BlockSpec(memory_space=pl.ANY))
        aliases = {len(args): 0}
        args.append(alias)
    nbytes = heads * (2 * rows * (2 * DK_B + 2 * DV_B) * 4 + 2 * rows * DV_B * 2 + 5 * DK_B * DV_B * 4
                      + 6 * rows * rows * 4 + 8 * rows * DV_B * 4) + 4 * rows * DK_B * 4
    return pl.pallas_call(
        functools.partial(_ret_body, rows=rows, n_t=n_t, t_valid=t_valid, heads=heads, has_s0=state0 is not None,
                          has_alias=alias is not None),
        out_shape=(jax.ShapeDtypeStruct((out_rows, V_B), BF16), jax.ShapeDtypeStruct((n_seq, H_B, DK_B, DV_B), F32)),
        grid=(n_seq * groups, n_t),
        in_specs=in_specs,
        out_specs=(pl.BlockSpec((rows, wv), rmap(0)), pl.BlockSpec((None, heads, DK_B, DV_B), smap)),
        scratch_shapes=[pltpu.VMEM((heads, DK_B, DV_B), F32), pltpu.VMEM((heads, rows, rows), F32),
                        pltpu.VMEM((heads, rows, DK_B), F32), pltpu.VMEM((heads, rows, DK_B), F32)],
        input_output_aliases=aliases,
        compiler_params=_params(("parallel", "arbitrary"), nbytes),
        name="retention",
    )(*args)


def _dil_prefill_body(slopes_ref, *refs, seq):
    qkv = refs[:9]
    o_ref, og_scr, lse_scr = refs[9:]
    slot = pl.program_id(1)
    scale = HD_C ** -0.5
    rr = _iota((BAND, BAND), 0)
    cc = _iota((BAND, BAND), 1)
    dist_cur = (rr - cc).astype(F32)
    ok_cur = rr >= cc
    ok_prev = cc >= rr
    n_units = seq // BAND
    for g, d in enumerate(DILATIONS):
        q_ref, k_ref, v_ref = qkv[3 * g:3 * g + 3]
        pen = slopes_ref[g * G_C + slot] * float(d)
        shift = d.bit_length() - 1

        def rows_of(start, d=d):
            return pl.ds(start, BAND) if d == 1 else pl.ds(start, BAND, stride=d)

        has_prev = seq // (BAND * d) > 1

        def unit(u, carry, q_ref=q_ref, k_ref=k_ref, v_ref=v_ref, pen=pen, d=d, shift=shift, rows_of=rows_of, g=g,
                 has_prev=has_prev):
            blk = u >> shift
            start = (u & (d - 1)) + blk * (BAND * d)
            q = q_ref[rows_of(start), :].astype(BF16)
            keys = k_ref[rows_of(start), :].astype(BF16)
            vals = v_ref[rows_of(start), :].astype(BF16)
            if has_prev:
                prev = jnp.maximum(start - BAND * d, 0)
                keys = jnp.concatenate([k_ref[rows_of(prev), :].astype(BF16), keys], axis=0)
                vals = jnp.concatenate([v_ref[rows_of(prev), :].astype(BF16), vals], axis=0)
                bias = jnp.concatenate([jnp.where(ok_prev, -pen * (dist_cur + float(BAND)), NEG)
                                        + jnp.where(blk > 0, 0.0, NEG),
                                        jnp.where(ok_cur, -pen * dist_cur, NEG)], axis=1)
            else:
                bias = jnp.where(ok_cur, -pen * dist_cur, NEG)
            s = _dot_nt(q, keys) * scale + bias
            m = jnp.max(s, axis=-1, keepdims=True)
            p = jnp.exp(s - m)
            l = jnp.sum(p, axis=-1, keepdims=True)
            acc = _dot(p.astype(BF16), vals)
            og_scr[g, rows_of(start), :] = acc / l
            lse_scr[g, rows_of(start), :] = jnp.broadcast_to(m + jnp.log(l), (BAND, HD_C))
            return carry

        lax.fori_loop(0, n_units, unit, 0, unroll=True)

    lse = [lse_scr[g] for g in range(N_GROUPS_C)]
    top = jnp.maximum(jnp.maximum(lse[0], lse[1]), lse[2])
    w = [jnp.exp(x - top) for x in lse]
    total = w[0] + w[1] + w[2]
    o_ref[...] = ((w[0] * og_scr[0] + w[1] * og_scr[1] + w[2] * og_scr[2]) / total).astype(o_ref.dtype)


def _dil_prefill(h, slopes, *, n_seq, seq, out_rows):
    def cmap(g, j):
        return lambda b, s: (b, (OFF_C + g * 3 * V_C + j * V_C) // HD_C + s)

    in_specs = [pl.BlockSpec(memory_space=pltpu.SMEM)]
    in_specs += [pl.BlockSpec((seq, HD_C), cmap(g, j)) for g in range(N_GROUPS_C) for j in range(3)]
    nbytes = 2 * 9 * seq * HD_C * 4 + 2 * seq * HD_C * 2 + 2 * N_GROUPS_C * seq * HD_C * 4 + 8 * seq * HD_C * 4
    return pl.pallas_call(
        functools.partial(_dil_prefill_body, seq=seq),
        out_shape=jax.ShapeDtypeStruct((out_rows, V_C), BF16),
        grid=(n_seq, G_C),
        in_specs=in_specs,
        out_specs=pl.BlockSpec((seq, HD_C), lambda b, s: (b, s)),
        scratch_shapes=[pltpu.VMEM((N_GROUPS_C, seq, HD_C), F32), pltpu.VMEM((N_GROUPS_C, seq, HD_C), F32)],
        compiler_params=_params(("parallel", "parallel"), nbytes),
        name="dilated_prefill",
    )(slopes, *([h] * 9))


def _dil_step_body(slopes_ref, *refs):
    new = refs[:9]
    caches = refs[9:12]
    o_ref = refs[13]
    scale = HD_C ** -0.5
    col = _iota((SEQ_PAD, BAND * G_C), 1)
    col_slot = col & (G_C - 1)
    steps = (BAND - (col >> (G_C.bit_length() - 1))).astype(F32)
    keys = [c[0].reshape(BAND * G_C, HD_C).astype(BF16) for c in caches]
    vals = [c[1].reshape(BAND * G_C, HD_C).astype(BF16) for c in caches]
    for slot in range(G_C):
        lanes = slice(slot * HD_C, (slot + 1) * HD_C)
        scores, self_scores = [], []
        for g, d in enumerate(DILATIONS):
            q_ref, k_ref, _ = new[3 * g:3 * g + 3]
            q = q_ref[:, lanes]
            s = _dot_nt(q.astype(BF16), keys[g]) * scale
            pen = slopes_ref[g * G_C + slot] * float(d)
            scores.append(jnp.where(col_slot == slot, s - pen * steps, NEG))
            self_scores.append(jnp.sum(q * k_ref[:, lanes], axis=-1, keepdims=True) * scale)
        m = self_scores[0]
        for x in self_scores[1:]:
            m = jnp.maximum(m, x)
        for x in scores:
            m = jnp.maximum(m, jnp.max(x, axis=-1, keepdims=True))
        acc = jnp.zeros((SEQ_PAD, HD_C), F32)
        total = jnp.zeros((SEQ_PAD, 1), F32)
        for g in range(N_GROUPS_C):
            p = jnp.exp(scores[g] - m)
            p_self = jnp.exp(self_scores[g] - m)
            total += jnp.sum(p, axis=-1, keepdims=True) + p_self
            acc += _dot(p.astype(BF16), vals[g]) + p_self * new[3 * g + 2][:, lanes]
        o_ref[:, lanes] = (acc / total).astype(o_ref.dtype)


def _dil_step(h, slopes, caches, layer, oc, *, n_seq, row0):
    rb0 = row0 // SEQ_PAD

    def cmap(g, j):
        return lambda b: (rb0 + b, (OFF_C + g * 3 * V_C + j * V_C) // V_C)

    in_specs = [pl.BlockSpec(memory_space=pltpu.SMEM)]
    in_specs += [pl.BlockSpec((SEQ_PAD, V_C), cmap(g, j)) for g in range(N_GROUPS_C) for j in range(3)]
    strided = []
    for c, d in zip(caches, DILATIONS):
        assert c.shape[3] == BAND * d
        strided.append(c.reshape(c.shape[0], c.shape[1], 2, BAND, d, G_C, HD_C))
        in_specs.append(pl.BlockSpec((None, None, 2, BAND, None, G_C, HD_C), lambda b: (layer, b, 0, 0, 0, 0, 0)))
    in_specs.append(pl.BlockSpec(memory_space=pl.ANY))
    nbytes = 2 * N_GROUPS_C * 2 * BAND * V_C * 4 + 64 * SEQ_PAD * BAND * G_C * 4
    return pl.pallas_call(
        _dil_step_body,
        out_shape=jax.ShapeDtypeStruct(oc.shape, oc.dtype),
        grid=(n_seq,),
        in_specs=in_specs,
        out_specs=pl.BlockSpec((SEQ_PAD, V_C), lambda b: (rb0 + b, 0)),
        input_output_aliases={13: 0},
        compiler_params=_params(("parallel",), nbytes),
        name="dilated_step",
    )(slopes, *([h] * 9), *strided, oc)


def _shift_body(c_ref, new_ref, o_ref):
    length = c_ref.shape[1]
    o_ref[:, 0:length - 1] = c_ref[:, 1:length]
    o_ref[:, length - 1:length] = new_ref[...]


def _shift_windows(cache, new_rows):
    depth, n, _, length = cache.shape[:4]
    blk = lambda rows: pl.BlockSpec((None, None, 2, rows, G_C, HD_C), lambda l, b: (l, b, 0, 0, 0, 0))
    return pl.pallas_call(
        _shift_body,
        out_shape=jax.ShapeDtypeStruct(cache.shape, cache.dtype),
        grid=(depth, n),
        in_specs=[blk(length), blk(1)],
        out_specs=blk(length),
        compiler_params=_params(("parallel", "parallel"), 4 * 2 * length * V_C * 4),
        name="shift_windows",
    )(cache, new_rows)


def _rope_tables(positions):
    half = DK_B // 2
    inv = ROPE_BASE ** (-np.arange(half, dtype=np.float64) / half)
    ang = np.asarray(positions, np.float64)[:, None] * inv[None, :]
    cos = np.concatenate([np.cos(ang), np.cos(ang)], axis=-1)
    sin = np.concatenate([-np.sin(ang), np.sin(ang)], axis=-1)
    return jnp.asarray(cos, F32), jnp.asarray(sin, F32)


def _layer(x, l, w, lw, state_gla, state_ret, caches, tabs, dims):
    n_p, seq, n_s, m_p, m = dims
    tm = _pick_tile(m, 704)
    tm_big = _pick_tile(m, 1100)
    xn = _rmsnorm(x, lw["g_mix"], BF16)
    h = _in_proj(xn, w["w_in_t"], l, tm=tm_big, tn=768)
    la, bsum = _gate(xn, w["w_in_t"], l, lw["w_alpha2"], lw["b_alpha"], tm=_pick_tile(m, 704, GLA_ROWS), block=GLA_ROWS)

    gla = functools.partial(_gla, h, la, g_gla=lw["g_gla"], out_rows=m)
    oa, sa_p = gla(bsum, n_seq=n_p, seq_rows=seq, rows=GLA_ROWS, row0=0, t_valid=None, state0=None, alias=None)
    oa, sa_s = gla(None, n_seq=n_s, seq_rows=SEQ_PAD, rows=SEQ_PAD, row0=m_p, t_valid=1, state0=state_gla, alias=oa)

    ret = functools.partial(_ret, h, out_rows=m)
    ob, sb_p = ret(tabs["cos_p"], tabs["sin_p"], tabs["lg"], n_seq=n_p, seq_rows=seq, rows=RET_ROWS, row0=0,
                   t_valid=None, state0=None, alias=None)
    ob, sb_s = ret(tabs["cos_s"], tabs["sin_s"], tabs["lg"], n_seq=n_s, seq_rows=SEQ_PAD, rows=SEQ_PAD, row0=m_p,
                   t_valid=1, state0=state_ret, alias=ob)

    oc = _dil_prefill(h, tabs["slopes"], n_seq=n_p, seq=seq, out_rows=m)
    oc = _dil_step(h, tabs["slopes"], caches, l, oc, n_seq=n_s, row0=m_p)

    gates = _matmul_ws(xn, w["w_merge"], l, tm=tm_big, tn=768, out_dtype=BF16, act=_sigmoid, name="merge_gates")
    merged = _merge_up(oa, ob, oc, gates, w["w_up_a"], w["w_up_b"], w["w_up_c"], l, tm=tm_big, tn=512)
    x, xf, xf_ss = _matmul_ws(merged, w["w_out"], l, tm=tm, tn=512, residual=x, norm_gain=lw["g_ffn"], name="out_proj")
    hid = _ffn_up(xf, xf_ss, w["w_ffn_gate"], w["w_ffn_up"], l, tm=_pick_tile(m, 1728), tn=256)
    d_ff = hid.shape[1]
    x = _matmul_ksplit_res(hid, w["w_ffn_down"], l, x, tm=tm, tn=512, tk=d_ff // 2, name="ffn_down")

    win_p, new_s = [], []
    for g in range(N_GROUPS_C):
        off_k = OFF_C + g * 3 * V_C + V_C
        keep = min(BAND * DILATIONS[g], seq)
        kv_p = jnp.stack([lax.slice(h, ((b + 1) * seq - keep, off_k), ((b + 1) * seq, off_k + 2 * V_C))
                          for b in range(n_p)])
        win_p.append(kv_p.reshape(n_p, keep, 2, G_C, HD_C).transpose(0, 2, 1, 3, 4))
        kv_s = lax.slice(h, (m_p, off_k), (m, off_k + 2 * V_C)).reshape(n_s, SEQ_PAD, 2, G_C, HD_C)[:, :1]
        new_s.append(kv_s.transpose(0, 2, 1, 3, 4))
    return x, (sa_p, sa_s, sb_p, sb_s), win_p, new_s


def kernel(x_prompt, x_sample, state_gla, state_ret, cache_win0, cache_win1, cache_win2, g_mix, w_in, w_alpha2,
           b_alpha, g_gla, w_merge, w_up_a, w_up_b, w_up_c, w_out, g_ffn, w_ffn_gate, w_ffn_up, w_ffn_down, g_final):
    n_p, seq, d = x_prompt.shape
    n_s = x_sample.shape[0]
    depth = g_mix.shape[0]
    assert d == D_MODEL and x_sample.shape[1] == 1 and seq % (BAND * DILATIONS[-1]) == 0 and seq % RET_ROWS == 0
    m_p = n_p * seq
    m = m_p + n_s * SEQ_PAD
    dims = (n_p, seq, n_s, m_p, m)

    xs = jnp.pad(x_sample, ((0, 0), (0, SEQ_PAD - 1), (0, 0)))
    x = jnp.concatenate([x_prompt.reshape(m_p, d), xs.reshape(n_s * SEQ_PAD, d)], axis=0)

    cos_p, sin_p = _rope_tables(np.arange(seq))
    cos_s, sin_s = _rope_tables(PAST_LEN + np.arange(SEQ_PAD))
    log_gamma = np.log1p(-(2.0 ** (-5.0 - np.arange(H_B, dtype=np.float64))))
    heads = np.arange(1, N_GROUPS_C * G_C + 1, dtype=np.float64)
    tabs = {
        "cos_p": cos_p, "sin_p": sin_p, "cos_s": cos_s, "sin_s": sin_s,
        "lg": jnp.asarray(np.broadcast_to(log_gamma[:, None], (H_B, RET_ROWS)), F32),
        "slopes": jnp.asarray(2.0 ** (-8.0 * heads / (N_GROUPS_C * G_C)), F32),
    }

    assert w_in.shape[2] == N_MAIN + GATE_RANK
    w = {"w_in_t": jnp.swapaxes(w_in, 1, 2), "w_merge": w_merge, "w_up_a": w_up_a, "w_up_b": w_up_b, "w_up_c": w_up_c,
         "w_out": w_out, "w_ffn_gate": w_ffn_gate, "w_ffn_up": w_ffn_up, "w_ffn_down": w_ffn_down.astype(BF16)}
    caches = (cache_win0, cache_win1, cache_win2)
    outs = {k: [] for k in ("gla_p", "gla_s", "ret_p", "ret_s")}
    win_p = [[] for _ in range(N_GROUPS_C)]
    win_s = [[] for _ in range(N_GROUPS_C)]
    for l in range(depth):
        lw = {
            "g_mix": g_mix[l], "g_gla": g_gla[l], "g_ffn": g_ffn[l], "b_alpha": b_alpha[l],
            "w_alpha2": w_alpha2[l],
        }
        x, (sa_p, sa_s, sb_p, sb_s), wp, ws = _layer(x, l, w, lw, state_gla[l], state_ret[l], caches, tabs, dims)
        outs["gla_p"].append(sa_p)
        outs["gla_s"].append(sa_s)
        outs["ret_p"].append(sb_p)
        outs["ret_s"].append(sb_s)
        for g in range(N_GROUPS_C):
            win_p[g].append(wp[g])
            win_s[g].append(ws[g])

    y_prompt = _rmsnorm(x, g_final, F32, rows=m_p).reshape(n_p, seq, d)
    y_sample = _rmsnorm(x[m_p:], g_final, F32).reshape(n_s, SEQ_PAD, d)[:, :1]
    win_s = [_shift_windows(caches[g], jnp.stack(win_s[g])) for g in range(N_GROUPS_C)]
    return (y_prompt, y_sample, jnp.stack(outs["gla_p"]), jnp.stack(outs["gla_s"]), jnp.stack(outs["ret_p"]),
            jnp.stack(outs["ret_s"]), jnp.stack(win_p[0]), win_s[0], jnp.stack(win_p[1]), win_s[1],
            jnp.stack(win_p[2]), win_s[2])
```

```python
import functools

import numpy as np
import jax
import jax.numpy as jnp
from jax import lax
from jax.experimental import pallas as pl
from jax.experimental.pallas import tpu as pltpu

F32 = jnp.float32
BF16 = jnp.bfloat16
HIGHEST = lax.Precision.HIGHEST

D_MODEL = 4096
H_A, DK_A, DV_A = 4, 256, 512
GATE_RANK, GATE_TAU = 16, 16.0
H_B, DK_B, DV_B = 8, 128, 256
ROPE_BASE = 10000.0
DILATIONS = (1, 4, 16)
N_GROUPS_C, G_C, HD_C = 3, 4, 128
BAND = 128
PAST_LEN = 16384
QK_A, V_A = H_A * DK_A, H_A * DV_A
QK_B, V_B = H_B * DK_B, H_B * DV_B
V_C = G_C * HD_C
C_COLS = N_GROUPS_C * 3 * V_C
EPS = 1e-6
NEG = -1e30

OFF_QA, OFF_KA, OFF_VA, OFF_RA = 0, QK_A, 2 * QK_A, 2 * QK_A + V_A
OFF_B = 2 * QK_A + 2 * V_A
OFF_QB, OFF_KB, OFF_VB, OFF_GB = OFF_B, OFF_B + QK_B, OFF_B + 2 * QK_B, OFF_B + 2 * QK_B + V_B
OFF_C = OFF_B + 2 * QK_B + 2 * V_B
N_MAIN = OFF_C + C_COLS

LANES = 128
SEQ_PAD = 16
GLA_ROWS = 128
RET_ROWS = 256
RET_HEADS = 8
SUBLANES = 8
VMEM_CAP = 56 * 1024 * 1024


def _vmem(nbytes):
    return int(min(VMEM_CAP, nbytes + (6 << 20)))


def _params(sem, nbytes):
    return pltpu.CompilerParams(dimension_semantics=sem, vmem_limit_bytes=_vmem(nbytes))


def _pick_tile(m, cap, align=16):
    best = align
    for t in range(align, cap + 1, align):
        if m % t == 0:
            best = t
    assert m % best == 0, (m, cap, align)
    return best


def _iota(shape, dim):
    return lax.broadcasted_iota(jnp.int32, shape, dim)


def _dot(a, b):
    return jnp.dot(a, b, preferred_element_type=F32)


def _dot_nt(a, b):
    return lax.dot_general(a, b, (((1,), (1,)), ((), ())), preferred_element_type=F32)


def _dot_tn(a, b):
    return lax.dot_general(a, b, (((0,), (0,)), ((), ())), preferred_element_type=F32)


def _sigmoid(x):
    return 1.0 / (1.0 + jnp.exp(-x))


def _rmsnorm_body(x_ref, g_ref, o_ref):
    x = x_ref[...]
    ms = jnp.mean(x * x, axis=-1, keepdims=True)
    o_ref[...] = (x * lax.rsqrt(ms + EPS) * g_ref[...]).astype(o_ref.dtype)


def _rmsnorm(x, g, out_dtype, rows=None):
    m, d = x.shape
    m = m if rows is None else rows
    tm = _pick_tile(m, 512)
    nbytes = 2 * tm * d * (4 + jnp.dtype(out_dtype).itemsize) + 2 * tm * d * 4
    return pl.pallas_call(
        _rmsnorm_body,
        out_shape=jax.ShapeDtypeStruct((m, d), out_dtype),
        grid=(m // tm,),
        in_specs=[pl.BlockSpec((tm, d), lambda i: (i, 0)), pl.BlockSpec((1, d), lambda i: (0, 0))],
        out_specs=pl.BlockSpec((tm, d), lambda i: (i, 0)),
        compiler_params=_params(("parallel",), nbytes),
        name="rmsnorm",
    )(x, g.reshape(1, d))


def _mm_ws_body(*refs, act, has_res, transposed):
    x_ref, w_ref = refs[:2]
    r_ref = refs[2] if has_res else None
    o_ref = refs[-1]
    if transposed:
        y = _dot_nt(x_ref[...], w_ref[0].astype(BF16))
    else:
        y = _dot(x_ref[...], w_ref[...].astype(BF16))
    if act is not None:
        y = act(y)
    if has_res:
        y = r_ref[...] + y
    o_ref[...] = y.astype(o_ref.dtype)


def _matmul_ws(x, w, layer, *, tm, tn, n=None, row_of=None, out_dtype=F32, act=None, residual=None,
               rows_outer=False, name="matmul"):
    m, k = x.shape
    transposed = row_of is not None
    n = w.shape[2] if n is None else n
    assert m % tm == 0 and n % tn == 0
    ij = (lambda a, b: (a, b)) if rows_outer else (lambda a, b: (b, a))
    if transposed:
        w_spec = pl.BlockSpec((pl.Element(1), pl.Element(tn), pl.Element(k)),
                              lambda a, b: (layer, pl.multiple_of(row_of(ij(a, b)[1]), SUBLANES), 0))
    else:
        w_spec = pl.BlockSpec((None, k, tn), lambda a, b: (layer, 0, ij(a, b)[1]))
    in_specs = [pl.BlockSpec((tm, k), lambda a, b: (ij(a, b)[0], 0)), w_spec]
    args = [x, w]
    osz = jnp.dtype(out_dtype).itemsize
    nbytes = 2 * (tm * k * 2 + k * tn * 4 + tm * tn * osz) + k * tn * 2 + 2 * tm * tn * 4
    if residual is not None:
        in_specs.append(pl.BlockSpec((tm, tn), lambda a, b: ij(a, b)))
        args.append(residual)
        nbytes += 2 * tm * tn * 4
    return pl.pallas_call(
        functools.partial(_mm_ws_body, act=act, has_res=residual is not None, transposed=transposed),
        out_shape=jax.ShapeDtypeStruct((m, n), out_dtype),
        grid=(m // tm, n // tn) if rows_outer else (n // tn, m // tm),
        in_specs=in_specs,
        out_specs=pl.BlockSpec((tm, tn), lambda a, b: ij(a, b)),
        compiler_params=_params(("parallel", "parallel"), nbytes),
        name=name,
    )(*args)


def _in_proj(x, w_in_t, layer, *, tm, tn):
    assert OFF_B % tn == 0
    return _matmul_ws(x, w_in_t, layer, tm=tm, tn=tn, n=N_MAIN, name="in_proj",
                      row_of=lambda j: j * tn + jnp.where(j >= OFF_B // tn, GATE_RANK, 0))


def _mm_ksplit_res_body(x_ref, w_ref, r_ref, o_ref, acc_ref, *, tm, n_k):
    kk = pl.program_id(1)
    rows = pl.ds(pl.multiple_of(pl.program_id(2) * tm, tm), tm)
    part = _dot(x_ref[...], w_ref[...])

    @pl.when(kk == 0)
    def _():
        acc_ref[rows, :] = r_ref[...] + part

    @pl.when(jnp.logical_and(kk > 0, kk < n_k - 1))
    def _():
        acc_ref[rows, :] += part

    @pl.when(kk == n_k - 1)
    def _():
        o_ref[...] = acc_ref[rows, :] + part


def _matmul_ksplit_res(x, w, layer, residual, *, tm, tn, tk, name):
    m, k = x.shape
    n = w.shape[2]
    assert m % tm == 0 and n % tn == 0 and k % tk == 0 and k // tk >= 2
    n_k, n_i = k // tk, m // tm
    nbytes = 2 * (tm * tk * 2 + tk * tn * 2 + 2 * tm * tn * 4) + m * tn * 4 + 2 * tm * tn * 4
    return pl.pallas_call(
        functools.partial(_mm_ksplit_res_body, tm=tm, n_k=n_k),
        out_shape=jax.ShapeDtypeStruct((m, n), F32),
        grid=(n // tn, n_k, n_i),
        in_specs=[
            pl.BlockSpec((tm, tk), lambda j, kk, i: (i, kk)),
            pl.BlockSpec((None, tk, tn), lambda j, kk, i: (layer, kk, j)),
            pl.BlockSpec((tm, tn), lambda j, kk, i: (jnp.where(kk == 0, i, n_i - 1), j)),
        ],
        out_specs=pl.BlockSpec((tm, tn), lambda j, kk, i: (jnp.where(kk == n_k - 1, i, 0), j)),
        scratch_shapes=[pltpu.VMEM((m, tn), F32)],
        compiler_params=_params(("parallel", "arbitrary", "arbitrary"), nbytes),
        name=name,
    )(x, w, residual)


def _merge_up_body(oa_ref, ob_ref, oc_ref, ga_ref, gb_ref, gc_ref, wa_ref, wb_ref, wc_ref, o_ref):
    acc = ga_ref[...].astype(F32) * _dot(oa_ref[...], wa_ref[...].astype(BF16))
    acc += gb_ref[...].astype(F32) * _dot(ob_ref[...], wb_ref[...].astype(BF16))
    acc += gc_ref[...].astype(F32) * _dot(oc_ref[...], wc_ref[...].astype(BF16))
    o_ref[...] = acc.astype(o_ref.dtype)


def _merge_up(oa, ob, oc, gates, w_up_a, w_up_b, w_up_c, layer, *, tm, tn):
    m = oa.shape[0]
    d = w_up_a.shape[2]
    nt = d // tn
    assert m % tm == 0 and d % tn == 0

    def row(width):
        return pl.BlockSpec((tm, width), lambda j, i: (i, 0))

    def gate(branch):
        return pl.BlockSpec((tm, tn), lambda j, i: (i, branch * nt + j))

    def panel(kdim):
        return pl.BlockSpec((None, kdim, tn), lambda j, i: (layer, 0, j))

    kk = V_A + V_B + V_C
    nbytes = 2 * (tm * kk * 2 + 3 * tm * tn * 2 + kk * tn * 4 + tm * tn * 2) + kk * tn * 2 + 6 * tm * tn * 4
    return pl.pallas_call(
        _merge_up_body,
        out_shape=jax.ShapeDtypeStruct((m, d), BF16),
        grid=(nt, m // tm),
        in_specs=[row(V_A), row(V_B), row(V_C), gate(0), gate(1), gate(2), panel(V_A), panel(V_B), panel(V_C)],
        out_specs=pl.BlockSpec((tm, tn), lambda j, i: (i, j)),
        compiler_params=_params(("parallel", "parallel"), nbytes),
        name="merge_up",
    )(oa, ob, oc, gates, gates, gates, w_up_a, w_up_b, w_up_c)


def _ffn_up_body(x_ref, wg_ref, wu_ref, o_ref):
    x = x_ref[...]
    a = _dot(x, wg_ref[...].astype(BF16))
    o_ref[...] = (a * _sigmoid(a) * _dot(x, wu_ref[...].astype(BF16))).astype(o_ref.dtype)


def _ffn_up(x, wg, wu, layer, *, tm, tn):
    m, k = x.shape
    n = wg.shape[2]
    assert m % tm == 0 and n % tn == 0
    nbytes = 2 * (tm * k * 2 + 2 * k * tn * 4 + tm * tn * 2) + 2 * k * tn * 2 + 3 * tm * tn * 4
    return pl.pallas_call(
        _ffn_up_body,
        out_shape=jax.ShapeDtypeStruct((m, n), BF16),
        grid=(m // tm, n // tn),
        in_specs=[pl.BlockSpec((tm, k), lambda i, j: (i, 0)), pl.BlockSpec((None, k, tn), lambda i, j: (layer, 0, j)),
                  pl.BlockSpec((None, k, tn), lambda i, j: (layer, 0, j))],
        out_specs=pl.BlockSpec((tm, tn), lambda i, j: (i, j)),
        compiler_params=_params(("parallel", "parallel"), nbytes),
        name="ffn_up",
    )(x, wg, wu)


def _gla_body(*refs, rows, n_t, t_valid, heads, has_bsum, has_s0, has_alias):
    it = iter(refs)
    q_ref, k_ref, v_ref, r_ref, la_ref, g_ref = [next(it) for _ in range(6)]
    bsum_ref = next(it) if has_bsum else None
    s0_ref = next(it) if has_s0 else None
    if has_alias:
        next(it)
    o_ref, sout_ref, st_scr, b_scr, mask_scr = [next(it) for _ in range(5)]
    t = pl.program_id(1)

    @pl.when(t == 0)
    def _():
        for hh in range(heads):
            st_scr[hh] = s0_ref[hh].T if has_s0 else jnp.zeros(st_scr.shape[1:], F32)
        t_idx = _iota((rows, rows), 0)
        s_idx = _iota((rows, rows), 1)
        mask_scr[0] = (t_idx == s_idx).astype(F32)
        for lv in range(1, mask_scr.shape[0]):
            m = rows >> lv
            same_pair = (t_idx >> (2 * m).bit_length() - 1) == (s_idx >> (2 * m).bit_length() - 1)
            upper_lower = jnp.logical_and((t_idx & (2 * m - 1)) >= m, (s_idx & (2 * m - 1)) < m)
            mask_scr[lv] = jnp.logical_and(same_pair, upper_lower).astype(F32)

    for hh in range(heads):
        kk = slice(hh * DK_A, (hh + 1) * DK_A)
        vv = slice(hh * DV_A, (hh + 1) * DV_A)
        _gla_head(q_ref.at[:, kk], k_ref.at[:, kk], v_ref.at[:, vv], r_ref.at[:, vv], la_ref.at[:, kk], g_ref,
                  None if bsum_ref is None else bsum_ref.at[:, kk], o_ref.at[:, vv], st_scr.at[hh], b_scr.at[hh],
                  mask_scr, rows=rows, t_valid=t_valid)

    @pl.when(t == n_t - 1)
    def _():
        for hh in range(heads):
            sout_ref[hh] = st_scr[hh].T


def _gla_head(q_ref, k_ref, v_ref, r_ref, la_ref, g_ref, bsum_ref, o_ref, st_scr, b_scr, mask_scr, *, rows, t_valid):
    la = la_ref[...]
    row_k = _iota((rows, DK_A), 0)
    if t_valid is not None:
        la = jnp.where(row_k < t_valid, la, 0.0)
    if bsum_ref is not None:
        b = bsum_ref[...]
    else:
        tri = (_iota((rows, rows), 0) >= _iota((rows, rows), 1)).astype(F32)
        b = jnp.dot(tri, la, precision=HIGHEST, preferred_element_type=F32)
    b_scr[...] = b

    q = q_ref[...] * (DK_A ** -0.5)
    k = k_ref[...]
    v16 = v_ref[...].astype(BF16)
    st = st_scr[...]
    o = _dot_nt((q * jnp.exp(b)).astype(BF16), st.astype(BF16))

    scores = mask_scr[0] * _dot_nt(q.astype(BF16), k.astype(BF16))
    m = rows // 2
    level = 0
    while m >= 1:
        level += 1
        sub = row_k & (2 * m - 1)
        if m >= 4:
            pieces = [jnp.broadcast_to(b_scr[pl.ds(p * 2 * m + m - 1, 1), :], (2 * m, DK_A)) for p in range(rows // (2 * m))]
            log_fac = -jnp.abs(b - (pieces[0] if len(pieces) == 1 else jnp.concatenate(pieces, axis=0)))
        elif m == 2:
            la_next = pltpu.roll(la, rows - 1, 0)
            la_prev = pltpu.roll(la, 1, 0)
            log_fac = jnp.where(sub == 0, la_next, jnp.where(sub == 1, 0.0, jnp.where(sub == 2, la, la + la_prev)))
        else:
            log_fac = jnp.where(sub == 1, la, 0.0)
        mixed = (jnp.where(sub >= m, q, k) * jnp.exp(log_fac)).astype(BF16)
        scores = scores + mask_scr[level] * _dot_nt(mixed, mixed)
        m //= 2
    o = o + _dot(scores.astype(BF16), v16)

    b_last = b_scr[pl.ds(rows - 1, 1), :]
    k_st = k if t_valid is None else jnp.where(row_k < t_valid, k, 0.0)
    st_new = st * jnp.exp(b_last) + _dot_tn(v16, (k_st * jnp.exp(b_last - b)).astype(BF16))
    st_scr[...] = st_new

    ms = jnp.mean(o * o, axis=-1, keepdims=True)
    r = r_ref[...]
    o_ref[...] = (r * _sigmoid(r) * (o * lax.rsqrt(ms + EPS) * g_ref[...])).astype(o_ref.dtype)


def _gla(h, la, bsum, g_gla, *, n_seq, seq_rows, rows, row0, t_valid, state0, out_rows, alias, heads=H_A):
    n_t = seq_rows // rows
    rb0 = row0 // rows
    groups = H_A // heads
    wk, wv = heads * DK_A, heads * DV_A

    def rmap(blk0):
        return lambda s, t: (rb0 + (s // groups) * n_t + t, blk0 + s % groups)

    smap = lambda s, t: (s // groups, s % groups, 0, 0)
    in_specs = [
        pl.BlockSpec((rows, wk), rmap(OFF_QA // wk)),
        pl.BlockSpec((rows, wk), rmap(OFF_KA // wk)),
        pl.BlockSpec((rows, wv), rmap(OFF_VA // wv)),
        pl.BlockSpec((rows, wv), rmap(OFF_RA // wv)),
        pl.BlockSpec((rows, wk), rmap(0)),
        pl.BlockSpec((1, DV_A), lambda s, t: (0, 0)),
    ]
    args = [h, h, h, h, la, g_gla.reshape(1, DV_A)]
    if bsum is not None:
        in_specs.append(pl.BlockSpec((rows, wk), rmap(0)))
        args.append(bsum)
    if state0 is not None:
        in_specs.append(pl.BlockSpec((None, heads, DK_A, DV_A), smap))
        args.append(state0)
    aliases = {}
    if alias is not None:
        in_specs.append(pl.BlockSpec(memory_space=pl.ANY))
        aliases = {len(args): 0}
        args.append(alias)
    nbytes = heads * (2 * rows * (4 * DK_A + 2 * DV_A) * 4 + 2 * rows * DV_A * 2 + 5 * DK_A * DV_A * 4
                      + rows * DK_A * 4 + 16 * rows * DV_A * 4)
    return pl.pallas_call(
        functools.partial(_gla_body, rows=rows, n_t=n_t, t_valid=t_valid, heads=heads, has_bsum=bsum is not None,
                          has_s0=state0 is not None, has_alias=alias is not None),
        out_shape=(jax.ShapeDtypeStruct((out_rows, V_A), BF16), jax.ShapeDtypeStruct((n_seq, H_A, DK_A, DV_A), F32)),
        grid=(n_seq * groups, n_t),
        in_specs=in_specs,
        out_specs=(pl.BlockSpec((rows, wv), rmap(0)), pl.BlockSpec((None, heads, DK_A, DV_A), smap)),
        scratch_shapes=[pltpu.VMEM((heads, DV_A, DK_A), F32), pltpu.VMEM((heads, rows, DK_A), F32),
                        pltpu.VMEM((rows.bit_length(), rows, rows), F32)],
        input_output_aliases=aliases,
        compiler_params=_params(("parallel", "arbitrary"), nbytes),
        name="gla",
    )(*args)


def _gate_body(x_ref, w_ref, wa_hi_ref, wa_lo_ref, ba_ref, la_ref, bsum_ref, *, block):
    za = _dot_nt(x_ref[...], w_ref[0].astype(BF16))
    za_hi = za.astype(BF16)
    za_lo = (za - za_hi.astype(F32)).astype(BF16)
    wa_hi = wa_hi_ref[...]
    z = _dot(za_hi, wa_hi) + (_dot(za_lo, wa_hi) + _dot(za_hi, wa_lo_ref[...])) + ba_ref[...]
    la = (jnp.minimum(z, 0.0) - jnp.log(1.0 + jnp.exp(-jnp.abs(z)))) * (1.0 / GATE_TAU)
    la_ref[...] = la
    tri = (_iota((block, block), 0) >= _iota((block, block), 1)).astype(BF16)
    la_hi = la.astype(BF16)
    rest = la - la_hi.astype(F32)
    la_mid = rest.astype(BF16)
    la_lo = (rest - la_mid.astype(F32)).astype(BF16)
    for p in range(x_ref.shape[0] // block):
        rows = slice(p * block, (p + 1) * block)
        bsum_ref[rows, :] = _dot(tri, la_hi[rows]) + (_dot(tri, la_mid[rows]) + _dot(tri, la_lo[rows]))


def _gate(x, w_in_t, layer, w_alpha2, b_alpha, *, tm, block):
    m, k = x.shape
    assert m % tm == 0 and tm % block == 0 and OFF_B % SUBLANES == 0
    wa = jnp.pad(w_alpha2, ((0, LANES - GATE_RANK), (0, 0)))
    wa_hi = wa.astype(BF16)
    wa_lo = (wa - wa_hi.astype(F32)).astype(BF16)
    full = lambda shape: pl.BlockSpec(shape, lambda i: (0, 0))
    nbytes = 2 * (tm * k * 2 + k * LANES * 4 + 2 * tm * QK_A * 4) + k * LANES * 2 + 8 * tm * QK_A * 4
    return pl.pallas_call(
        functools.partial(_gate_body, block=block),
        out_shape=(jax.ShapeDtypeStruct((m, QK_A), F32), jax.ShapeDtypeStruct((m, QK_A), F32)),
        grid=(m // tm,),
        in_specs=[pl.BlockSpec((tm, k), lambda i: (i, 0)),
                  pl.BlockSpec((pl.Element(1), pl.Element(LANES), pl.Element(k)), lambda i: (layer, OFF_B, 0)),
                  full((LANES, QK_A)), full((LANES, QK_A)), full((1, QK_A))],
        out_specs=(pl.BlockSpec((tm, QK_A), lambda i: (i, 0)), pl.BlockSpec((tm, QK_A), lambda i: (i, 0))),
        compiler_params=_params(("parallel",), nbytes),
        name="gla_gate",
    )(x, w_in_t, wa_hi, wa_lo, b_alpha.reshape(1, QK_A))


def _ret_body(*refs, rows, n_t, t_valid, heads, has_s0, has_alias):
    it = iter(refs)
    q_ref, k_ref, v_ref, g_ref, cos_ref, sin_ref, lg_ref = [next(it) for _ in range(7)]
    s0_ref = next(it) if has_s0 else None
    if has_alias:
        next(it)
    o_ref, sout_ref, st_scr, rel_scr, dq_scr, dk_scr = [next(it) for _ in range(6)]
    t = pl.program_id(1)
    head0 = (pl.program_id(0) % (H_B // heads)) * heads
    n_valid = rows if t_valid is None else t_valid
    row_k = _iota((rows, DK_B), 0)

    @pl.when(t == 0)
    def _():
        steps = jnp.minimum(row_k + 1, n_valid).astype(F32)
        st = jnp.minimum(_iota((rows, rows), 0) + 1, n_valid)
        ss = jnp.minimum(_iota((rows, rows), 1) + 1, n_valid)
        causal = _iota((rows, rows), 0) >= _iota((rows, rows), 1)
        for hh in range(heads):
            st_scr[hh] = s0_ref[hh] if has_s0 else jnp.zeros(st_scr.shape[1:], F32)
            lg = lg_ref[pl.ds(head0 + hh, 1), :]
            dq_scr[hh] = jnp.exp(lg[:, :DK_B] * steps)
            dk_scr[hh] = jnp.exp(lg[:, :DK_B] * (n_valid - steps))
            rel_scr[hh] = jnp.exp(jnp.where(causal, lg[:, :rows] * (st - ss).astype(F32), NEG))

    for hh in range(heads):
        kk = slice(hh * DK_B, (hh + 1) * DK_B)
        vv = slice(hh * DV_B, (hh + 1) * DV_B)
        _ret_head(q_ref.at[:, kk], k_ref.at[:, kk], v_ref.at[:, vv], g_ref.at[:, vv], cos_ref, sin_ref,
                  lg_ref.at[pl.ds(head0 + hh, 1), :], o_ref.at[:, vv], st_scr.at[hh], rel_scr.at[hh], dq_scr.at[hh],
                  dk_scr.at[hh], rows=rows, t_valid=t_valid)

    @pl.when(t == n_t - 1)
    def _():
        for hh in range(heads):
            sout_ref[hh] = st_scr[hh]


def _ret_head(q_ref, k_ref, v_ref, g_ref, cos_ref, sin_ref, lg_ref, o_ref, st_scr, rel_scr, dq_scr, dk_scr, *,
              rows, t_valid):
    lg = lg_ref[...]
    n_valid = rows if t_valid is None else t_valid
    row_k = _iota((rows, DK_B), 0)
    cos = cos_ref[...]
    sin = sin_ref[...]

    def rot(x):
        return x * cos + pltpu.roll(x, DK_B // 2, 1) * sin

    q = rot(q_ref[...])
    k = rot(k_ref[...]) * (DK_B ** -0.5)
    if t_valid is not None:
        k = jnp.where(row_k < t_valid, k, 0.0)
    decay_q = dq_scr[...]
    decay_k = dk_scr[...]

    v16 = v_ref[...].astype(BF16)
    state = st_scr[...]
    scores = _dot_nt(q.astype(BF16), k.astype(BF16)) * rel_scr[...]
    o = _dot(scores.astype(BF16), v16) + _dot((q * decay_q).astype(BF16), state.astype(BF16))
    state_new = state * jnp.exp(lg[:, :DV_B] * float(n_valid)) + _dot_tn((k * decay_k).astype(BF16), v16)
    st_scr[...] = state_new

    ms = jnp.mean(o * o, axis=-1, keepdims=True)
    g = g_ref[...]
    o_ref[...] = (g * _sigmoid(g) * (o * lax.rsqrt(ms + EPS))).astype(o_ref.dtype)


def _ret(h, cos, sin, lg_tab, *, n_seq, seq_rows, rows, row0, t_valid, state0, out_rows, alias, heads=RET_HEADS):
    n_t = seq_rows // rows
    rb0 = row0 // rows
    groups = H_B // heads
    wk, wv = heads * DK_B, heads * DV_B

    def rmap(blk0):
        return lambda s, t: (rb0 + (s // groups) * n_t + t, blk0 + s % groups)

    smap = lambda s, t: (s // groups, s % groups, 0, 0)
    in_specs = [
        pl.BlockSpec((rows, wk), rmap(OFF_QB // wk)),
        pl.BlockSpec((rows, wk), rmap(OFF_KB // wk)),
        pl.BlockSpec((rows, wv), rmap(OFF_VB // wv)),
        pl.BlockSpec((rows, wv), rmap(OFF_GB // wv)),
        pl.BlockSpec((rows, DK_B), lambda s, t: (t, 0)),
        pl.BlockSpec((rows, DK_B), lambda s, t: (t, 0)),
        pl.BlockSpec((H_B, RET_ROWS), lambda s, t: (0, 0)),
    ]
    args = [h, h, h, h, cos, sin, lg_tab]
    if state0 is not None:
        in_specs.append(pl.BlockSpec((None, heads, DK_B, DV_B), smap))
        args.append(state0)
    aliases = {}
    if alias is not None:
        in_specs.append(pl.BlockSpec(memory_space=pl.ANY))
        aliases = {len(args): 0}
        args.append(alias)
    nbytes = heads * (2 * rows * (2 * DK_B + 2 * DV_B) * 4 + 2 * rows * DV_B * 2 + 5 * DK_B * DV_B * 4
                      + 6 * rows * rows * 4 + 8 * rows * DV_B * 4) + 4 * rows * DK_B * 4
    return pl.pallas_call(
        functools.partial(_ret_body, rows=rows, n_t=n_t, t_valid=t_valid, heads=heads, has_s0=state0 is not None,
                          has_alias=alias is not None),
        out_shape=(jax.ShapeDtypeStruct((out_rows, V_B), BF16), jax.ShapeDtypeStruct((n_seq, H_B, DK_B, DV_B), F32)),
        grid=(n_seq * groups, n_t),
        in_specs=in_specs,
        out_specs=(pl.BlockSpec((rows, wv), rmap(0)), pl.BlockSpec((None, heads, DK_B, DV_B), smap)),
        scratch_shapes=[pltpu.VMEM((heads, DK_B, DV_B), F32), pltpu.VMEM((heads, rows, rows), F32),
                        pltpu.VMEM((heads, rows, DK_B), F32), pltpu.VMEM((heads, rows, DK_B), F32)],
        input_output_aliases=aliases,
        compiler_params=_params(("parallel", "arbitrary"), nbytes),
        name="retention",
    )(*args)


def _dil_prefill_body(slopes_ref, *refs, seq):
    qkv = refs[:9]
    o_ref, og_scr, lse_scr = refs[9:]
    slot = pl.program_id(1)
    scale = HD_C ** -0.5
    rr = _iota((BAND, BAND), 0)
    cc = _iota((BAND, BAND), 1)
    dist_cur = (rr - cc).astype(F32)
    ok_cur = rr >= cc
    ok_prev = cc >= rr
    n_units = seq // BAND
    for g, d in enumerate(DILATIONS):
        q_ref, k_ref, v_ref = qkv[3 * g:3 * g + 3]
        pen = slopes_ref[g * G_C + slot] * float(d)
        shift = d.bit_length() - 1

        def rows_of(start, d=d):
            return pl.ds(start, BAND) if d == 1 else pl.ds(start, BAND, stride=d)

        has_prev = seq // (BAND * d) > 1

        def unit(u, carry, q_ref=q_ref, k_ref=k_ref, v_ref=v_ref, pen=pen, d=d, shift=shift, rows_of=rows_of, g=g,
                 has_prev=has_prev):
            blk = u >> shift
            start = (u & (d - 1)) + blk * (BAND * d)
            q = q_ref[rows_of(start), :].astype(BF16)
            keys = k_ref[rows_of(start), :].astype(BF16)
            vals = v_ref[rows_of(start), :].astype(BF16)
            if has_prev:
                prev = jnp.maximum(start - BAND * d, 0)
                keys = jnp.concatenate([k_ref[rows_of(prev), :].astype(BF16), keys], axis=0)
                vals = jnp.concatenate([v_ref[rows_of(prev), :].astype(BF16), vals], axis=0)
                bias = jnp.concatenate([jnp.where(ok_prev, -pen * (dist_cur + float(BAND)), NEG)
                                        + jnp.where(blk > 0, 0.0, NEG),
                                        jnp.where(ok_cur, -pen * dist_cur, NEG)], axis=1)
            else:
                bias = jnp.where(ok_cur, -pen * dist_cur, NEG)
            s = _dot_nt(q, keys) * scale + bias
            m = jnp.max(s, axis=-1, keepdims=True)
            p = jnp.exp(s - m)
            l = jnp.sum(p, axis=-1, keepdims=True)
            acc = _dot(p.astype(BF16), vals)
            og_scr[g, rows_of(start), :] = acc / l
            lse_scr[g, rows_of(start), :] = jnp.broadcast_to(m + jnp.log(l), (BAND, HD_C))
            return carry

        lax.fori_loop(0, n_units, unit, 0, unroll=True)

    lse = [lse_scr[g] for g in range(N_GROUPS_C)]
    top = jnp.maximum(jnp.maximum(lse[0], lse[1]), lse[2])
    w = [jnp.exp(x - top) for x in lse]
    total = w[0] + w[1] + w[2]
    o_ref[...] = ((w[0] * og_scr[0] + w[1] * og_scr[1] + w[2] * og_scr[2]) / total).astype(o_ref.dtype)


def _dil_prefill(h, slopes, *, n_seq, seq, out_rows):
    def cmap(g, j):
        return lambda b, s: (b, (OFF_C + g * 3 * V_C + j * V_C) // HD_C + s)

    in_specs = [pl.BlockSpec(memory_space=pltpu.SMEM)]
    in_specs += [pl.BlockSpec((seq, HD_C), cmap(g, j)) for g in range(N_GROUPS_C) for j in range(3)]
    nbytes = 2 * 9 * seq * HD_C * 4 + 2 * seq * HD_C * 2 + 2 * N_GROUPS_C * seq * HD_C * 4 + 8 * seq * HD_C * 4
    return pl.pallas_call(
        functools.partial(_dil_prefill_body, seq=seq),
        out_shape=jax.ShapeDtypeStruct((out_rows, V_C), BF16),
        grid=(n_seq, G_C),
        in_specs=in_specs,
        out_specs=pl.BlockSpec((seq, HD_C), lambda b, s: (b, s)),
        scratch_shapes=[pltpu.VMEM((N_GROUPS_C, seq, HD_C), F32), pltpu.VMEM((N_GROUPS_C, seq, HD_C), F32)],
        compiler_params=_params(("parallel", "parallel"), nbytes),
        name="dilated_prefill",
    )(slopes, *([h] * 9))


def _dil_step_body(slopes_ref, *refs):
    new = refs[:9]
    caches = refs[9:12]
    o_ref = refs[13]
    scale = HD_C ** -0.5
    col = _iota((SEQ_PAD, BAND * G_C), 1)
    col_slot = col & (G_C - 1)
    steps = (BAND - (col >> (G_C.bit_length() - 1))).astype(F32)
    keys = [c[0].reshape(BAND * G_C, HD_C).astype(BF16) for c in caches]
    vals = [c[1].reshape(BAND * G_C, HD_C).astype(BF16) for c in caches]
    for slot in range(G_C):
        lanes = slice(slot * HD_C, (slot + 1) * HD_C)
        scores, self_scores = [], []
        for g, d in enumerate(DILATIONS):
            q_ref, k_ref, _ = new[3 * g:3 * g + 3]
            q = q_ref[:, lanes]
            s = _dot_nt(q.astype(BF16), keys[g]) * scale
            pen = slopes_ref[g * G_C + slot] * float(d)
            scores.append(jnp.where(col_slot == slot, s - pen * steps, NEG))
            self_scores.append(jnp.sum(q * k_ref[:, lanes], axis=-1, keepdims=True) * scale)
        m = self_scores[0]
        for x in self_scores[1:]:
            m = jnp.maximum(m, x)
        for x in scores:
            m = jnp.maximum(m, jnp.max(x, axis=-1, keepdims=True))
        acc = jnp.zeros((SEQ_PAD, HD_C), F32)
        total = jnp.zeros((SEQ_PAD, 1), F32)
        for g in range(N_GROUPS_C):
            p = jnp.exp(scores[g] - m)
            p_self = jnp.exp(self_scores[g] - m)
            total += jnp.sum(p, axis=-1, keepdims=True) + p_self
            acc += _dot(p.astype(BF16), vals[g]) + p_self * new[3 * g + 2][:, lanes]
        o_ref[:, lanes] = (acc / total).astype(o_ref.dtype)


def _dil_step(h, slopes, caches, layer, oc, *, n_seq, row0):
    rb0 = row0 // SEQ_PAD

    def cmap(g, j):
        return lambda b: (rb0 + b, (OFF_C + g * 3 * V_C + j * V_C) // V_C)

    in_specs = [pl.BlockSpec(memory_space=pltpu.SMEM)]
    in_specs += [pl.BlockSpec((SEQ_PAD, V_C), cmap(g, j)) for g in range(N_GROUPS_C) for j in range(3)]
    strided = []
    for c, d in zip(caches, DILATIONS):
        assert c.shape[3] == BAND * d
        strided.append(c.reshape(c.shape[0], c.shape[1], 2, BAND, d, G_C, HD_C))
        in_specs.append(pl.BlockSpec((None, None, 2, BAND, None, G_C, HD_C), lambda b: (layer, b, 0, 0, 0, 0, 0)))
    in_specs.append(pl.BlockSpec(memory_space=pl.ANY))
    nbytes = 2 * N_GROUPS_C * 2 * BAND * V_C * 4 + 64 * SEQ_PAD * BAND * G_C * 4
    return pl.pallas_call(
        _dil_step_body,
        out_shape=jax.ShapeDtypeStruct(oc.shape, oc.dtype),
        grid=(n_seq,),
        in_specs=in_specs,
        out_specs=pl.BlockSpec((SEQ_PAD, V_C), lambda b: (rb0 + b, 0)),
        input_output_aliases={13: 0},
        compiler_params=_params(("parallel",), nbytes),
        name="dilated_step",
    )(slopes, *([h] * 9), *strided, oc)


def _shift_body(c_ref, new_ref, o_ref):
    length = c_ref.shape[1]
    o_ref[:, 0:length - 1] = c_ref[:, 1:length]
    o_ref[:, length - 1:length] = new_ref[...]


def _shift_windows(cache, new_rows):
    depth, n, _, length = cache.shape[:4]
    blk = lambda rows: pl.BlockSpec((None, None, 2, rows, G_C, HD_C), lambda l, b: (l, b, 0, 0, 0, 0))
    return pl.pallas_call(
        _shift_body,
        out_shape=jax.ShapeDtypeStruct(cache.shape, cache.dtype),
        grid=(depth, n),
        in_specs=[blk(length), blk(1)],
        out_specs=blk(length),
        compiler_params=_params(("parallel", "parallel"), 4 * 2 * length * V_C * 4),
        name="shift_windows",
    )(cache, new_rows)


def _rope_tables(positions):
    half = DK_B // 2
    inv = ROPE_BASE ** (-np.arange(half, dtype=np.float64) / half)
    ang = np.asarray(positions, np.float64)[:, None] * inv[None, :]
    cos = np.concatenate([np.cos(ang), np.cos(ang)], axis=-1)
    sin = np.concatenate([-np.sin(ang), np.sin(ang)], axis=-1)
    return jnp.asarray(cos, F32), jnp.asarray(sin, F32)


def _layer(x, l, w, lw, state_gla, state_ret, caches, tabs, dims):
    n_p, seq, n_s, m_p, m = dims
    tm = _pick_tile(m, 704)
    tm_big = _pick_tile(m, 1100)
    xn = _rmsnorm(x, lw["g_mix"], BF16)
    h = _in_proj(xn, w["w_in_t"], l, tm=tm_big, tn=768)
    la, bsum = _gate(xn, w["w_in_t"], l, lw["w_alpha2"], lw["b_alpha"], tm=_pick_tile(m, 704, GLA_ROWS), block=GLA_ROWS)

    gla = functools.partial(_gla, h, la, g_gla=lw["g_gla"], out_rows=m)
    oa, sa_p = gla(bsum, n_seq=n_p, seq_rows=seq, rows=GLA_ROWS, row0=0, t_valid=None, state0=None, alias=None)
    oa, sa_s = gla(None, n_seq=n_s, seq_rows=SEQ_PAD, rows=SEQ_PAD, row0=m_p, t_valid=1, state0=state_gla, alias=oa)

    ret = functools.partial(_ret, h, out_rows=m)
    ob, sb_p = ret(tabs["cos_p"], tabs["sin_p"], tabs["lg"], n_seq=n_p, seq_rows=seq, rows=RET_ROWS, row0=0,
                   t_valid=None, state0=None, alias=None)
    ob, sb_s = ret(tabs["cos_s"], tabs["sin_s"], tabs["lg"], n_seq=n_s, seq_rows=SEQ_PAD, rows=SEQ_PAD, row0=m_p,
                   t_valid=1, state0=state_ret, alias=ob)

    oc = _dil_prefill(h, tabs["slopes"], n_seq=n_p, seq=seq, out_rows=m)
    oc = _dil_step(h, tabs["slopes"], caches, l, oc, n_seq=n_s, row0=m_p)

    gates = _matmul_ws(xn, w["w_merge"], l, tm=tm_big, tn=768, out_dtype=BF16, act=_sigmoid, name="merge_gates")
    merged = _merge_up(oa, ob, oc, gates, w["w_up_a"], w["w_up_b"], w["w_up_c"], l, tm=tm_big, tn=512)
    x = _matmul_ws(merged, w["w_out"], l, tm=_pick_tile(m, 1728), tn=256, residual=x, rows_outer=True, name="out_proj")
    xf = _rmsnorm(x, lw["g_ffn"], BF16)
    hid = _ffn_up(xf, w["w_ffn_gate"], w["w_ffn_up"], l, tm=_pick_tile(m, 1728), tn=256)
    d_ff = hid.shape[1]
    x = _matmul_ksplit_res(hid, w["w_ffn_down"], l, x, tm=tm, tn=512, tk=d_ff // 2, name="ffn_down")

    win_p, new_s = [], []
    for g in range(N_GROUPS_C):
        off_k = OFF_C + g * 3 * V_C + V_C
        keep = min(BAND * DILATIONS[g], seq)
        kv_p = jnp.stack([lax.slice(h, ((b + 1) * seq - keep, off_k), ((b + 1) * seq, off_k + 2 * V_C))
                          for b in range(n_p)])
        win_p.append(kv_p.reshape(n_p, keep, 2, G_C, HD_C).transpose(0, 2, 1, 3, 4))
        kv_s = lax.slice(h, (m_p, off_k), (m, off_k + 2 * V_C)).reshape(n_s, SEQ_PAD, 2, G_C, HD_C)[:, :1]
        new_s.append(kv_s.transpose(0, 2, 1, 3, 4))
    return x, (sa_p, sa_s, sb_p, sb_s), win_p, new_s


def kernel(x_prompt, x_sample, state_gla, state_ret, cache_win0, cache_win1, cache_win2, g_mix, w_in, w_alpha2,
           b_alpha, g_gla, w_merge, w_up_a, w_up_b, w_up_c, w_out, g_ffn, w_ffn_gate, w_ffn_up, w_ffn_down, g_final):
    n_p, seq, d = x_prompt.shape
    n_s = x_sample.shape[0]
    depth = g_mix.shape[0]
    assert d == D_MODEL and x_sample.shape[1] == 1 and seq % (BAND * DILATIONS[-1]) == 0 and seq % RET_ROWS == 0
    m_p = n_p * seq
    m = m_p + n_s * SEQ_PAD
    dims = (n_p, seq, n_s, m_p, m)

    xs = jnp.pad(x_sample, ((0, 0), (0, SEQ_PAD - 1), (0, 0)))
    x = jnp.concatenate([x_prompt.reshape(m_p, d), xs.reshape(n_s * SEQ_PAD, d)], axis=0)

    cos_p, sin_p = _rope_tables(np.arange(seq))
    cos_s, sin_s = _rope_tables(PAST_LEN + np.arange(SEQ_PAD))
    log_gamma = np.log1p(-(2.0 ** (-5.0 - np.arange(H_B, dtype=np.float64))))
    heads = np.arange(1, N_GROUPS_C * G_C + 1, dtype=np.float64)
    tabs = {
        "cos_p": cos_p, "sin_p": sin_p, "cos_s": cos_s, "sin_s": sin_s,
        "lg": jnp.asarray(np.broadcast_to(log_gamma[:, None], (H_B, RET_ROWS)), F32),
        "slopes": jnp.asarray(2.0 ** (-8.0 * heads / (N_GROUPS_C * G_C)), F32),
    }

    assert w_in.shape[2] == N_MAIN + GATE_RANK
    w = {"w_in_t": jnp.swapaxes(w_in, 1, 2), "w_merge": w_merge, "w_up_a": w_up_a, "w_up_b": w_up_b, "w_up_c": w_up_c,
         "w_out": w_out, "w_ffn_gate": w_ffn_gate, "w_ffn_up": w_ffn_up, "w_ffn_down": w_ffn_down.astype(BF16)}
    caches = (cache_win0, cache_win1, cache_win2)
    outs = {k: [] for k in ("gla_p", "gla_s", "ret_p", "ret_s")}
    win_p = [[] for _ in range(N_GROUPS_C)]
    win_s = [[] for _ in range(N_GROUPS_C)]
    for l in range(depth):
        lw = {
            "g_mix": g_mix[l], "g_gla": g_gla[l], "g_ffn": g_ffn[l], "b_alpha": b_alpha[l],
            "w_alpha2": w_alpha2[l],
        }
        x, (sa_p, sa_s, sb_p, sb_s), wp, ws = _layer(x, l, w, lw, state_gla[l], state_ret[l], caches, tabs, dims)
        outs["gla_p"].append(sa_p)
        outs["gla_s"].append(sa_s)
        outs["ret_p"].append(sb_p)
        outs["ret_s"].append(sb_s)
        for g in range(N_GROUPS_C):
            win_p[g].append(wp[g])
            win_s[g].append(ws[g])

    y_prompt = _rmsnorm(x, g_final, F32, rows=m_p).reshape(n_p, seq, d)
    y_sample = _rmsnorm(x[m_p:], g_final, F32).reshape(n_s, SEQ_PAD, d)[:, :1]
    win_s = [_shift_windows(caches[g], jnp.stack(win_s[g])) for g in range(N_GROUPS_C)]
    return (y_prompt, y_sample, jnp.stack(outs["gla_p"]), jnp.stack(outs["gla_s"]), jnp.stack(outs["ret_p"]),
            jnp.stack(outs["ret_s"]), jnp.stack(win_p[0]), win_s[0], jnp.stack(win_p[1]), win_s[1],
            jnp.stack(win_p[2]), win_s[2])
```

```python
import functools

import numpy as np
import jax
import jax.numpy as jnp
from jax import lax
from jax.experimental import pallas as pl
from jax.experimental.pallas import tpu as pltpu

F32 = jnp.float32
BF16 = jnp.bfloat16
HIGHEST = lax.Precision.HIGHEST

D_MODEL = 4096
H_A, DK_A, DV_A = 4, 256, 512
GATE_RANK, GATE_TAU = 16, 16.0
H_B, DK_B, DV_B = 8, 128, 256
ROPE_BASE = 10000.0
DILATIONS = (1, 4, 16)
N_GROUPS_C, G_C, HD_C = 3, 4, 128
BAND = 128
PAST_LEN = 16384
QK_A, V_A = H_A * DK_A, H_A * DV_A
QK_B, V_B = H_B * DK_B, H_B * DV_B
V_C = G_C * HD_C
C_COLS = N_GROUPS_C * 3 * V_C
EPS = 1e-6
NEG = -1e30

OFF_QA, OFF_KA, OFF_VA, OFF_RA = 0, QK_A, 2 * QK_A, 2 * QK_A + V_A
OFF_B = 2 * QK_A + 2 * V_A
OFF_QB, OFF_KB, OFF_VB, OFF_GB = OFF_B, OFF_B + QK_B, OFF_B + 2 * QK_B, OFF_B + 2 * QK_B + V_B
OFF_C = OFF_B + 2 * QK_B + 2 * V_B
N_MAIN = OFF_C + C_COLS

LANES = 128
SEQ_PAD = 16
GLA_ROWS = 128
RET_ROWS = 256
RET_HEADS = 8
SUBLANES = 8
VMEM_CAP = 56 * 1024 * 1024


def _vmem(nbytes):
    return int(min(VMEM_CAP, nbytes + (6 << 20)))


def _params(sem, nbytes):
    return pltpu.CompilerParams(dimension_semantics=sem, vmem_limit_bytes=_vmem(nbytes))


def _pick_tile(m, cap, align=16):
    best = align
    for t in range(align, cap + 1, align):
        if m % t == 0:
            best = t
    assert m % best == 0, (m, cap, align)
    return best


def _iota(shape, dim):
    return lax.broadcasted_iota(jnp.int32, shape, dim)


def _dot(a, b):
    return jnp.dot(a, b, preferred_element_type=F32)


def _dot_nt(a, b):
    return lax.dot_general(a, b, (((1,), (1,)), ((), ())), preferred_element_type=F32)


def _dot_tn(a, b):
    return lax.dot_general(a, b, (((0,), (0,)), ((), ())), preferred_element_type=F32)


def _sigmoid(x):
    return 1.0 / (1.0 + jnp.exp(-x))


def _rmsnorm_body(x_ref, g_ref, o_ref):
    x = x_ref[...]
    ms = jnp.mean(x * x, axis=-1, keepdims=True)
    o_ref[...] = (x * lax.rsqrt(ms + EPS) * g_ref[...]).astype(o_ref.dtype)


def _rmsnorm(x, g, out_dtype, rows=None):
    m, d = x.shape
    m = m if rows is None else rows
    tm = _pick_tile(m, 512)
    nbytes = 2 * tm * d * (4 + jnp.dtype(out_dtype).itemsize) + 2 * tm * d * 4
    return pl.pallas_call(
        _rmsnorm_body,
        out_shape=jax.ShapeDtypeStruct((m, d), out_dtype),
        grid=(m // tm,),
        in_specs=[pl.BlockSpec((tm, d), lambda i: (i, 0)), pl.BlockSpec((1, d), lambda i: (0, 0))],
        out_specs=pl.BlockSpec((tm, d), lambda i: (i, 0)),
        compiler_params=_params(("parallel",), nbytes),
        name="rmsnorm",
    )(x, g.reshape(1, d))


def _mm_ws_body(*refs, act, has_res, transposed):
    x_ref, w_ref = refs[:2]
    r_ref = refs[2] if has_res else None
    o_ref = refs[-1]
    if transposed:
        y = _dot_nt(x_ref[...], w_ref[0].astype(BF16))
    else:
        y = _dot(x_ref[...], w_ref[...].astype(BF16))
    if act is not None:
        y = act(y)
    if has_res:
        y = r_ref[...] + y
    o_ref[...] = y.astype(o_ref.dtype)


def _matmul_ws(x, w, layer, *, tm, tn, n=None, row_of=None, out_dtype=F32, act=None, residual=None, name="matmul"):
    m, k = x.shape
    transposed = row_of is not None
    n = w.shape[2] if n is None else n
    assert m % tm == 0 and n % tn == 0
    if transposed:
        w_spec = pl.BlockSpec((pl.Element(1), pl.Element(tn), pl.Element(k)),
                              lambda j, i: (layer, pl.multiple_of(row_of(j), SUBLANES), 0))
    else:
        w_spec = pl.BlockSpec((None, k, tn), lambda j, i: (layer, 0, j))
    in_specs = [pl.BlockSpec((tm, k), lambda j, i: (i, 0)), w_spec]
    args = [x, w]
    osz = jnp.dtype(out_dtype).itemsize
    nbytes = 2 * (tm * k * 2 + k * tn * 4 + tm * tn * osz) + k * tn * 2 + 2 * tm * tn * 4
    if residual is not None:
        in_specs.append(pl.BlockSpec((tm, tn), lambda j, i: (i, j)))
        args.append(residual)
        nbytes += 2 * tm * tn * 4
    return pl.pallas_call(
        functools.partial(_mm_ws_body, act=act, has_res=residual is not None, transposed=transposed),
        out_shape=jax.ShapeDtypeStruct((m, n), out_dtype),
        grid=(n // tn, m // tm),
        in_specs=in_specs,
        out_specs=pl.BlockSpec((tm, tn), lambda j, i: (i, j)),
        compiler_params=_params(("parallel", "parallel"), nbytes),
        name=name,
    )(*args)


def _in_proj(x, w_in_t, layer, *, tm, tn):
    assert OFF_B % tn == 0
    return _matmul_ws(x, w_in_t, layer, tm=tm, tn=tn, n=N_MAIN, name="in_proj",
                      row_of=lambda j: j * tn + jnp.where(j >= OFF_B // tn, GATE_RANK, 0))


def _mm_ksplit_res_body(x_ref, w_ref, r_ref, o_ref, acc_ref, *, tm, n_k):
    kk = pl.program_id(1)
    rows = pl.ds(pl.multiple_of(pl.program_id(2) * tm, tm), tm)
    part = _dot(x_ref[...], w_ref[...])

    @pl.when(kk == 0)
    def _():
        acc_ref[rows, :] = r_ref[...] + part

    @pl.when(jnp.logical_and(kk > 0, kk < n_k - 1))
    def _():
        acc_ref[rows, :] += part

    @pl.when(kk == n_k - 1)
    def _():
        o_ref[...] = acc_ref[rows, :] + part


def _matmul_ksplit_res(x, w, layer, residual, *, tm, tn, tk, name):
    m, k = x.shape
    n = w.shape[2]
    assert m % tm == 0 and n % tn == 0 and k % tk == 0 and k // tk >= 2
    n_k, n_i = k // tk, m // tm
    nbytes = 2 * (tm * tk * 2 + tk * tn * 2 + 2 * tm * tn * 4) + m * tn * 4 + 2 * tm * tn * 4
    return pl.pallas_call(
        functools.partial(_mm_ksplit_res_body, tm=tm, n_k=n_k),
        out_shape=jax.ShapeDtypeStruct((m, n), F32),
        grid=(n // tn, n_k, n_i),
        in_specs=[
            pl.BlockSpec((tm, tk), lambda j, kk, i: (i, kk)),
            pl.BlockSpec((None, tk, tn), lambda j, kk, i: (layer, kk, j)),
            pl.BlockSpec((tm, tn), lambda j, kk, i: (jnp.where(kk == 0, i, n_i - 1), j)),
        ],
        out_specs=pl.BlockSpec((tm, tn), lambda j, kk, i: (jnp.where(kk == n_k - 1, i, 0), j)),
        scratch_shapes=[pltpu.VMEM((m, tn), F32)],
        compiler_params=_params(("parallel", "arbitrary", "arbitrary"), nbytes),
        name=name,
    )(x, w, residual)


def _merge_up_body(oa_ref, ob_ref, oc_ref, ga_ref, gb_ref, gc_ref, wa_ref, wb_ref, wc_ref, o_ref):
    acc = ga_ref[...].astype(F32) * _dot(oa_ref[...], wa_ref[...].astype(BF16))
    acc += gb_ref[...].astype(F32) * _dot(ob_ref[...], wb_ref[...].astype(BF16))
    acc += gc_ref[...].astype(F32) * _dot(oc_ref[...], wc_ref[...].astype(BF16))
    o_ref[...] = acc.astype(o_ref.dtype)


def _merge_up(oa, ob, oc, gates, w_up_a, w_up_b, w_up_c, layer, *, tm, tn):
    m = oa.shape[0]
    d = w_up_a.shape[2]
    nt = d // tn
    assert m % tm == 0 and d % tn == 0

    def row(width):
        return pl.BlockSpec((tm, width), lambda j, i: (i, 0))

    def gate(branch):
        return pl.BlockSpec((tm, tn), lambda j, i: (i, branch * nt + j))

    def panel(kdim):
        return pl.BlockSpec((None, kdim, tn), lambda j, i: (layer, 0, j))

    kk = V_A + V_B + V_C
    nbytes = 2 * (tm * kk * 2 + 3 * tm * tn * 2 + kk * tn * 4 + tm * tn * 2) + kk * tn * 2 + 6 * tm * tn * 4
    return pl.pallas_call(
        _merge_up_body,
        out_shape=jax.ShapeDtypeStruct((m, d), BF16),
        grid=(nt, m // tm),
        in_specs=[row(V_A), row(V_B), row(V_C), gate(0), gate(1), gate(2), panel(V_A), panel(V_B), panel(V_C)],
        out_specs=pl.BlockSpec((tm, tn), lambda j, i: (i, j)),
        compiler_params=_params(("parallel", "parallel"), nbytes),
        name="merge_up",
    )(oa, ob, oc, gates, gates, gates, w_up_a, w_up_b, w_up_c)


def _ffn_up_body(x_ref, wg_ref, wu_ref, o_ref):
    wg = wg_ref[...].astype(BF16)
    wu = wu_ref[...].astype(BF16)
    half = x_ref.shape[0] // 2
    for rows in (slice(0, half), slice(half, 2 * half)):
        x = x_ref[rows, :]
        a = _dot(x, wg)
        o_ref[rows, :] = (a * _sigmoid(a) * _dot(x, wu)).astype(o_ref.dtype)


def _ffn_up(x, wg, wu, layer, *, tm, tn):
    m, k = x.shape
    n = wg.shape[2]
    assert m % tm == 0 and n % tn == 0
    nbytes = 2 * (tm * k * 2 + 2 * k * tn * 4 + tm * tn * 2) + 2 * k * tn * 2 + 3 * tm * tn * 4
    return pl.pallas_call(
        _ffn_up_body,
        out_shape=jax.ShapeDtypeStruct((m, n), BF16),
        grid=(m // tm, n // tn),
        in_specs=[pl.BlockSpec((tm, k), lambda i, j: (i, 0)), pl.BlockSpec((None, k, tn), lambda i, j: (layer, 0, j)),
                  pl.BlockSpec((None, k, tn), lambda i, j: (layer, 0, j))],
        out_specs=pl.BlockSpec((tm, tn), lambda i, j: (i, j)),
        compiler_params=_params(("parallel", "parallel"), nbytes),
        name="ffn_up",
    )(x, wg, wu)


def _gla_body(*refs, rows, n_t, t_valid, heads, has_bsum, has_s0, has_alias):
    it = iter(refs)
    q_ref, k_ref, v_ref, r_ref, la_ref, g_ref = [next(it) for _ in range(6)]
    bsum_ref = next(it) if has_bsum else None
    s0_ref = next(it) if has_s0 else None
    if has_alias:
        next(it)
    o_ref, sout_ref, st_scr, b_scr = [next(it) for _ in range(4)]
    t = pl.program_id(1)

    @pl.when(t == 0)
    def _():
        for hh in range(heads):
            st_scr[hh] = s0_ref[hh].T if has_s0 else jnp.zeros(st_scr.shape[1:], F32)

    for hh in range(heads):
        kk = slice(hh * DK_A, (hh + 1) * DK_A)
        vv = slice(hh * DV_A, (hh + 1) * DV_A)
        _gla_head(q_ref.at[:, kk], k_ref.at[:, kk], v_ref.at[:, vv], r_ref.at[:, vv], la_ref.at[:, kk], g_ref,
                  None if bsum_ref is None else bsum_ref.at[:, kk], o_ref.at[:, vv], st_scr.at[hh], b_scr.at[hh],
                  rows=rows, t_valid=t_valid)

    @pl.when(t == n_t - 1)
    def _():
        for hh in range(heads):
            sout_ref[hh] = st_scr[hh].T


def _gla_head(q_ref, k_ref, v_ref, r_ref, la_ref, g_ref, bsum_ref, o_ref, st_scr, b_scr, *, rows, t_valid):
    la = la_ref[...]
    row_k = _iota((rows, DK_A), 0)
    if t_valid is not None:
        la = jnp.where(row_k < t_valid, la, 0.0)
    if bsum_ref is not None:
        b = bsum_ref[...]
    else:
        tri = (_iota((rows, rows), 0) >= _iota((rows, rows), 1)).astype(F32)
        b = jnp.dot(tri, la, precision=HIGHEST, preferred_element_type=F32)
    b_scr[...] = b

    q = q_ref[...] * (DK_A ** -0.5)
    k = k_ref[...]
    v16 = v_ref[...].astype(BF16)
    st = st_scr[...]
    o = _dot_nt((q * jnp.exp(b)).astype(BF16), st.astype(BF16))

    t_idx = _iota((rows, rows), 0)
    s_idx = _iota((rows, rows), 1)
    scores = jnp.where(t_idx == s_idx, _dot_nt(q.astype(BF16), k.astype(BF16)), 0.0)
    m = rows // 2
    while m >= 1:
        sub = row_k & (2 * m - 1)
        if m >= 4:
            pieces = [jnp.broadcast_to(b_scr[pl.ds(p * 2 * m + m - 1, 1), :], (2 * m, DK_A)) for p in range(rows // (2 * m))]
            log_fac = -jnp.abs(b - (pieces[0] if len(pieces) == 1 else jnp.concatenate(pieces, axis=0)))
        elif m == 2:
            la_next = pltpu.roll(la, rows - 1, 0)
            la_prev = pltpu.roll(la, 1, 0)
            log_fac = jnp.where(sub == 0, la_next, jnp.where(sub == 1, 0.0, jnp.where(sub == 2, la, la + la_prev)))
        else:
            log_fac = jnp.where(sub == 1, la, 0.0)
        fac = jnp.exp(log_fac)
        upper = sub >= m
        qs = jnp.where(upper, q * fac, 0.0)
        ks = jnp.where(upper, 0.0, k * fac)
        part = _dot_nt(qs.astype(BF16), ks.astype(BF16))
        if 2 * m < rows:
            shift = (2 * m).bit_length() - 1
            part = jnp.where((t_idx >> shift) == (s_idx >> shift), part, 0.0)
        scores = scores + part
        m //= 2
    o = o + _dot(scores.astype(BF16), v16)

    b_last = b_scr[pl.ds(rows - 1, 1), :]
    k_st = k if t_valid is None else jnp.where(row_k < t_valid, k, 0.0)
    st_new = st * jnp.exp(b_last) + _dot_tn(v16, (k_st * jnp.exp(b_last - b)).astype(BF16))
    st_scr[...] = st_new

    ms = jnp.mean(o * o, axis=-1, keepdims=True)
    r = r_ref[...]
    o_ref[...] = (r * _sigmoid(r) * (o * lax.rsqrt(ms + EPS) * g_ref[...])).astype(o_ref.dtype)


def _gla(h, la, bsum, g_gla, *, n_seq, seq_rows, rows, row0, t_valid, state0, out_rows, alias, heads=H_A):
    n_t = seq_rows // rows
    rb0 = row0 // rows
    groups = H_A // heads
    wk, wv = heads * DK_A, heads * DV_A

    def rmap(blk0):
        return lambda s, t: (rb0 + (s // groups) * n_t + t, blk0 + s % groups)

    smap = lambda s, t: (s // groups, s % groups, 0, 0)
    in_specs = [
        pl.BlockSpec((rows, wk), rmap(OFF_QA // wk)),
        pl.BlockSpec((rows, wk), rmap(OFF_KA // wk)),
        pl.BlockSpec((rows, wv), rmap(OFF_VA // wv)),
        pl.BlockSpec((rows, wv), rmap(OFF_RA // wv)),
        pl.BlockSpec((rows, wk), rmap(0)),
        pl.BlockSpec((1, DV_A), lambda s, t: (0, 0)),
    ]
    args = [h, h, h, h, la, g_gla.reshape(1, DV_A)]
    if bsum is not None:
        in_specs.append(pl.BlockSpec((rows, wk), rmap(0)))
        args.append(bsum)
    if state0 is not None:
        in_specs.append(pl.BlockSpec((None, heads, DK_A, DV_A), smap))
        args.append(state0)
    aliases = {}
    if alias is not None:
        in_specs.append(pl.BlockSpec(memory_space=pl.ANY))
        aliases = {len(args): 0}
        args.append(alias)
    nbytes = heads * (2 * rows * (4 * DK_A + 2 * DV_A) * 4 + 2 * rows * DV_A * 2 + 5 * DK_A * DV_A * 4
                      + rows * DK_A * 4 + 16 * rows * DV_A * 4)
    return pl.pallas_call(
        functools.partial(_gla_body, rows=rows, n_t=n_t, t_valid=t_valid, heads=heads, has_bsum=bsum is not None,
                          has_s0=state0 is not None, has_alias=alias is not None),
        out_shape=(jax.ShapeDtypeStruct((out_rows, V_A), BF16), jax.ShapeDtypeStruct((n_seq, H_A, DK_A, DV_A), F32)),
        grid=(n_seq * groups, n_t),
        in_specs=in_specs,
        out_specs=(pl.BlockSpec((rows, wv), rmap(0)), pl.BlockSpec((None, heads, DK_A, DV_A), smap)),
        scratch_shapes=[pltpu.VMEM((heads, DV_A, DK_A), F32), pltpu.VMEM((heads, rows, DK_A), F32)],
        input_output_aliases=aliases,
        compiler_params=_params(("parallel", "arbitrary"), nbytes),
        name="gla",
    )(*args)


def _gate_body(x_ref, w_ref, wa_hi_ref, wa_lo_ref, ba_ref, la_ref, bsum_ref, *, block):
    za = _dot_nt(x_ref[...], w_ref[0].astype(BF16))
    za_hi = za.astype(BF16)
    za_lo = (za - za_hi.astype(F32)).astype(BF16)
    wa_hi = wa_hi_ref[...]
    z = _dot(za_hi, wa_hi) + (_dot(za_lo, wa_hi) + _dot(za_hi, wa_lo_ref[...])) + ba_ref[...]
    la = (jnp.minimum(z, 0.0) - jnp.log(1.0 + jnp.exp(-jnp.abs(z)))) * (1.0 / GATE_TAU)
    la_ref[...] = la
    tri = (_iota((block, block), 0) >= _iota((block, block), 1)).astype(BF16)
    la_hi = la.astype(BF16)
    rest = la - la_hi.astype(F32)
    la_mid = rest.astype(BF16)
    la_lo = (rest - la_mid.astype(F32)).astype(BF16)
    for p in range(x_ref.shape[0] // block):
        rows = slice(p * block, (p + 1) * block)
        bsum_ref[rows, :] = _dot(tri, la_hi[rows]) + (_dot(tri, la_mid[rows]) + _dot(tri, la_lo[rows]))


def _gate(x, w_in_t, layer, w_alpha2, b_alpha, *, tm, block):
    m, k = x.shape
    assert m % tm == 0 and tm % block == 0 and OFF_B % SUBLANES == 0
    wa = jnp.pad(w_alpha2, ((0, LANES - GATE_RANK), (0, 0)))
    wa_hi = wa.astype(BF16)
    wa_lo = (wa - wa_hi.astype(F32)).astype(BF16)
    full = lambda shape: pl.BlockSpec(shape, lambda i: (0, 0))
    nbytes = 2 * (tm * k * 2 + k * LANES * 4 + 2 * tm * QK_A * 4) + k * LANES * 2 + 8 * tm * QK_A * 4
    return pl.pallas_call(
        functools.partial(_gate_body, block=block),
        out_shape=(jax.ShapeDtypeStruct((m, QK_A), F32), jax.ShapeDtypeStruct((m, QK_A), F32)),
        grid=(m // tm,),
        in_specs=[pl.BlockSpec((tm, k), lambda i: (i, 0)),
                  pl.BlockSpec((pl.Element(1), pl.Element(LANES), pl.Element(k)), lambda i: (layer, OFF_B, 0)),
                  full((LANES, QK_A)), full((LANES, QK_A)), full((1, QK_A))],
        out_specs=(pl.BlockSpec((tm, QK_A), lambda i: (i, 0)), pl.BlockSpec((tm, QK_A), lambda i: (i, 0))),
        compiler_params=_params(("parallel",), nbytes),
        name="gla_gate",
    )(x, w_in_t, wa_hi, wa_lo, b_alpha.reshape(1, QK_A))


def _ret_body(*refs, rows, n_t, t_valid, heads, has_s0, has_alias):
    it = iter(refs)
    q_ref, k_ref, v_ref, g_ref, cos_ref, sin_ref, lg_ref = [next(it) for _ in range(7)]
    s0_ref = next(it) if has_s0 else None
    if has_alias:
        next(it)
    o_ref, sout_ref, st_scr, rel_scr, dq_scr, dk_scr = [next(it) for _ in range(6)]
    t = pl.program_id(1)
    head0 = (pl.program_id(0) % (H_B // heads)) * heads
    n_valid = rows if t_valid is None else t_valid
    row_k = _iota((rows, DK_B), 0)

    @pl.when(t == 0)
    def _():
        steps = jnp.minimum(row_k + 1, n_valid).astype(F32)
        st = jnp.minimum(_iota((rows, rows), 0) + 1, n_valid)
        ss = jnp.minimum(_iota((rows, rows), 1) + 1, n_valid)
        causal = _iota((rows, rows), 0) >= _iota((rows, rows), 1)
        for hh in range(heads):
            st_scr[hh] = s0_ref[hh] if has_s0 else jnp.zeros(st_scr.shape[1:], F32)
            lg = lg_ref[pl.ds(head0 + hh, 1), :]
            dq_scr[hh] = jnp.exp(lg[:, :DK_B] * steps)
            dk_scr[hh] = jnp.exp(lg[:, :DK_B] * (n_valid - steps))
            rel_scr[hh] = jnp.exp(jnp.where(causal, lg[:, :rows] * (st - ss).astype(F32), NEG))

    for hh in range(heads):
        kk = slice(hh * DK_B, (hh + 1) * DK_B)
        vv = slice(hh * DV_B, (hh + 1) * DV_B)
        _ret_head(q_ref.at[:, kk], k_ref.at[:, kk], v_ref.at[:, vv], g_ref.at[:, vv], cos_ref, sin_ref,
                  lg_ref.at[pl.ds(head0 + hh, 1), :], o_ref.at[:, vv], st_scr.at[hh], rel_scr.at[hh], dq_scr.at[hh],
                  dk_scr.at[hh], rows=rows, t_valid=t_valid)

    @pl.when(t == n_t - 1)
    def _():
        for hh in range(heads):
            sout_ref[hh] = st_scr[hh]


def _ret_head(q_ref, k_ref, v_ref, g_ref, cos_ref, sin_ref, lg_ref, o_ref, st_scr, rel_scr, dq_scr, dk_scr, *,
              rows, t_valid):
    lg = lg_ref[...]
    n_valid = rows if t_valid is None else t_valid
    row_k = _iota((rows, DK_B), 0)
    cos = cos_ref[...]
    sin = sin_ref[...]

    def rot(x):
        return x * cos + pltpu.roll(x, DK_B // 2, 1) * sin

    q = rot(q_ref[...])
    k = rot(k_ref[...]) * (DK_B ** -0.5)
    if t_valid is not None:
        k = jnp.where(row_k < t_valid, k, 0.0)
    decay_q = dq_scr[...]
    decay_k = dk_scr[...]

    v16 = v_ref[...].astype(BF16)
    state = st_scr[...]
    scores = _dot_nt(q.astype(BF16), k.astype(BF16)) * rel_scr[...]
    o = _dot(scores.astype(BF16), v16) + _dot((q * decay_q).astype(BF16), state.astype(BF16))
    state_new = state * jnp.exp(lg[:, :DV_B] * float(n_valid)) + _dot_tn((k * decay_k).astype(BF16), v16)
    st_scr[...] = state_new

    ms = jnp.mean(o * o, axis=-1, keepdims=True)
    g = g_ref[...]
    o_ref[...] = (g * _sigmoid(g) * (o * lax.rsqrt(ms + EPS))).astype(o_ref.dtype)


def _ret(h, cos, sin, lg_tab, *, n_seq, seq_rows, rows, row0, t_valid, state0, out_rows, alias, heads=RET_HEADS):
    n_t = seq_rows // rows
    rb0 = row0 // rows
    groups = H_B // heads
    wk, wv = heads * DK_B, heads * DV_B

    def rmap(blk0):
        return lambda s, t: (rb0 + (s // groups) * n_t + t, blk0 + s % groups)

    smap = lambda s, t: (s // groups, s % groups, 0, 0)
    in_specs = [
        pl.BlockSpec((rows, wk), rmap(OFF_QB // wk)),
        pl.BlockSpec((rows, wk), rmap(OFF_KB // wk)),
        pl.BlockSpec((rows, wv), rmap(OFF_VB // wv)),
        pl.BlockSpec((rows, wv), rmap(OFF_GB // wv)),
        pl.BlockSpec((rows, DK_B), lambda s, t: (t, 0)),
        pl.BlockSpec((rows, DK_B), lambda s, t: (t, 0)),
        pl.BlockSpec((H_B, RET_ROWS), lambda s, t: (0, 0)),
    ]
    args = [h, h, h, h, cos, sin, lg_tab]
    if state0 is not None:
        in_specs.append(pl.BlockSpec((None, heads, DK_B, DV_B), smap))
        args.append(state0)
    aliases = {}
    if alias is not None:
        in_specs.append(pl.BlockSpec(memory_space=pl.ANY))
        aliases = {len(args): 0}
        args.append(alias)
    nbytes = heads * (2 * rows * (2 * DK_B + 2 * DV_B) * 4 + 2 * rows * DV_B * 2 + 5 * DK_B * DV_B * 4
                      + 6 * rows * rows * 4 + 8 * rows * DV_B * 4) + 4 * rows * DK_B * 4
    return pl.pallas_call(
        functools.partial(_ret_body, rows=rows, n_t=n_t, t_valid=t_valid, heads=heads, has_s0=state0 is not None,
                          has_alias=alias is not None),
        out_shape=(jax.ShapeDtypeStruct((out_rows, V_B), BF16), jax.ShapeDtypeStruct((n_seq, H_B, DK_B, DV_B), F32)),
        grid=(n_seq * groups, n_t),
        in_specs=in_specs,
        out_specs=(pl.BlockSpec((rows, wv), rmap(0)), pl.BlockSpec((None, heads, DK_B, DV_B), smap)),
        scratch_shapes=[pltpu.VMEM((heads, DK_B, DV_B), F32), pltpu.VMEM((heads, rows, rows), F32),
                        pltpu.VMEM((heads, rows, DK_B), F32), pltpu.VMEM((heads, rows, DK_B), F32)],
        input_output_aliases=aliases,
        compiler_params=_params(("parallel", "arbitrary"), nbytes),
        name="retention",
    )(*args)


def _dil_prefill_body(slopes_ref, *refs, seq):
    qkv = refs[:9]
    o_ref, og_scr, lse_scr = refs[9:]
    slot = pl.program_id(1)
    scale = HD_C ** -0.5
    rr = _iota((BAND, BAND), 0)
    cc = _iota((BAND, BAND), 1)
    dist_cur = (rr - cc).astype(F32)
    ok_cur = rr >= cc
    ok_prev = cc >= rr
    n_units = seq // BAND
    for g, d in enumerate(DILATIONS):
        q_ref, k_ref, v_ref = qkv[3 * g:3 * g + 3]
        pen = slopes_ref[g * G_C + slot] * float(d)
        shift = d.bit_length() - 1

        def rows_of(start, d=d):
            return pl.ds(start, BAND) if d == 1 else pl.ds(start, BAND, stride=d)

        has_prev = seq // (BAND * d) > 1

        def unit(u, carry, q_ref=q_ref, k_ref=k_ref, v_ref=v_ref, pen=pen, d=d, shift=shift, rows_of=rows_of, g=g,
                 has_prev=has_prev):
            blk = u >> shift
            start = (u & (d - 1)) + blk * (BAND * d)
            q = q_ref[rows_of(start), :].astype(BF16)
            keys = k_ref[rows_of(start), :].astype(BF16)
            vals = v_ref[rows_of(start), :].astype(BF16)
            if has_prev:
                prev = jnp.maximum(start - BAND * d, 0)
                keys = jnp.concatenate([k_ref[rows_of(prev), :].astype(BF16), keys], axis=0)
                vals = jnp.concatenate([v_ref[rows_of(prev), :].astype(BF16), vals], axis=0)
                bias = jnp.concatenate([jnp.where(ok_prev, -pen * (dist_cur + float(BAND)), NEG)
                                        + jnp.where(blk > 0, 0.0, NEG),
                                        jnp.where(ok_cur, -pen * dist_cur, NEG)], axis=1)
            else:
                bias = jnp.where(ok_cur, -pen * dist_cur, NEG)
            s = _dot_nt(q, keys) * scale + bias
            m = jnp.max(s, axis=-1, keepdims=True)
            p = jnp.exp(s - m)
            l = jnp.sum(p, axis=-1, keepdims=True)
            acc = _dot(p.astype(BF16), vals)
            og_scr[g, rows_of(start), :] = acc / l
            lse_scr[g, rows_of(start), :] = jnp.broadcast_to(m + jnp.log(l), (BAND, HD_C))
            return carry

        lax.fori_loop(0, n_units, unit, 0, unroll=True)

    lse = [lse_scr[g] for g in range(N_GROUPS_C)]
    top = jnp.maximum(jnp.maximum(lse[0], lse[1]), lse[2])
    w = [jnp.exp(x - top) for x in lse]
    total = w[0] + w[1] + w[2]
    o_ref[...] = ((w[0] * og_scr[0] + w[1] * og_scr[1] + w[2] * og_scr[2]) / total).astype(o_ref.dtype)


def _dil_prefill(h, slopes, *, n_seq, seq, out_rows):
    def cmap(g, j):
        return lambda b, s: (b, (OFF_C + g * 3 * V_C + j * V_C) // HD_C + s)

    in_specs = [pl.BlockSpec(memory_space=pltpu.SMEM)]
    in_specs += [pl.BlockSpec((seq, HD_C), cmap(g, j)) for g in range(N_GROUPS_C) for j in range(3)]
    nbytes = 2 * 9 * seq * HD_C * 4 + 2 * seq * HD_C * 2 + 2 * N_GROUPS_C * seq * HD_C * 4 + 8 * seq * HD_C * 4
    return pl.pallas_call(
        functools.partial(_dil_prefill_body, seq=seq),
        out_shape=jax.ShapeDtypeStruct((out_rows, V_C), BF16),
        grid=(n_seq, G_C),
        in_specs=in_specs,
        out_specs=pl.BlockSpec((seq, HD_C), lambda b, s: (b, s)),
        scratch_shapes=[pltpu.VMEM((N_GROUPS_C, seq, HD_C), F32), pltpu.VMEM((N_GROUPS_C, seq, HD_C), F32)],
        compiler_params=_params(("parallel", "parallel"), nbytes),
        name="dilated_prefill",
    )(slopes, *([h] * 9))


def _dil_step_body(slopes_ref, *refs):
    new = refs[:9]
    caches = refs[9:12]
    o_ref = refs[13]
    scale = HD_C ** -0.5
    col = _iota((SEQ_PAD, BAND * G_C), 1)
    col_slot = col & (G_C - 1)
    steps = (BAND - (col >> (G_C.bit_length() - 1))).astype(F32)
    keys = [c[0].reshape(BAND * G_C, HD_C).astype(BF16) for c in caches]
    vals = [c[1].reshape(BAND * G_C, HD_C).astype(BF16) for c in caches]
    for slot in range(G_C):
        lanes = slice(slot * HD_C, (slot + 1) * HD_C)
        scores, self_scores = [], []
        for g, d in enumerate(DILATIONS):
            q_ref, k_ref, _ = new[3 * g:3 * g + 3]
            q = q_ref[:, lanes]
            s = _dot_nt(q.astype(BF16), keys[g]) * scale
            pen = slopes_ref[g * G_C + slot] * float(d)
            scores.append(jnp.where(col_slot == slot, s - pen * steps, NEG))
            self_scores.append(jnp.sum(q * k_ref[:, lanes], axis=-1, keepdims=True) * scale)
        m = self_scores[0]
        for x in self_scores[1:]:
            m = jnp.maximum(m, x)
        for x in scores:
            m = jnp.maximum(m, jnp.max(x, axis=-1, keepdims=True))
        acc = jnp.zeros((SEQ_PAD, HD_C), F32)
        total = jnp.zeros((SEQ_PAD, 1), F32)
        for g in range(N_GROUPS_C):
            p = jnp.exp(scores[g] - m)
            p_self = jnp.exp(self_scores[g] - m)
            total += jnp.sum(p, axis=-1, keepdims=True) + p_self
            acc += _dot(p.astype(BF16), vals[g]) + p_self * new[3 * g + 2][:, lanes]
        o_ref[:, lanes] = (acc / total).astype(o_ref.dtype)


def _dil_step(h, slopes, caches, layer, oc, *, n_seq, row0):
    rb0 = row0 // SEQ_PAD

    def cmap(g, j):
        return lambda b: (rb0 + b, (OFF_C + g * 3 * V_C + j * V_C) // V_C)

    in_specs = [pl.BlockSpec(memory_space=pltpu.SMEM)]
    in_specs += [pl.BlockSpec((SEQ_PAD, V_C), cmap(g, j)) for g in range(N_GROUPS_C) for j in range(3)]
    strided = []
    for c, d in zip(caches, DILATIONS):
        assert c.shape[3] == BAND * d
        strided.append(c.reshape(c.shape[0], c.shape[1], 2, BAND, d, G_C, HD_C))
        in_specs.append(pl.BlockSpec((None, None, 2, BAND, None, G_C, HD_C), lambda b: (layer, b, 0, 0, 0, 0, 0)))
    in_specs.append(pl.BlockSpec(memory_space=pl.ANY))
    nbytes = 2 * N_GROUPS_C * 2 * BAND * V_C * 4 + 64 * SEQ_PAD * BAND * G_C * 4
    return pl.pallas_call(
        _dil_step_body,
        out_shape=jax.ShapeDtypeStruct(oc.shape, oc.dtype),
        grid=(n_seq,),
        in_specs=in_specs,
        out_specs=pl.BlockSpec((SEQ_PAD, V_C), lambda b: (rb0 + b, 0)),
        input_output_aliases={13: 0},
        compiler_params=_params(("parallel",), nbytes),
        name="dilated_step",
    )(slopes, *([h] * 9), *strided, oc)


def _shift_body(c_ref, new_ref, o_ref):
    length = c_ref.shape[1]
    o_ref[:, 0:length - 1] = c_ref[:, 1:length]
    o_ref[:, length - 1:length] = new_ref[...]


def _shift_windows(cache, new_rows):
    depth, n, _, length = cache.shape[:4]
    blk = lambda rows: pl.BlockSpec((None, None, 2, rows, G_C, HD_C), lambda l, b: (l, b, 0, 0, 0, 0))
    return pl.pallas_call(
        _shift_body,
        out_shape=jax.ShapeDtypeStruct(cache.shape, cache.dtype),
        grid=(depth, n),
        in_specs=[blk(length), blk(1)],
        out_specs=blk(length),
        compiler_params=_params(("parallel", "parallel"), 4 * 2 * length * V_C * 4),
        name="shift_windows",
    )(cache, new_rows)


def _rope_tables(positions):
    half = DK_B // 2
    inv = ROPE_BASE ** (-np.arange(half, dtype=np.float64) / half)
    ang = np.asarray(positions, np.float64)[:, None] * inv[None, :]
    cos = np.concatenate([np.cos(ang), np.cos(ang)], axis=-1)
    sin = np.concatenate([-np.sin(ang), np.sin(ang)], axis=-1)
    return jnp.asarray(cos, F32), jnp.asarray(sin, F32)


def _layer(x, l, w, lw, state_gla, state_ret, caches, tabs, dims):
    n_p, seq, n_s, m_p, m = dims
    tm = _pick_tile(m, 704)
    tm_big = _pick_tile(m, 1100)
    xn = _rmsnorm(x, lw["g_mix"], BF16)
    h = _in_proj(xn, w["w_in_t"], l, tm=tm_big, tn=768)
    la, bsum = _gate(xn, w["w_in_t"], l, lw["w_alpha2"], lw["b_alpha"], tm=_pick_tile(m, 704, GLA_ROWS), block=GLA_ROWS)

    gla = functools.partial(_gla, h, la, g_gla=lw["g_gla"], out_rows=m)
    oa, sa_p = gla(bsum, n_seq=n_p, seq_rows=seq, rows=GLA_ROWS, row0=0, t_valid=None, state0=None, alias=None)
    oa, sa_s = gla(None, n_seq=n_s, seq_rows=SEQ_PAD, rows=SEQ_PAD, row0=m_p, t_valid=1, state0=state_gla, alias=oa)

    ret = functools.partial(_ret, h, out_rows=m)
    ob, sb_p = ret(tabs["cos_p"], tabs["sin_p"], tabs["lg"], n_seq=n_p, seq_rows=seq, rows=RET_ROWS, row0=0,
                   t_valid=None, state0=None, alias=None)
    ob, sb_s = ret(tabs["cos_s"], tabs["sin_s"], tabs["lg"], n_seq=n_s, seq_rows=SEQ_PAD, rows=SEQ_PAD, row0=m_p,
                   t_valid=1, state0=state_ret, alias=ob)

    oc = _dil_prefill(h, tabs["slopes"], n_seq=n_p, seq=seq, out_rows=m)
    oc = _dil_step(h, tabs["slopes"], caches, l, oc, n_seq=n_s, row0=m_p)

    gates = _matmul_ws(xn, w["w_merge"], l, tm=tm_big, tn=768, out_dtype=BF16, act=_sigmoid, name="merge_gates")
    merged = _merge_up(oa, ob, oc, gates, w["w_up_a"], w["w_up_b"], w["w_up_c"], l, tm=tm_big, tn=512)
    x = _matmul_ws(merged, w["w_out"], l, tm=tm_big, tn=512, residual=x, name="out_proj")
    xf = _rmsnorm(x, lw["g_ffn"], BF16)
    hid = _ffn_up(xf, w["w_ffn_gate"], w["w_ffn_up"], l, tm=_pick_tile(m, 1728), tn=256)
    d_ff = hid.shape[1]
    x = _matmul_ksplit_res(hid, w["w_ffn_down"], l, x, tm=tm, tn=512, tk=d_ff // 2, name="ffn_down")

    win_p, new_s = [], []
    for g in range(N_GROUPS_C):
        off_k = OFF_C + g * 3 * V_C + V_C
        keep = min(BAND * DILATIONS[g], seq)
        kv_p = jnp.stack([lax.slice(h, ((b + 1) * seq - keep, off_k), ((b + 1) * seq, off_k + 2 * V_C))
                          for b in range(n_p)])
        win_p.append(kv_p.reshape(n_p, keep, 2, G_C, HD_C).transpose(0, 2, 1, 3, 4))
        kv_s = lax.slice(h, (m_p, off_k), (m, off_k + 2 * V_C)).reshape(n_s, SEQ_PAD, 2, G_C, HD_C)[:, :1]
        new_s.append(kv_s.transpose(0, 2, 1, 3, 4))
    return x, (sa_p, sa_s, sb_p, sb_s), win_p, new_s


def kernel(x_prompt, x_sample, state_gla, state_ret, cache_win0, cache_win1, cache_win2, g_mix, w_in, w_alpha2,
           b_alpha, g_gla, w_merge, w_up_a, w_up_b, w_up_c, w_out, g_ffn, w_ffn_gate, w_ffn_up, w_ffn_down, g_final):
    n_p, seq, d = x_prompt.shape
    n_s = x_sample.shape[0]
    depth = g_mix.shape[0]
    assert d == D_MODEL and x_sample.shape[1] == 1 and seq % (BAND * DILATIONS[-1]) == 0 and seq % RET_ROWS == 0
    m_p = n_p * seq
    m = m_p + n_s * SEQ_PAD
    dims = (n_p, seq, n_s, m_p, m)

    xs = jnp.pad(x_sample, ((0, 0), (0, SEQ_PAD - 1), (0, 0)))
    x = jnp.concatenate([x_prompt.reshape(m_p, d), xs.reshape(n_s * SEQ_PAD, d)], axis=0)

    cos_p, sin_p = _rope_tables(np.arange(seq))
    cos_s, sin_s = _rope_tables(PAST_LEN + np.arange(SEQ_PAD))
    log_gamma = np.log1p(-(2.0 ** (-5.0 - np.arange(H_B, dtype=np.float64))))
    heads = np.arange(1, N_GROUPS_C * G_C + 1, dtype=np.float64)
    tabs = {
        "cos_p": cos_p, "sin_p": sin_p, "cos_s": cos_s, "sin_s": sin_s,
        "lg": jnp.asarray(np.broadcast_to(log_gamma[:, None], (H_B, RET_ROWS)), F32),
        "slopes": jnp.asarray(2.0 ** (-8.0 * heads / (N_GROUPS_C * G_C)), F32),
    }

    assert w_in.shape[2] == N_MAIN + GATE_RANK
    w = {"w_in_t": jnp.swapaxes(w_in, 1, 2), "w_merge": w_merge, "w_up_a": w_up_a, "w_up_b": w_up_b, "w_up_c": w_up_c,
         "w_out": w_out, "w_ffn_gate": w_ffn_gate, "w_ffn_up": w_ffn_up, "w_ffn_down": w_ffn_down.astype(BF16)}
    caches = (cache_win0, cache_win1, cache_win2)
    outs = {k: [] for k in ("gla_p", "gla_s", "ret_p", "ret_s")}
    win_p = [[] for _ in range(N_GROUPS_C)]
    win_s = [[] for _ in range(N_GROUPS_C)]
    for l in range(depth):
        lw = {
            "g_mix": g_mix[l], "g_gla": g_gla[l], "g_ffn": g_ffn[l], "b_alpha": b_alpha[l],
            "w_alpha2": w_alpha2[l],
        }
        x, (sa_p, sa_s, sb_p, sb_s), wp, ws = _layer(x, l, w, lw, state_gla[l], state_ret[l], caches, tabs, dims)
        outs["gla_p"].append(sa_p)
        outs["gla_s"].append(sa_s)
        outs["ret_p"].append(sb_p)
        outs["ret_s"].append(sb_s)
        for g in range(N_GROUPS_C):
            win_p[g].append(wp[g])
            win_s[g].append(ws[g])

    y_prompt = _rmsnorm(x, g_final, F32, rows=m_p).reshape(n_p, seq, d)
    y_sample = _rmsnorm(x[m_p:], g_final, F32).reshape(n_s, SEQ_PAD, d)[:, :1]
    win_s = [_shift_windows(caches[g], jnp.stack(win_s[g])) for g in range(N_GROUPS_C)]
    return (y_prompt, y_sample, jnp.stack(outs["gla_p"]), jnp.stack(outs["gla_s"]), jnp.stack(outs["ret_p"]),
            jnp.stack(outs["ret_s"]), jnp.stack(win_p[0]), win_s[0], jnp.stack(win_p[1]), win_s[1],
            jnp.stack(win_p[2]), win_s[2])
```

```python
import functools

import numpy as np
import jax
import jax.numpy as jnp
from jax import lax
from jax.experimental import pallas as pl
from jax.experimental.pallas import tpu as pltpu

F32 = jnp.float32
BF16 = jnp.bfloat16
HIGHEST = lax.Precision.HIGHEST

D_MODEL = 4096
H_A, DK_A, DV_A = 4, 256, 512
GATE_RANK, GATE_TAU = 16, 16.0
H_B, DK_B, DV_B = 8, 128, 256
ROPE_BASE = 10000.0
DILATIONS = (1, 4, 16)
N_GROUPS_C, G_C, HD_C = 3, 4, 128
BAND = 128
PAST_LEN = 16384
QK_A, V_A = H_A * DK_A, H_A * DV_A
QK_B, V_B = H_B * DK_B, H_B * DV_B
V_C = G_C * HD_C
C_COLS = N_GROUPS_C * 3 * V_C
EPS = 1e-6
NEG = -1e30

OFF_QA, OFF_KA, OFF_VA, OFF_RA = 0, QK_A, 2 * QK_A, 2 * QK_A + V_A
OFF_B = 2 * QK_A + 2 * V_A
OFF_QB, OFF_KB, OFF_VB, OFF_GB = OFF_B, OFF_B + QK_B, OFF_B + 2 * QK_B, OFF_B + 2 * QK_B + V_B
OFF_C = OFF_B + 2 * QK_B + 2 * V_B
N_MAIN = OFF_C + C_COLS

LANES = 128
SEQ_PAD = 16
GLA_ROWS = 128
RET_ROWS = 256
RET_HEADS = 8
SUBLANES = 8
ROW_CHUNKS = 5
VMEM_CAP = 56 * 1024 * 1024


def _vmem(nbytes):
    return int(min(VMEM_CAP, nbytes + (6 << 20)))


def _params(sem, nbytes):
    return pltpu.CompilerParams(dimension_semantics=sem, vmem_limit_bytes=_vmem(nbytes))


def _pick_tile(m, cap, align=16):
    best = align
    for t in range(align, cap + 1, align):
        if m % t == 0:
            best = t
    assert m % best == 0, (m, cap, align)
    return best


def _iota(shape, dim):
    return lax.broadcasted_iota(jnp.int32, shape, dim)


def _dot(a, b):
    return jnp.dot(a, b, preferred_element_type=F32)


def _dot_nt(a, b):
    return lax.dot_general(a, b, (((1,), (1,)), ((), ())), preferred_element_type=F32)


def _dot_tn(a, b):
    return lax.dot_general(a, b, (((0,), (0,)), ((), ())), preferred_element_type=F32)


def _sigmoid(x):
    return 1.0 / (1.0 + jnp.exp(-x))


def _rmsnorm_body(x_ref, g_ref, o_ref):
    x = x_ref[...]
    ms = jnp.mean(x * x, axis=-1, keepdims=True)
    o_ref[...] = (x * lax.rsqrt(ms + EPS) * g_ref[...]).astype(o_ref.dtype)


def _rmsnorm(x, g, out_dtype, rows=None):
    m, d = x.shape
    m = m if rows is None else rows
    tm = _pick_tile(m, 512)
    nbytes = 2 * tm * d * (4 + jnp.dtype(out_dtype).itemsize) + 2 * tm * d * 4
    return pl.pallas_call(
        _rmsnorm_body,
        out_shape=jax.ShapeDtypeStruct((m, d), out_dtype),
        grid=(m // tm,),
        in_specs=[pl.BlockSpec((tm, d), lambda i: (i, 0)), pl.BlockSpec((1, d), lambda i: (0, 0))],
        out_specs=pl.BlockSpec((tm, d), lambda i: (i, 0)),
        compiler_params=_params(("parallel",), nbytes),
        name="rmsnorm",
    )(x, g.reshape(1, d))


def _row_chunks(tm, want):
    if want <= 1 or tm % (want * 16) != 0:
        return [slice(0, tm)]
    return [slice(c * (tm // want), (c + 1) * (tm // want)) for c in range(want)]


def _mm_ws_body(*refs, act, has_res, transposed, chunks):
    x_ref, w_ref = refs[:2]
    r_ref = refs[2] if has_res else None
    o_ref = refs[-1]
    w = w_ref[0].astype(BF16) if transposed else w_ref[...].astype(BF16)
    for rows in _row_chunks(x_ref.shape[0], chunks):
        y = _dot_nt(x_ref[rows, :], w) if transposed else _dot(x_ref[rows, :], w)
        if act is not None:
            y = act(y)
        if has_res:
            y = r_ref[rows, :] + y
        o_ref[rows, :] = y.astype(o_ref.dtype)


def _matmul_ws(x, w, layer, *, tm, tn, n=None, row_of=None, out_dtype=F32, act=None, residual=None, chunks=1,
               name="matmul"):
    m, k = x.shape
    transposed = row_of is not None
    n = w.shape[2] if n is None else n
    assert m % tm == 0 and n % tn == 0
    if transposed:
        w_spec = pl.BlockSpec((pl.Element(1), pl.Element(tn), pl.Element(k)),
                              lambda j, i: (layer, pl.multiple_of(row_of(j), SUBLANES), 0))
    else:
        w_spec = pl.BlockSpec((None, k, tn), lambda j, i: (layer, 0, j))
    in_specs = [pl.BlockSpec((tm, k), lambda j, i: (i, 0)), w_spec]
    args = [x, w]
    osz = jnp.dtype(out_dtype).itemsize
    nbytes = 2 * (tm * k * 2 + k * tn * 4 + tm * tn * osz) + k * tn * 2 + 2 * tm * tn * 4
    if residual is not None:
        in_specs.append(pl.BlockSpec((tm, tn), lambda j, i: (i, j)))
        args.append(residual)
        nbytes += 2 * tm * tn * 4
    return pl.pallas_call(
        functools.partial(_mm_ws_body, act=act, has_res=residual is not None, transposed=transposed, chunks=chunks),
        out_shape=jax.ShapeDtypeStruct((m, n), out_dtype),
        grid=(n // tn, m // tm),
        in_specs=in_specs,
        out_specs=pl.BlockSpec((tm, tn), lambda j, i: (i, j)),
        compiler_params=_params(("parallel", "parallel"), nbytes),
        name=name,
    )(*args)


def _in_proj(x, w_in_t, layer, *, tm, tn):
    assert OFF_B % tn == 0
    return _matmul_ws(x, w_in_t, layer, tm=tm, tn=tn, n=N_MAIN, name="in_proj",
                      row_of=lambda j: j * tn + jnp.where(j >= OFF_B // tn, GATE_RANK, 0))


def _mm_ksplit_res_body(x_ref, w_ref, r_ref, o_ref, acc_ref, *, tm, n_k):
    kk = pl.program_id(1)
    rows = pl.ds(pl.multiple_of(pl.program_id(2) * tm, tm), tm)
    part = _dot(x_ref[...], w_ref[...])

    @pl.when(kk == 0)
    def _():
        acc_ref[rows, :] = r_ref[...] + part

    @pl.when(jnp.logical_and(kk > 0, kk < n_k - 1))
    def _():
        acc_ref[rows, :] += part

    @pl.when(kk == n_k - 1)
    def _():
        o_ref[...] = acc_ref[rows, :] + part


def _matmul_ksplit_res(x, w, layer, residual, *, tm, tn, tk, name):
    m, k = x.shape
    n = w.shape[2]
    assert m % tm == 0 and n % tn == 0 and k % tk == 0 and k // tk >= 2
    n_k, n_i = k // tk, m // tm
    nbytes = 2 * (tm * tk * 2 + tk * tn * 2 + 2 * tm * tn * 4) + m * tn * 4 + 2 * tm * tn * 4
    return pl.pallas_call(
        functools.partial(_mm_ksplit_res_body, tm=tm, n_k=n_k),
        out_shape=jax.ShapeDtypeStruct((m, n), F32),
        grid=(n // tn, n_k, n_i),
        in_specs=[
            pl.BlockSpec((tm, tk), lambda j, kk, i: (i, kk)),
            pl.BlockSpec((None, tk, tn), lambda j, kk, i: (layer, kk, j)),
            pl.BlockSpec((tm, tn), lambda j, kk, i: (jnp.where(kk == 0, i, n_i - 1), j)),
        ],
        out_specs=pl.BlockSpec((tm, tn), lambda j, kk, i: (jnp.where(kk == n_k - 1, i, 0), j)),
        scratch_shapes=[pltpu.VMEM((m, tn), F32)],
        compiler_params=_params(("parallel", "arbitrary", "arbitrary"), nbytes),
        name=name,
    )(x, w, residual)


def _merge_up_body(oa_ref, ob_ref, oc_ref, ga_ref, gb_ref, gc_ref, wa_ref, wb_ref, wc_ref, o_ref):
    wa, wb, wc = wa_ref[...].astype(BF16), wb_ref[...].astype(BF16), wc_ref[...].astype(BF16)
    for rows in _row_chunks(oa_ref.shape[0], ROW_CHUNKS):
        acc = ga_ref[rows, :].astype(F32) * _dot(oa_ref[rows, :], wa)
        acc += gb_ref[rows, :].astype(F32) * _dot(ob_ref[rows, :], wb)
        acc += gc_ref[rows, :].astype(F32) * _dot(oc_ref[rows, :], wc)
        o_ref[rows, :] = acc.astype(o_ref.dtype)


def _merge_up(oa, ob, oc, gates, w_up_a, w_up_b, w_up_c, layer, *, tm, tn):
    m = oa.shape[0]
    d = w_up_a.shape[2]
    nt = d // tn
    assert m % tm == 0 and d % tn == 0

    def row(width):
        return pl.BlockSpec((tm, width), lambda j, i: (i, 0))

    def gate(branch):
        return pl.BlockSpec((tm, tn), lambda j, i: (i, branch * nt + j))

    def panel(kdim):
        return pl.BlockSpec((None, kdim, tn), lambda j, i: (layer, 0, j))

    kk = V_A + V_B + V_C
    nbytes = 2 * (tm * kk * 2 + 3 * tm * tn * 2 + kk * tn * 4 + tm * tn * 2) + kk * tn * 2 + 6 * tm * tn * 4
    return pl.pallas_call(
        _merge_up_body,
        out_shape=jax.ShapeDtypeStruct((m, d), BF16),
        grid=(nt, m // tm),
        in_specs=[row(V_A), row(V_B), row(V_C), gate(0), gate(1), gate(2), panel(V_A), panel(V_B), panel(V_C)],
        out_specs=pl.BlockSpec((tm, tn), lambda j, i: (i, j)),
        compiler_params=_params(("parallel", "parallel"), nbytes),
        name="merge_up",
    )(oa, ob, oc, gates, gates, gates, w_up_a, w_up_b, w_up_c)


def _ffn_up_body(x_ref, wg_ref, wu_ref, o_ref):
    wg = wg_ref[...].astype(BF16)
    wu = wu_ref[...].astype(BF16)
    half = x_ref.shape[0] // 2
    for rows in (slice(0, half), slice(half, 2 * half)):
        x = x_ref[rows, :]
        a = _dot(x, wg)
        o_ref[rows, :] = (a * _sigmoid(a) * _dot(x, wu)).astype(o_ref.dtype)


def _ffn_up(x, wg, wu, layer, *, tm, tn):
    m, k = x.shape
    n = wg.shape[2]
    assert m % tm == 0 and n % tn == 0
    nbytes = 2 * (tm * k * 2 + 2 * k * tn * 4 + tm * tn * 2) + 2 * k * tn * 2 + 3 * tm * tn * 4
    return pl.pallas_call(
        _ffn_up_body,
        out_shape=jax.ShapeDtypeStruct((m, n), BF16),
        grid=(m // tm, n // tn),
        in_specs=[pl.BlockSpec((tm, k), lambda i, j: (i, 0)), pl.BlockSpec((None, k, tn), lambda i, j: (layer, 0, j)),
                  pl.BlockSpec((None, k, tn), lambda i, j: (layer, 0, j))],
        out_specs=pl.BlockSpec((tm, tn), lambda i, j: (i, j)),
        compiler_params=_params(("parallel", "parallel"), nbytes),
        name="ffn_up",
    )(x, wg, wu)


def _gla_body(*refs, rows, n_t, t_valid, heads, has_bsum, has_s0, has_alias):
    it = iter(refs)
    q_ref, k_ref, v_ref, r_ref, la_ref, g_ref = [next(it) for _ in range(6)]
    bsum_ref = next(it) if has_bsum else None
    s0_ref = next(it) if has_s0 else None
    if has_alias:
        next(it)
    o_ref, sout_ref, st_scr, b_scr = [next(it) for _ in range(4)]
    t = pl.program_id(1)

    @pl.when(t == 0)
    def _():
        for hh in range(heads):
            st_scr[hh] = s0_ref[hh].T if has_s0 else jnp.zeros(st_scr.shape[1:], F32)

    for hh in range(heads):
        kk = slice(hh * DK_A, (hh + 1) * DK_A)
        vv = slice(hh * DV_A, (hh + 1) * DV_A)
        _gla_head(q_ref.at[:, kk], k_ref.at[:, kk], v_ref.at[:, vv], r_ref.at[:, vv], la_ref.at[:, kk], g_ref,
                  None if bsum_ref is None else bsum_ref.at[:, kk], o_ref.at[:, vv], st_scr.at[hh], b_scr.at[hh],
                  rows=rows, t_valid=t_valid)

    @pl.when(t == n_t - 1)
    def _():
        for hh in range(heads):
            sout_ref[hh] = st_scr[hh].T


def _gla_head(q_ref, k_ref, v_ref, r_ref, la_ref, g_ref, bsum_ref, o_ref, st_scr, b_scr, *, rows, t_valid):
    la = la_ref[...]
    row_k = _iota((rows, DK_A), 0)
    if t_valid is not None:
        la = jnp.where(row_k < t_valid, la, 0.0)
    if bsum_ref is not None:
        b = bsum_ref[...]
    else:
        tri = (_iota((rows, rows), 0) >= _iota((rows, rows), 1)).astype(F32)
        b = jnp.dot(tri, la, precision=HIGHEST, preferred_element_type=F32)
    b_scr[...] = b

    q = q_ref[...] * (DK_A ** -0.5)
    k = k_ref[...]
    v16 = v_ref[...].astype(BF16)
    st = st_scr[...]
    o = _dot_nt((q * jnp.exp(b)).astype(BF16), st.astype(BF16))

    t_idx = _iota((rows, rows), 0)
    s_idx = _iota((rows, rows), 1)
    scores = jnp.where(t_idx == s_idx, _dot_nt(q.astype(BF16), k.astype(BF16)), 0.0)
    m = rows // 2
    while m >= 1:
        sub = row_k & (2 * m - 1)
        if m >= 4:
            pieces = [jnp.broadcast_to(b_scr[pl.ds(p * 2 * m + m - 1, 1), :], (2 * m, DK_A)) for p in range(rows // (2 * m))]
            log_fac = -jnp.abs(b - (pieces[0] if len(pieces) == 1 else jnp.concatenate(pieces, axis=0)))
        elif m == 2:
            la_next = pltpu.roll(la, rows - 1, 0)
            la_prev = pltpu.roll(la, 1, 0)
            log_fac = jnp.where(sub == 0, la_next, jnp.where(sub == 1, 0.0, jnp.where(sub == 2, la, la + la_prev)))
        else:
            log_fac = jnp.where(sub == 1, la, 0.0)
        fac = jnp.exp(log_fac)
        upper = sub >= m
        qs = jnp.where(upper, q * fac, 0.0)
        ks = jnp.where(upper, 0.0, k * fac)
        part = _dot_nt(qs.astype(BF16), ks.astype(BF16))
        if 2 * m < rows:
            shift = (2 * m).bit_length() - 1
            part = jnp.where((t_idx >> shift) == (s_idx >> shift), part, 0.0)
        scores = scores + part
        m //= 2
    o = o + _dot(scores.astype(BF16), v16)

    b_last = b_scr[pl.ds(rows - 1, 1), :]
    k_st = k if t_valid is None else jnp.where(row_k < t_valid, k, 0.0)
    st_new = st * jnp.exp(b_last) + _dot_tn(v16, (k_st * jnp.exp(b_last - b)).astype(BF16))
    st_scr[...] = st_new

    ms = jnp.mean(o * o, axis=-1, keepdims=True)
    r = r_ref[...]
    o_ref[...] = (r * _sigmoid(r) * (o * lax.rsqrt(ms + EPS) * g_ref[...])).astype(o_ref.dtype)


def _gla(h, la, bsum, g_gla, *, n_seq, seq_rows, rows, row0, t_valid, state0, out_rows, alias, heads=H_A):
    n_t = seq_rows // rows
    rb0 = row0 // rows
    groups = H_A // heads
    wk, wv = heads * DK_A, heads * DV_A

    def rmap(blk0):
        return lambda s, t: (rb0 + (s // groups) * n_t + t, blk0 + s % groups)

    smap = lambda s, t: (s // groups, s % groups, 0, 0)
    in_specs = [
        pl.BlockSpec((rows, wk), rmap(OFF_QA // wk)),
        pl.BlockSpec((rows, wk), rmap(OFF_KA // wk)),
        pl.BlockSpec((rows, wv), rmap(OFF_VA // wv)),
        pl.BlockSpec((rows, wv), rmap(OFF_RA // wv)),
        pl.BlockSpec((rows, wk), rmap(0)),
        pl.BlockSpec((1, DV_A), lambda s, t: (0, 0)),
    ]
    args = [h, h, h, h, la, g_gla.reshape(1, DV_A)]
    if bsum is not None:
        in_specs.append(pl.BlockSpec((rows, wk), rmap(0)))
        args.append(bsum)
    if state0 is not None:
        in_specs.append(pl.BlockSpec((None, heads, DK_A, DV_A), smap))
        args.append(state0)
    aliases = {}
    if alias is not None:
        in_specs.append(pl.BlockSpec(memory_space=pl.ANY))
        aliases = {len(args): 0}
        args.append(alias)
    nbytes = heads * (2 * rows * (4 * DK_A + 2 * DV_A) * 4 + 2 * rows * DV_A * 2 + 5 * DK_A * DV_A * 4
                      + rows * DK_A * 4 + 16 * rows * DV_A * 4)
    return pl.pallas_call(
        functools.partial(_gla_body, rows=rows, n_t=n_t, t_valid=t_valid, heads=heads, has_bsum=bsum is not None,
                          has_s0=state0 is not None, has_alias=alias is not None),
        out_shape=(jax.ShapeDtypeStruct((out_rows, V_A), BF16), jax.ShapeDtypeStruct((n_seq, H_A, DK_A, DV_A), F32)),
        grid=(n_seq * groups, n_t),
        in_specs=in_specs,
        out_specs=(pl.BlockSpec((rows, wv), rmap(0)), pl.BlockSpec((None, heads, DK_A, DV_A), smap)),
        scratch_shapes=[pltpu.VMEM((heads, DV_A, DK_A), F32), pltpu.VMEM((heads, rows, DK_A), F32)],
        input_output_aliases=aliases,
        compiler_params=_params(("parallel", "arbitrary"), nbytes),
        name="gla",
    )(*args)


def _gate_body(x_ref, w_ref, wa_hi_ref, wa_lo_ref, ba_ref, la_ref, bsum_ref, *, block):
    za = _dot_nt(x_ref[...], w_ref[0].astype(BF16))
    za_hi = za.astype(BF16)
    za_lo = (za - za_hi.astype(F32)).astype(BF16)
    wa_hi = wa_hi_ref[...]
    z = _dot(za_hi, wa_hi) + (_dot(za_lo, wa_hi) + _dot(za_hi, wa_lo_ref[...])) + ba_ref[...]
    la = (jnp.minimum(z, 0.0) - jnp.log(1.0 + jnp.exp(-jnp.abs(z)))) * (1.0 / GATE_TAU)
    la_ref[...] = la
    tri = (_iota((block, block), 0) >= _iota((block, block), 1)).astype(BF16)
    la_hi = la.astype(BF16)
    rest = la - la_hi.astype(F32)
    la_mid = rest.astype(BF16)
    la_lo = (rest - la_mid.astype(F32)).astype(BF16)
    for p in range(x_ref.shape[0] // block):
        rows = slice(p * block, (p + 1) * block)
        bsum_ref[rows, :] = _dot(tri, la_hi[rows]) + (_dot(tri, la_mid[rows]) + _dot(tri, la_lo[rows]))


def _gate(x, w_in_t, layer, w_alpha2, b_alpha, *, tm, block):
    m, k = x.shape
    assert m % tm == 0 and tm % block == 0 and OFF_B % SUBLANES == 0
    wa = jnp.pad(w_alpha2, ((0, LANES - GATE_RANK), (0, 0)))
    wa_hi = wa.astype(BF16)
    wa_lo = (wa - wa_hi.astype(F32)).astype(BF16)
    full = lambda shape: pl.BlockSpec(shape, lambda i: (0, 0))
    nbytes = 2 * (tm * k * 2 + k * LANES * 4 + 2 * tm * QK_A * 4) + k * LANES * 2 + 8 * tm * QK_A * 4
    return pl.pallas_call(
        functools.partial(_gate_body, block=block),
        out_shape=(jax.ShapeDtypeStruct((m, QK_A), F32), jax.ShapeDtypeStruct((m, QK_A), F32)),
        grid=(m // tm,),
        in_specs=[pl.BlockSpec((tm, k), lambda i: (i, 0)),
                  pl.BlockSpec((pl.Element(1), pl.Element(LANES), pl.Element(k)), lambda i: (layer, OFF_B, 0)),
                  full((LANES, QK_A)), full((LANES, QK_A)), full((1, QK_A))],
        out_specs=(pl.BlockSpec((tm, QK_A), lambda i: (i, 0)), pl.BlockSpec((tm, QK_A), lambda i: (i, 0))),
        compiler_params=_params(("parallel",), nbytes),
        name="gla_gate",
    )(x, w_in_t, wa_hi, wa_lo, b_alpha.reshape(1, QK_A))


def _ret_body(*refs, rows, n_t, t_valid, heads, has_s0, has_alias):
    it = iter(refs)
    q_ref, k_ref, v_ref, g_ref, cos_ref, sin_ref, lg_ref = [next(it) for _ in range(7)]
    s0_ref = next(it) if has_s0 else None
    if has_alias:
        next(it)
    o_ref, sout_ref, st_scr, rel_scr, dq_scr, dk_scr = [next(it) for _ in range(6)]
    t = pl.program_id(1)
    head0 = (pl.program_id(0) % (H_B // heads)) * heads
    n_valid = rows if t_valid is None else t_valid
    row_k = _iota((rows, DK_B), 0)

    @pl.when(t == 0)
    def _():
        steps = jnp.minimum(row_k + 1, n_valid).astype(F32)
        st = jnp.minimum(_iota((rows, rows), 0) + 1, n_valid)
        ss = jnp.minimum(_iota((rows, rows), 1) + 1, n_valid)
        causal = _iota((rows, rows), 0) >= _iota((rows, rows), 1)
        for hh in range(heads):
            st_scr[hh] = s0_ref[hh] if has_s0 else jnp.zeros(st_scr.shape[1:], F32)
            lg = lg_ref[pl.ds(head0 + hh, 1), :]
            dq_scr[hh] = jnp.exp(lg[:, :DK_B] * steps)
            dk_scr[hh] = jnp.exp(lg[:, :DK_B] * (n_valid - steps))
            rel_scr[hh] = jnp.exp(jnp.where(causal, lg[:, :rows] * (st - ss).astype(F32), NEG))

    for hh in range(heads):
        kk = slice(hh * DK_B, (hh + 1) * DK_B)
        vv = slice(hh * DV_B, (hh + 1) * DV_B)
        _ret_head(q_ref.at[:, kk], k_ref.at[:, kk], v_ref.at[:, vv], g_ref.at[:, vv], cos_ref, sin_ref,
                  lg_ref.at[pl.ds(head0 + hh, 1), :], o_ref.at[:, vv], st_scr.at[hh], rel_scr.at[hh], dq_scr.at[hh],
                  dk_scr.at[hh], rows=rows, t_valid=t_valid)

    @pl.when(t == n_t - 1)
    def _():
        for hh in range(heads):
            sout_ref[hh] = st_scr[hh]


def _ret_head(q_ref, k_ref, v_ref, g_ref, cos_ref, sin_ref, lg_ref, o_ref, st_scr, rel_scr, dq_scr, dk_scr, *,
              rows, t_valid):
    lg = lg_ref[...]
    n_valid = rows if t_valid is None else t_valid
    row_k = _iota((rows, DK_B), 0)
    cos = cos_ref[...]
    sin = sin_ref[...]

    def rot(x):
        return x * cos + pltpu.roll(x, DK_B // 2, 1) * sin

    q = rot(q_ref[...])
    k = rot(k_ref[...]) * (DK_B ** -0.5)
    if t_valid is not None:
        k = jnp.where(row_k < t_valid, k, 0.0)
    decay_q = dq_scr[...]
    decay_k = dk_scr[...]

    v16 = v_ref[...].astype(BF16)
    state = st_scr[...]
    scores = _dot_nt(q.astype(BF16), k.astype(BF16)) * rel_scr[...]
    o = _dot(scores.astype(BF16), v16) + _dot((q * decay_q).astype(BF16), state.astype(BF16))
    state_new = state * jnp.exp(lg[:, :DV_B] * float(n_valid)) + _dot_tn((k * decay_k).astype(BF16), v16)
    st_scr[...] = state_new

    ms = jnp.mean(o * o, axis=-1, keepdims=True)
    g = g_ref[...]
    o_ref[...] = (g * _sigmoid(g) * (o * lax.rsqrt(ms + EPS))).astype(o_ref.dtype)


def _ret(h, cos, sin, lg_tab, *, n_seq, seq_rows, rows, row0, t_valid, state0, out_rows, alias, heads=RET_HEADS):
    n_t = seq_rows // rows
    rb0 = row0 // rows
    groups = H_B // heads
    wk, wv = heads * DK_B, heads * DV_B

    def rmap(blk0):
        return lambda s, t: (rb0 + (s // groups) * n_t + t, blk0 + s % groups)

    smap = lambda s, t: (s // groups, s % groups, 0, 0)
    in_specs = [
        pl.BlockSpec((rows, wk), rmap(OFF_QB // wk)),
        pl.BlockSpec((rows, wk), rmap(OFF_KB // wk)),
        pl.BlockSpec((rows, wv), rmap(OFF_VB // wv)),
        pl.BlockSpec((rows, wv), rmap(OFF_GB // wv)),
        pl.BlockSpec((rows, DK_B), lambda s, t: (t, 0)),
        pl.BlockSpec((rows, DK_B), lambda s, t: (t, 0)),
        pl.BlockSpec((H_B, RET_ROWS), lambda s, t: (0, 0)),
    ]
    args = [h, h, h, h, cos, sin, lg_tab]
    if state0 is not None:
        in_specs.append(pl.BlockSpec((None, heads, DK_B, DV_B), smap))
        args.append(state0)
    aliases = {}
    if alias is not None:
        in_specs.append(pl.BlockSpec(memory_space=pl.ANY))
        aliases = {len(args): 0}
        args.append(alias)
    nbytes = heads * (2 * rows * (2 * DK_B + 2 * DV_B) * 4 + 2 * rows * DV_B * 2 + 5 * DK_B * DV_B * 4
                      + 6 * rows * rows * 4 + 8 * rows * DV_B * 4) + 4 * rows * DK_B * 4
    return pl.pallas_call(
        functools.partial(_ret_body, rows=rows, n_t=n_t, t_valid=t_valid, heads=heads, has_s0=state0 is not None,
                          has_alias=alias is not None),
        out_shape=(jax.ShapeDtypeStruct((out_rows, V_B), BF16), jax.ShapeDtypeStruct((n_seq, H_B, DK_B, DV_B), F32)),
        grid=(n_seq * groups, n_t),
        in_specs=in_specs,
        out_specs=(pl.BlockSpec((rows, wv), rmap(0)), pl.BlockSpec((None, heads, DK_B, DV_B), smap)),
        scratch_shapes=[pltpu.VMEM((heads, DK_B, DV_B), F32), pltpu.VMEM((heads, rows, rows), F32),
                        pltpu.VMEM((heads, rows, DK_B), F32), pltpu.VMEM((heads, rows, DK_B), F32)],
        input_output_aliases=aliases,
        compiler_params=_params(("parallel", "arbitrary"), nbytes),
        name="retention",
    )(*args)


def _dil_prefill_body(slopes_ref, *refs, seq):
    qkv = refs[:9]
    o_ref, og_scr, lse_scr = refs[9:]
    slot = pl.program_id(1)
    scale = HD_C ** -0.5
    rr = _iota((BAND, BAND), 0)
    cc = _iota((BAND, BAND), 1)
    dist_cur = (rr - cc).astype(F32)
    ok_cur = rr >= cc
    ok_prev = cc >= rr
    n_units = seq // BAND
    for g, d in enumerate(DILATIONS):
        q_ref, k_ref, v_ref = qkv[3 * g:3 * g + 3]
        pen = slopes_ref[g * G_C + slot] * float(d)
        shift = d.bit_length() - 1

        def rows_of(start, d=d):
            return pl.ds(start, BAND) if d == 1 else pl.ds(start, BAND, stride=d)

        has_prev = seq // (BAND * d) > 1

        def unit(u, carry, q_ref=q_ref, k_ref=k_ref, v_ref=v_ref, pen=pen, d=d, shift=shift, rows_of=rows_of, g=g,
                 has_prev=has_prev):
            blk = u >> shift
            start = (u & (d - 1)) + blk * (BAND * d)
            q = q_ref[rows_of(start), :].astype(BF16)
            keys = k_ref[rows_of(start), :].astype(BF16)
            vals = v_ref[rows_of(start), :].astype(BF16)
            if has_prev:
                prev = jnp.maximum(start - BAND * d, 0)
                keys = jnp.concatenate([k_ref[rows_of(prev), :].astype(BF16), keys], axis=0)
                vals = jnp.concatenate([v_ref[rows_of(prev), :].astype(BF16), vals], axis=0)
                bias = jnp.concatenate([jnp.where(ok_prev, -pen * (dist_cur + float(BAND)), NEG)
                                        + jnp.where(blk > 0, 0.0, NEG),
                                        jnp.where(ok_cur, -pen * dist_cur, NEG)], axis=1)
            else:
                bias = jnp.where(ok_cur, -pen * dist_cur, NEG)
            s = _dot_nt(q, keys) * scale + bias
            m = jnp.max(s, axis=-1, keepdims=True)
            p = jnp.exp(s - m)
            l = jnp.sum(p, axis=-1, keepdims=True)
            acc = _dot(p.astype(BF16), vals)
            og_scr[g, rows_of(start), :] = acc / l
            lse_scr[g, rows_of(start), :] = jnp.broadcast_to(m + jnp.log(l), (BAND, HD_C))
            return carry

        lax.fori_loop(0, n_units, unit, 0, unroll=True)

    lse = [lse_scr[g] for g in range(N_GROUPS_C)]
    top = jnp.maximum(jnp.maximum(lse[0], lse[1]), lse[2])
    w = [jnp.exp(x - top) for x in lse]
    total = w[0] + w[1] + w[2]
    o_ref[...] = ((w[0] * og_scr[0] + w[1] * og_scr[1] + w[2] * og_scr[2]) / total).astype(o_ref.dtype)


def _dil_prefill(h, slopes, *, n_seq, seq, out_rows):
    def cmap(g, j):
        return lambda b, s: (b, (OFF_C + g * 3 * V_C + j * V_C) // HD_C + s)

    in_specs = [pl.BlockSpec(memory_space=pltpu.SMEM)]
    in_specs += [pl.BlockSpec((seq, HD_C), cmap(g, j)) for g in range(N_GROUPS_C) for j in range(3)]
    nbytes = 2 * 9 * seq * HD_C * 4 + 2 * seq * HD_C * 2 + 2 * N_GROUPS_C * seq * HD_C * 4 + 8 * seq * HD_C * 4
    return pl.pallas_call(
        functools.partial(_dil_prefill_body, seq=seq),
        out_shape=jax.ShapeDtypeStruct((out_rows, V_C), BF16),
        grid=(n_seq, G_C),
        in_specs=in_specs,
        out_specs=pl.BlockSpec((seq, HD_C), lambda b, s: (b, s)),
        scratch_shapes=[pltpu.VMEM((N_GROUPS_C, seq, HD_C), F32), pltpu.VMEM((N_GROUPS_C, seq, HD_C), F32)],
        compiler_params=_params(("parallel", "parallel"), nbytes),
        name="dilated_prefill",
    )(slopes, *([h] * 9))


def _dil_step_body(slopes_ref, *refs):
    new = refs[:9]
    caches = refs[9:12]
    o_ref = refs[13]
    scale = HD_C ** -0.5
    col = _iota((SEQ_PAD, BAND * G_C), 1)
    col_slot = col & (G_C - 1)
    steps = (BAND - (col >> (G_C.bit_length() - 1))).astype(F32)
    keys = [c[0].reshape(BAND * G_C, HD_C).astype(BF16) for c in caches]
    vals = [c[1].reshape(BAND * G_C, HD_C).astype(BF16) for c in caches]
    for slot in range(G_C):
        lanes = slice(slot * HD_C, (slot + 1) * HD_C)
        scores, self_scores = [], []
        for g, d in enumerate(DILATIONS):
            q_ref, k_ref, _ = new[3 * g:3 * g + 3]
            q = q_ref[:, lanes]
            s = _dot_nt(q.astype(BF16), keys[g]) * scale
            pen = slopes_ref[g * G_C + slot] * float(d)
            scores.append(jnp.where(col_slot == slot, s - pen * steps, NEG))
            self_scores.append(jnp.sum(q * k_ref[:, lanes], axis=-1, keepdims=True) * scale)
        m = self_scores[0]
        for x in self_scores[1:]:
            m = jnp.maximum(m, x)
        for x in scores:
            m = jnp.maximum(m, jnp.max(x, axis=-1, keepdims=True))
        acc = jnp.zeros((SEQ_PAD, HD_C), F32)
        total = jnp.zeros((SEQ_PAD, 1), F32)
        for g in range(N_GROUPS_C):
            p = jnp.exp(scores[g] - m)
            p_self = jnp.exp(self_scores[g] - m)
            total += jnp.sum(p, axis=-1, keepdims=True) + p_self
            acc += _dot(p.astype(BF16), vals[g]) + p_self * new[3 * g + 2][:, lanes]
        o_ref[:, lanes] = (acc / total).astype(o_ref.dtype)


def _dil_step(h, slopes, caches, layer, oc, *, n_seq, row0):
    rb0 = row0 // SEQ_PAD

    def cmap(g, j):
        return lambda b: (rb0 + b, (OFF_C + g * 3 * V_C + j * V_C) // V_C)

    in_specs = [pl.BlockSpec(memory_space=pltpu.SMEM)]
    in_specs += [pl.BlockSpec((SEQ_PAD, V_C), cmap(g, j)) for g in range(N_GROUPS_C) for j in range(3)]
    strided = []
    for c, d in zip(caches, DILATIONS):
        assert c.shape[3] == BAND * d
        strided.append(c.reshape(c.shape[0], c.shape[1], 2, BAND, d, G_C, HD_C))
        in_specs.append(pl.BlockSpec((None, None, 2, BAND, None, G_C, HD_C), lambda b: (layer, b, 0, 0, 0, 0, 0)))
    in_specs.append(pl.BlockSpec(memory_space=pl.ANY))
    nbytes = 2 * N_GROUPS_C * 2 * BAND * V_C * 4 + 64 * SEQ_PAD * BAND * G_C * 4
    return pl.pallas_call(
        _dil_step_body,
        out_shape=jax.ShapeDtypeStruct(oc.shape, oc.dtype),
        grid=(n_seq,),
        in_specs=in_specs,
        out_specs=pl.BlockSpec((SEQ_PAD, V_C), lambda b: (rb0 + b, 0)),
        input_output_aliases={13: 0},
        compiler_params=_params(("parallel",), nbytes),
        name="dilated_step",
    )(slopes, *([h] * 9), *strided, oc)


def _shift_body(c_ref, new_ref, o_ref):
    length = c_ref.shape[1]
    o_ref[:, 0:length - 1] = c_ref[:, 1:length]
    o_ref[:, length - 1:length] = new_ref[...]


def _shift_windows(cache, new_rows):
    depth, n, _, length = cache.shape[:4]
    blk = lambda rows: pl.BlockSpec((None, None, 2, rows, G_C, HD_C), lambda l, b: (l, b, 0, 0, 0, 0))
    return pl.pallas_call(
        _shift_body,
        out_shape=jax.ShapeDtypeStruct(cache.shape, cache.dtype),
        grid=(depth, n),
        in_specs=[blk(length), blk(1)],
        out_specs=blk(length),
        compiler_params=_params(("parallel", "parallel"), 4 * 2 * length * V_C * 4),
        name="shift_windows",
    )(cache, new_rows)


def _rope_tables(positions):
    half = DK_B // 2
    inv = ROPE_BASE ** (-np.arange(half, dtype=np.float64) / half)
    ang = np.asarray(positions, np.float64)[:, None] * inv[None, :]
    cos = np.concatenate([np.cos(ang), np.cos(ang)], axis=-1)
    sin = np.concatenate([-np.sin(ang), np.sin(ang)], axis=-1)
    return jnp.asarray(cos, F32), jnp.asarray(sin, F32)


def _layer(x, l, w, lw, state_gla, state_ret, caches, tabs, dims):
    n_p, seq, n_s, m_p, m = dims
    tm = _pick_tile(m, 704)
    tm_big = _pick_tile(m, 1100)
    xn = _rmsnorm(x, lw["g_mix"], BF16)
    h = _in_proj(xn, w["w_in_t"], l, tm=tm_big, tn=768)
    la, bsum = _gate(xn, w["w_in_t"], l, lw["w_alpha2"], lw["b_alpha"], tm=_pick_tile(m, 704, GLA_ROWS), block=GLA_ROWS)

    gla = functools.partial(_gla, h, la, g_gla=lw["g_gla"], out_rows=m)
    oa, sa_p = gla(bsum, n_seq=n_p, seq_rows=seq, rows=GLA_ROWS, row0=0, t_valid=None, state0=None, alias=None)
    oa, sa_s = gla(None, n_seq=n_s, seq_rows=SEQ_PAD, rows=SEQ_PAD, row0=m_p, t_valid=1, state0=state_gla, alias=oa)

    ret = functools.partial(_ret, h, out_rows=m)
    ob, sb_p = ret(tabs["cos_p"], tabs["sin_p"], tabs["lg"], n_seq=n_p, seq_rows=seq, rows=RET_ROWS, row0=0,
                   t_valid=None, state0=None, alias=None)
    ob, sb_s = ret(tabs["cos_s"], tabs["sin_s"], tabs["lg"], n_seq=n_s, seq_rows=SEQ_PAD, rows=SEQ_PAD, row0=m_p,
                   t_valid=1, state0=state_ret, alias=ob)

    oc = _dil_prefill(h, tabs["slopes"], n_seq=n_p, seq=seq, out_rows=m)
    oc = _dil_step(h, tabs["slopes"], caches, l, oc, n_seq=n_s, row0=m_p)

    gates = _matmul_ws(xn, w["w_merge"], l, tm=tm_big, tn=768, out_dtype=BF16, act=_sigmoid, chunks=ROW_CHUNKS,
                       name="merge_gates")
    merged = _merge_up(oa, ob, oc, gates, w["w_up_a"], w["w_up_b"], w["w_up_c"], l, tm=tm_big, tn=512)
    x = _matmul_ws(merged, w["w_out"], l, tm=tm_big, tn=512, residual=x, chunks=ROW_CHUNKS, name="out_proj")
    xf = _rmsnorm(x, lw["g_ffn"], BF16)
    hid = _ffn_up(xf, w["w_ffn_gate"], w["w_ffn_up"], l, tm=_pick_tile(m, 1728), tn=256)
    d_ff = hid.shape[1]
    x = _matmul_ksplit_res(hid, w["w_ffn_down"], l, x, tm=tm, tn=512, tk=d_ff // 2, name="ffn_down")

    win_p, new_s = [], []
    for g in range(N_GROUPS_C):
        off_k = OFF_C + g * 3 * V_C + V_C
        keep = min(BAND * DILATIONS[g], seq)
        kv_p = jnp.stack([lax.slice(h, ((b + 1) * seq - keep, off_k), ((b + 1) * seq, off_k + 2 * V_C))
                          for b in range(n_p)])
        win_p.append(kv_p.reshape(n_p, keep, 2, G_C, HD_C).transpose(0, 2, 1, 3, 4))
        kv_s = lax.slice(h, (m_p, off_k), (m, off_k + 2 * V_C)).reshape(n_s, SEQ_PAD, 2, G_C, HD_C)[:, :1]
        new_s.append(kv_s.transpose(0, 2, 1, 3, 4))
    return x, (sa_p, sa_s, sb_p, sb_s), win_p, new_s


def kernel(x_prompt, x_sample, state_gla, state_ret, cache_win0, cache_win1, cache_win2, g_mix, w_in, w_alpha2,
           b_alpha, g_gla, w_merge, w_up_a, w_up_b, w_up_c, w_out, g_ffn, w_ffn_gate, w_ffn_up, w_ffn_down, g_final):
    n_p, seq, d = x_prompt.shape
    n_s = x_sample.shape[0]
    depth = g_mix.shape[0]
    assert d == D_MODEL and x_sample.shape[1] == 1 and seq % (BAND * DILATIONS[-1]) == 0 and seq % RET_ROWS == 0
    m_p = n_p * seq
    m = m_p + n_s * SEQ_PAD
    dims = (n_p, seq, n_s, m_p, m)

    xs = jnp.pad(x_sample, ((0, 0), (0, SEQ_PAD - 1), (0, 0)))
    x = jnp.concatenate([x_prompt.reshape(m_p, d), xs.reshape(n_s * SEQ_PAD, d)], axis=0)

    cos_p, sin_p = _rope_tables(np.arange(seq))
    cos_s, sin_s = _rope_tables(PAST_LEN + np.arange(SEQ_PAD))
    log_gamma = np.log1p(-(2.0 ** (-5.0 - np.arange(H_B, dtype=np.float64))))
    heads = np.arange(1, N_GROUPS_C * G_C + 1, dtype=np.float64)
    tabs = {
        "cos_p": cos_p, "sin_p": sin_p, "cos_s": cos_s, "sin_s": sin_s,
        "lg": jnp.asarray(np.broadcast_to(log_gamma[:, None], (H_B, RET_ROWS)), F32),
        "slopes": jnp.asarray(2.0 ** (-8.0 * heads / (N_GROUPS_C * G_C)), F32),
    }

    assert w_in.shape[2] == N_MAIN + GATE_RANK
    w = {"w_in_t": jnp.swapaxes(w_in, 1, 2), "w_merge": w_merge, "w_up_a": w_up_a, "w_up_b": w_up_b, "w_up_c": w_up_c,
         "w_out": w_out, "w_ffn_gate": w_ffn_gate, "w_ffn_up": w_ffn_up, "w_ffn_down": w_ffn_down.astype(BF16)}
    caches = (cache_win0, cache_win1, cache_win2)
    outs = {k: [] for k in ("gla_p", "gla_s", "ret_p", "ret_s")}
    win_p = [[] for _ in range(N_GROUPS_C)]
    win_s = [[] for _ in range(N_GROUPS_C)]
    for l in range(depth):
        lw = {
            "g_mix": g_mix[l], "g_gla": g_gla[l], "g_ffn": g_ffn[l], "b_alpha": b_alpha[l],
            "w_alpha2": w_alpha2[l],
        }
        x, (sa_p, sa_s, sb_p, sb_s), wp, ws = _layer(x, l, w, lw, state_gla[l], state_ret[l], caches, tabs, dims)
        outs["gla_p"].append(sa_p)
        outs["gla_s"].append(sa_s)
        outs["ret_p"].append(sb_p)
        outs["ret_s"].append(sb_s)
        for g in range(N_GROUPS_C):
            win_p[g].append(wp[g])
            win_s[g].append(ws[g])

    y_prompt = _rmsnorm(x, g_final, F32, rows=m_p).reshape(n_p, seq, d)
    y_sample = _rmsnorm(x[m_p:], g_final, F32).reshape(n_s, SEQ_PAD, d)[:, :1]
    win_s = [_shift_windows(caches[g], jnp.stack(win_s[g])) for g in range(N_GROUPS_C)]
    return (y_prompt, y_sample, jnp.stack(outs["gla_p"]), jnp.stack(outs["gla_s"]), jnp.stack(outs["ret_p"]),
            jnp.stack(outs["ret_s"]), jnp.stack(win_p[0]), win_s[0], jnp.stack(win_p[1]), win_s[1],
            jnp.stack(win_p[2]), win_s[2])
```

```python
import functools

import numpy as np
import jax
import jax.numpy as jnp
from jax import lax
from jax.experimental import pallas as pl
from jax.experimental.pallas import tpu as pltpu

F32 = jnp.float32
BF16 = jnp.bfloat16
HIGHEST = lax.Precision.HIGHEST

D_MODEL = 4096
H_A, DK_A, DV_A = 4, 256, 512
GATE_RANK, GATE_TAU = 16, 16.0
H_B, DK_B, DV_B = 8, 128, 256
ROPE_BASE = 10000.0
DILATIONS = (1, 4, 16)
N_GROUPS_C, G_C, HD_C = 3, 4, 128
BAND = 128
PAST_LEN = 16384
QK_A, V_A = H_A * DK_A, H_A * DV_A
QK_B, V_B = H_B * DK_B, H_B * DV_B
V_C = G_C * HD_C
C_COLS = N_GROUPS_C * 3 * V_C
EPS = 1e-6
NEG = -1e30

OFF_QA, OFF_KA, OFF_VA, OFF_RA = 0, QK_A, 2 * QK_A, 2 * QK_A + V_A
OFF_B = 2 * QK_A + 2 * V_A
OFF_QB, OFF_KB, OFF_VB, OFF_GB = OFF_B, OFF_B + QK_B, OFF_B + 2 * QK_B, OFF_B + 2 * QK_B + V_B
OFF_C = OFF_B + 2 * QK_B + 2 * V_B
N_MAIN = OFF_C + C_COLS

LANES = 128
SEQ_PAD = 16
GLA_ROWS = 128
RET_ROWS = 256
RET_HEADS = 8
SUBLANES = 8
ROW_CHUNKS = 5
VMEM_CAP = 56 * 1024 * 1024


def _vmem(nbytes):
    return int(min(VMEM_CAP, nbytes + (6 << 20)))


def _params(sem, nbytes):
    return pltpu.CompilerParams(dimension_semantics=sem, vmem_limit_bytes=_vmem(nbytes))


def _pick_tile(m, cap, align=16):
    best = align
    for t in range(align, cap + 1, align):
        if m % t == 0:
            best = t
    assert m % best == 0, (m, cap, align)
    return best


def _iota(shape, dim):
    return lax.broadcasted_iota(jnp.int32, shape, dim)


def _dot(a, b):
    return jnp.dot(a, b, preferred_element_type=F32)


def _dot_nt(a, b):
    return lax.dot_general(a, b, (((1,), (1,)), ((), ())), preferred_element_type=F32)


def _dot_tn(a, b):
    return lax.dot_general(a, b, (((0,), (0,)), ((), ())), preferred_element_type=F32)


def _sigmoid(x):
    return 1.0 / (1.0 + jnp.exp(-x))


def _rmsnorm_body(x_ref, g_ref, o_ref):
    x = x_ref[...]
    ms = jnp.mean(x * x, axis=-1, keepdims=True)
    o_ref[...] = (x * lax.rsqrt(ms + EPS) * g_ref[...]).astype(o_ref.dtype)


def _rmsnorm(x, g, out_dtype, rows=None):
    m, d = x.shape
    m = m if rows is None else rows
    tm = _pick_tile(m, 512)
    nbytes = 2 * tm * d * (4 + jnp.dtype(out_dtype).itemsize) + 2 * tm * d * 4
    return pl.pallas_call(
        _rmsnorm_body,
        out_shape=jax.ShapeDtypeStruct((m, d), out_dtype),
        grid=(m // tm,),
        in_specs=[pl.BlockSpec((tm, d), lambda i: (i, 0)), pl.BlockSpec((1, d), lambda i: (0, 0))],
        out_specs=pl.BlockSpec((tm, d), lambda i: (i, 0)),
        compiler_params=_params(("parallel",), nbytes),
        name="rmsnorm",
    )(x, g.reshape(1, d))


def _row_chunks(tm, want):
    if want <= 1 or tm % (want * 16) != 0:
        return [slice(0, tm)]
    return [slice(c * (tm // want), (c + 1) * (tm // want)) for c in range(want)]


def _mm_ws_body(*refs, act, has_res, transposed, chunks):
    x_ref, w_ref = refs[:2]
    r_ref = refs[2] if has_res else None
    o_ref = refs[-1]
    w = w_ref[0].astype(BF16) if transposed else w_ref[...].astype(BF16)
    for rows in _row_chunks(x_ref.shape[0], chunks):
        y = _dot_nt(x_ref[rows, :], w) if transposed else _dot(x_ref[rows, :], w)
        if act is not None:
            y = act(y)
        if has_res:
            y = r_ref[rows, :] + y
        o_ref[rows, :] = y.astype(o_ref.dtype)


def _matmul_ws(x, w, layer, *, tm, tn, n=None, row_of=None, out_dtype=F32, act=None, residual=None, chunks=1,
               name="matmul"):
    m, k = x.shape
    transposed = row_of is not None
    n = w.shape[2] if n is None else n
    assert m % tm == 0 and n % tn == 0
    if transposed:
        w_spec = pl.BlockSpec((pl.Element(1), pl.Element(tn), pl.Element(k)),
                              lambda j, i: (layer, pl.multiple_of(row_of(j), SUBLANES), 0))
    else:
        w_spec = pl.BlockSpec((None, k, tn), lambda j, i: (layer, 0, j))
    in_specs = [pl.BlockSpec((tm, k), lambda j, i: (i, 0)), w_spec]
    args = [x, w]
    osz = jnp.dtype(out_dtype).itemsize
    nbytes = 2 * (tm * k * 2 + k * tn * 4 + tm * tn * osz) + k * tn * 2 + 2 * tm * tn * 4
    if residual is not None:
        in_specs.append(pl.BlockSpec((tm, tn), lambda j, i: (i, j)))
        args.append(residual)
        nbytes += 2 * tm * tn * 4
    return pl.pallas_call(
        functools.partial(_mm_ws_body, act=act, has_res=residual is not None, transposed=transposed, chunks=chunks),
        out_shape=jax.ShapeDtypeStruct((m, n), out_dtype),
        grid=(n // tn, m // tm),
        in_specs=in_specs,
        out_specs=pl.BlockSpec((tm, tn), lambda j, i: (i, j)),
        compiler_params=_params(("parallel", "parallel"), nbytes),
        name=name,
    )(*args)


def _in_proj(x, w_in_t, layer, *, tm, tn):
    assert OFF_B % tn == 0
    return _matmul_ws(x, w_in_t, layer, tm=tm, tn=tn, n=N_MAIN, name="in_proj",
                      row_of=lambda j: j * tn + jnp.where(j >= OFF_B // tn, GATE_RANK, 0))


def _mm_ksplit_res_body(x_ref, w_ref, r_ref, o_ref, acc_ref, *, tm, n_k):
    kk = pl.program_id(1)
    rows = pl.ds(pl.multiple_of(pl.program_id(2) * tm, tm), tm)
    part = _dot(x_ref[...], w_ref[...])

    @pl.when(kk == 0)
    def _():
        acc_ref[rows, :] = r_ref[...] + part

    @pl.when(jnp.logical_and(kk > 0, kk < n_k - 1))
    def _():
        acc_ref[rows, :] += part

    @pl.when(kk == n_k - 1)
    def _():
        o_ref[...] = acc_ref[rows, :] + part


def _matmul_ksplit_res(x, w, layer, residual, *, tm, tn, tk, name):
    m, k = x.shape
    n = w.shape[2]
    assert m % tm == 0 and n % tn == 0 and k % tk == 0 and k // tk >= 2
    n_k, n_i = k // tk, m // tm
    nbytes = 2 * (tm * tk * 2 + tk * tn * 2 + 2 * tm * tn * 4) + m * tn * 4 + 2 * tm * tn * 4
    return pl.pallas_call(
        functools.partial(_mm_ksplit_res_body, tm=tm, n_k=n_k),
        out_shape=jax.ShapeDtypeStruct((m, n), F32),
        grid=(n // tn, n_k, n_i),
        in_specs=[
            pl.BlockSpec((tm, tk), lambda j, kk, i: (i, kk)),
            pl.BlockSpec((None, tk, tn), lambda j, kk, i: (layer, kk, j)),
            pl.BlockSpec((tm, tn), lambda j, kk, i: (jnp.where(kk == 0, i, n_i - 1), j)),
        ],
        out_specs=pl.BlockSpec((tm, tn), lambda j, kk, i: (jnp.where(kk == n_k - 1, i, 0), j)),
        scratch_shapes=[pltpu.VMEM((m, tn), F32)],
        compiler_params=_params(("parallel", "arbitrary", "arbitrary"), nbytes),
        name=name,
    )(x, w, residual)


def _merge_up_body(oa_ref, ob_ref, oc_ref, ga_ref, gb_ref, gc_ref, wa_ref, wb_ref, wc_ref, o_ref):
    wa, wb, wc = wa_ref[...].astype(BF16), wb_ref[...].astype(BF16), wc_ref[...].astype(BF16)
    for rows in _row_chunks(oa_ref.shape[0], ROW_CHUNKS):
        acc = ga_ref[rows, :].astype(F32) * _dot(oa_ref[rows, :], wa)
        acc += gb_ref[rows, :].astype(F32) * _dot(ob_ref[rows, :], wb)
        acc += gc_ref[rows, :].astype(F32) * _dot(oc_ref[rows, :], wc)
        o_ref[rows, :] = acc.astype(o_ref.dtype)


def _merge_up(oa, ob, oc, gates, w_up_a, w_up_b, w_up_c, layer, *, tm, tn):
    m = oa.shape[0]
    d = w_up_a.shape[2]
    nt = d // tn
    assert m % tm == 0 and d % tn == 0

    def row(width):
        return pl.BlockSpec((tm, width), lambda j, i: (i, 0))

    def gate(branch):
        return pl.BlockSpec((tm, tn), lambda j, i: (i, branch * nt + j))

    def panel(kdim):
        return pl.BlockSpec((None, kdim, tn), lambda j, i: (layer, 0, j))

    kk = V_A + V_B + V_C
    nbytes = 2 * (tm * kk * 2 + 3 * tm * tn * 2 + kk * tn * 4 + tm * tn * 2) + kk * tn * 2 + 6 * tm * tn * 4
    return pl.pallas_call(
        _merge_up_body,
        out_shape=jax.ShapeDtypeStruct((m, d), BF16),
        grid=(nt, m // tm),
        in_specs=[row(V_A), row(V_B), row(V_C), gate(0), gate(1), gate(2), panel(V_A), panel(V_B), panel(V_C)],
        out_specs=pl.BlockSpec((tm, tn), lambda j, i: (i, j)),
        compiler_params=_params(("parallel", "parallel"), nbytes),
        name="merge_up",
    )(oa, ob, oc, gates, gates, gates, w_up_a, w_up_b, w_up_c)


def _ffn_up_body(x_ref, wg_ref, wu_ref, o_ref):
    wg = wg_ref[...].astype(BF16)
    wu = wu_ref[...].astype(BF16)
    for rows in _row_chunks(x_ref.shape[0], 4):
        x = x_ref[rows, :]
        a = _dot(x, wg)
        o_ref[rows, :] = (a * _sigmoid(a) * _dot(x, wu)).astype(o_ref.dtype)


def _ffn_up(x, wg, wu, layer, *, tm, tn):
    m, k = x.shape
    n = wg.shape[2]
    assert m % tm == 0 and n % tn == 0
    nbytes = 2 * (tm * k * 2 + 2 * k * tn * 4 + tm * tn * 2) + 2 * k * tn * 2 + 3 * tm * tn * 4
    return pl.pallas_call(
        _ffn_up_body,
        out_shape=jax.ShapeDtypeStruct((m, n), BF16),
        grid=(m // tm, n // tn),
        in_specs=[pl.BlockSpec((tm, k), lambda i, j: (i, 0)), pl.BlockSpec((None, k, tn), lambda i, j: (layer, 0, j)),
                  pl.BlockSpec((None, k, tn), lambda i, j: (layer, 0, j))],
        out_specs=pl.BlockSpec((tm, tn), lambda i, j: (i, j)),
        compiler_params=_params(("parallel", "parallel"), nbytes),
        name="ffn_up",
    )(x, wg, wu)


def _gla_body(*refs, rows, n_t, t_valid, heads, has_bsum, has_s0, has_alias):
    it = iter(refs)
    q_ref, k_ref, v_ref, r_ref, la_ref, g_ref = [next(it) for _ in range(6)]
    bsum_ref = next(it) if has_bsum else None
    s0_ref = next(it) if has_s0 else None
    if has_alias:
        next(it)
    o_ref, sout_ref, st_scr, b_scr = [next(it) for _ in range(4)]
    t = pl.program_id(1)

    @pl.when(t == 0)
    def _():
        for hh in range(heads):
            st_scr[hh] = s0_ref[hh].T if has_s0 else jnp.zeros(st_scr.shape[1:], F32)

    for hh in range(heads):
        kk = slice(hh * DK_A, (hh + 1) * DK_A)
        vv = slice(hh * DV_A, (hh + 1) * DV_A)
        _gla_head(q_ref.at[:, kk], k_ref.at[:, kk], v_ref.at[:, vv], r_ref.at[:, vv], la_ref.at[:, kk], g_ref,
                  None if bsum_ref is None else bsum_ref.at[:, kk], o_ref.at[:, vv], st_scr.at[hh], b_scr.at[hh],
                  rows=rows, t_valid=t_valid)

    @pl.when(t == n_t - 1)
    def _():
        for hh in range(heads):
            sout_ref[hh] = st_scr[hh].T


def _gla_head(q_ref, k_ref, v_ref, r_ref, la_ref, g_ref, bsum_ref, o_ref, st_scr, b_scr, *, rows, t_valid):
    la = la_ref[...]
    row_k = _iota((rows, DK_A), 0)
    if t_valid is not None:
        la = jnp.where(row_k < t_valid, la, 0.0)
    if bsum_ref is not None:
        b = bsum_ref[...]
    else:
        tri = (_iota((rows, rows), 0) >= _iota((rows, rows), 1)).astype(F32)
        b = jnp.dot(tri, la, precision=HIGHEST, preferred_element_type=F32)
    b_scr[...] = b

    q = q_ref[...] * (DK_A ** -0.5)
    k = k_ref[...]
    v16 = v_ref[...].astype(BF16)
    st = st_scr[...]
    o = _dot_nt((q * jnp.exp(b)).astype(BF16), st.astype(BF16))

    t_idx = _iota((rows, rows), 0)
    s_idx = _iota((rows, rows), 1)
    scores = jnp.where(t_idx == s_idx, _dot_nt(q.astype(BF16), k.astype(BF16)), 0.0)
    m = rows // 2
    while m >= 1:
        sub = row_k & (2 * m - 1)
        if m >= 4:
            pieces = [jnp.broadcast_to(b_scr[pl.ds(p * 2 * m + m - 1, 1), :], (2 * m, DK_A)) for p in range(rows // (2 * m))]
            log_fac = -jnp.abs(b - (pieces[0] if len(pieces) == 1 else jnp.concatenate(pieces, axis=0)))
        elif m == 2:
            la_next = pltpu.roll(la, rows - 1, 0)
            la_prev = pltpu.roll(la, 1, 0)
            log_fac = jnp.where(sub == 0, la_next, jnp.where(sub == 1, 0.0, jnp.where(sub == 2, la, la + la_prev)))
        else:
            log_fac = jnp.where(sub == 1, la, 0.0)
        fac = jnp.exp(log_fac)
        upper = sub >= m
        qs = jnp.where(upper, q * fac, 0.0)
        ks = jnp.where(upper, 0.0, k * fac)
        part = _dot_nt(qs.astype(BF16), ks.astype(BF16))
        if 2 * m < rows:
            shift = (2 * m).bit_length() - 1
            part = jnp.where((t_idx >> shift) == (s_idx >> shift), part, 0.0)
        scores = scores + part
        m //= 2
    o = o + _dot(scores.astype(BF16), v16)

    b_last = b_scr[pl.ds(rows - 1, 1), :]
    k_st = k if t_valid is None else jnp.where(row_k < t_valid, k, 0.0)
    st_new = st * jnp.exp(b_last) + _dot_tn(v16, (k_st * jnp.exp(b_last - b)).astype(BF16))
    st_scr[...] = st_new

    ms = jnp.mean(o * o, axis=-1, keepdims=True)
    r = r_ref[...]
    o_ref[...] = (r * _sigmoid(r) * (o * lax.rsqrt(ms + EPS) * g_ref[...])).astype(o_ref.dtype)


def _gla(h, la, bsum, g_gla, *, n_seq, seq_rows, rows, row0, t_valid, state0, out_rows, alias, heads=H_A):
    n_t = seq_rows // rows
    rb0 = row0 // rows
    groups = H_A // heads
    wk, wv = heads * DK_A, heads * DV_A

    def rmap(blk0):
        return lambda s, t: (rb0 + (s // groups) * n_t + t, blk0 + s % groups)

    smap = lambda s, t: (s // groups, s % groups, 0, 0)
    in_specs = [
        pl.BlockSpec((rows, wk), rmap(OFF_QA // wk)),
        pl.BlockSpec((rows, wk), rmap(OFF_KA // wk)),
        pl.BlockSpec((rows, wv), rmap(OFF_VA // wv)),
        pl.BlockSpec((rows, wv), rmap(OFF_RA // wv)),
        pl.BlockSpec((rows, wk), rmap(0)),
        pl.BlockSpec((1, DV_A), lambda s, t: (0, 0)),
    ]
    args = [h, h, h, h, la, g_gla.reshape(1, DV_A)]
    if bsum is not None:
        in_specs.append(pl.BlockSpec((rows, wk), rmap(0)))
        args.append(bsum)
    if state0 is not None:
        in_specs.append(pl.BlockSpec((None, heads, DK_A, DV_A), smap))
        args.append(state0)
    aliases = {}
    if alias is not None:
        in_specs.append(pl.BlockSpec(memory_space=pl.ANY))
        aliases = {len(args): 0}
        args.append(alias)
    nbytes = heads * (2 * rows * (4 * DK_A + 2 * DV_A) * 4 + 2 * rows * DV_A * 2 + 5 * DK_A * DV_A * 4
                      + rows * DK_A * 4 + 16 * rows * DV_A * 4)
    return pl.pallas_call(
        functools.partial(_gla_body, rows=rows, n_t=n_t, t_valid=t_valid, heads=heads, has_bsum=bsum is not None,
                          has_s0=state0 is not None, has_alias=alias is not None),
        out_shape=(jax.ShapeDtypeStruct((out_rows, V_A), BF16), jax.ShapeDtypeStruct((n_seq, H_A, DK_A, DV_A), F32)),
        grid=(n_seq * groups, n_t),
        in_specs=in_specs,
        out_specs=(pl.BlockSpec((rows, wv), rmap(0)), pl.BlockSpec((None, heads, DK_A, DV_A), smap)),
        scratch_shapes=[pltpu.VMEM((heads, DV_A, DK_A), F32), pltpu.VMEM((heads, rows, DK_A), F32)],
        input_output_aliases=aliases,
        compiler_params=_params(("parallel", "arbitrary"), nbytes),
        name="gla",
    )(*args)


def _gate_body(x_ref, w_ref, wa_hi_ref, wa_lo_ref, ba_ref, la_ref, bsum_ref, *, block):
    za = _dot_nt(x_ref[...], w_ref[0].astype(BF16))
    za_hi = za.astype(BF16)
    za_lo = (za - za_hi.astype(F32)).astype(BF16)
    wa_hi = wa_hi_ref[...]
    z = _dot(za_hi, wa_hi) + (_dot(za_lo, wa_hi) + _dot(za_hi, wa_lo_ref[...])) + ba_ref[...]
    la = (jnp.minimum(z, 0.0) - jnp.log(1.0 + jnp.exp(-jnp.abs(z)))) * (1.0 / GATE_TAU)
    la_ref[...] = la
    tri = (_iota((block, block), 0) >= _iota((block, block), 1)).astype(BF16)
    la_hi = la.astype(BF16)
    rest = la - la_hi.astype(F32)
    la_mid = rest.astype(BF16)
    la_lo = (rest - la_mid.astype(F32)).astype(BF16)
    for p in range(x_ref.shape[0] // block):
        rows = slice(p * block, (p + 1) * block)
        bsum_ref[rows, :] = _dot(tri, la_hi[rows]) + (_dot(tri, la_mid[rows]) + _dot(tri, la_lo[rows]))


def _gate(x, w_in_t, layer, w_alpha2, b_alpha, *, tm, block):
    m, k = x.shape
    assert m % tm == 0 and tm % block == 0 and OFF_B % SUBLANES == 0
    wa = jnp.pad(w_alpha2, ((0, LANES - GATE_RANK), (0, 0)))
    wa_hi = wa.astype(BF16)
    wa_lo = (wa - wa_hi.astype(F32)).astype(BF16)
    full = lambda shape: pl.BlockSpec(shape, lambda i: (0, 0))
    nbytes = 2 * (tm * k * 2 + k * LANES * 4 + 2 * tm * QK_A * 4) + k * LANES * 2 + 8 * tm * QK_A * 4
    return pl.pallas_call(
        functools.partial(_gate_body, block=block),
        out_shape=(jax.ShapeDtypeStruct((m, QK_A), F32), jax.ShapeDtypeStruct((m, QK_A), F32)),
        grid=(m // tm,),
        in_specs=[pl.BlockSpec((tm, k), lambda i: (i, 0)),
                  pl.BlockSpec((pl.Element(1), pl.Element(LANES), pl.Element(k)), lambda i: (layer, OFF_B, 0)),
                  full((LANES, QK_A)), full((LANES, QK_A)), full((1, QK_A))],
        out_specs=(pl.BlockSpec((tm, QK_A), lambda i: (i, 0)), pl.BlockSpec((tm, QK_A), lambda i: (i, 0))),
        compiler_params=_params(("parallel",), nbytes),
        name="gla_gate",
    )(x, w_in_t, wa_hi, wa_lo, b_alpha.reshape(1, QK_A))


def _ret_body(*refs, rows, n_t, t_valid, heads, has_s0, has_alias):
    it = iter(refs)
    q_ref, k_ref, v_ref, g_ref, cos_ref, sin_ref, lg_ref = [next(it) for _ in range(7)]
    s0_ref = next(it) if has_s0 else None
    if has_alias:
        next(it)
    o_ref, sout_ref, st_scr, rel_scr, dq_scr, dk_scr = [next(it) for _ in range(6)]
    t = pl.program_id(1)
    head0 = (pl.program_id(0) % (H_B // heads)) * heads
    n_valid = rows if t_valid is None else t_valid
    row_k = _iota((rows, DK_B), 0)

    @pl.when(t == 0)
    def _():
        steps = jnp.minimum(row_k + 1, n_valid).astype(F32)
        st = jnp.minimum(_iota((rows, rows), 0) + 1, n_valid)
        ss = jnp.minimum(_iota((rows, rows), 1) + 1, n_valid)
        causal = _iota((rows, rows), 0) >= _iota((rows, rows), 1)
        for hh in range(heads):
            st_scr[hh] = s0_ref[hh] if has_s0 else jnp.zeros(st_scr.shape[1:], F32)
            lg = lg_ref[pl.ds(head0 + hh, 1), :]
            dq_scr[hh] = jnp.exp(lg[:, :DK_B] * steps)
            dk_scr[hh] = jnp.exp(lg[:, :DK_B] * (n_valid - steps))
            rel_scr[hh] = jnp.exp(jnp.where(causal, lg[:, :rows] * (st - ss).astype(F32), NEG))

    for hh in range(heads):
        kk = slice(hh * DK_B, (hh + 1) * DK_B)
        vv = slice(hh * DV_B, (hh + 1) * DV_B)
        _ret_head(q_ref.at[:, kk], k_ref.at[:, kk], v_ref.at[:, vv], g_ref.at[:, vv], cos_ref, sin_ref,
                  lg_ref.at[pl.ds(head0 + hh, 1), :], o_ref.at[:, vv], st_scr.at[hh], rel_scr.at[hh], dq_scr.at[hh],
                  dk_scr.at[hh], rows=rows, t_valid=t_valid)

    @pl.when(t == n_t - 1)
    def _():
        for hh in range(heads):
            sout_ref[hh] = st_scr[hh]


def _ret_head(q_ref, k_ref, v_ref, g_ref, cos_ref, sin_ref, lg_ref, o_ref, st_scr, rel_scr, dq_scr, dk_scr, *,
              rows, t_valid):
    lg = lg_ref[...]
    n_valid = rows if t_valid is None else t_valid
    row_k = _iota((rows, DK_B), 0)
    cos = cos_ref[...]
    sin = sin_ref[...]

    def rot(x):
        return x * cos + pltpu.roll(x, DK_B // 2, 1) * sin

    q = rot(q_ref[...])
    k = rot(k_ref[...]) * (DK_B ** -0.5)
    if t_valid is not None:
        k = jnp.where(row_k < t_valid, k, 0.0)
    decay_q = dq_scr[...]
    decay_k = dk_scr[...]

    v16 = v_ref[...].astype(BF16)
    state = st_scr[...]
    scores = _dot_nt(q.astype(BF16), k.astype(BF16)) * rel_scr[...]
    o = _dot(scores.astype(BF16), v16) + _dot((q * decay_q).astype(BF16), state.astype(BF16))
    state_new = state * jnp.exp(lg[:, :DV_B] * float(n_valid)) + _dot_tn((k * decay_k).astype(BF16), v16)
    st_scr[...] = state_new

    ms = jnp.mean(o * o, axis=-1, keepdims=True)
    g = g_ref[...]
    o_ref[...] = (g * _sigmoid(g) * (o * lax.rsqrt(ms + EPS))).astype(o_ref.dtype)


def _ret(h, cos, sin, lg_tab, *, n_seq, seq_rows, rows, row0, t_valid, state0, out_rows, alias, heads=RET_HEADS):
    n_t = seq_rows // rows
    rb0 = row0 // rows
    groups = H_B // heads
    wk, wv = heads * DK_B, heads * DV_B

    def rmap(blk0):
        return lambda s, t: (rb0 + (s // groups) * n_t + t, blk0 + s % groups)

    smap = lambda s, t: (s // groups, s % groups, 0, 0)
    in_specs = [
        pl.BlockSpec((rows, wk), rmap(OFF_QB // wk)),
        pl.BlockSpec((rows, wk), rmap(OFF_KB // wk)),
        pl.BlockSpec((rows, wv), rmap(OFF_VB // wv)),
        pl.BlockSpec((rows, wv), rmap(OFF_GB // wv)),
        pl.BlockSpec((rows, DK_B), lambda s, t: (t, 0)),
        pl.BlockSpec((rows, DK_B), lambda s, t: (t, 0)),
        pl.BlockSpec((H_B, RET_ROWS), lambda s, t: (0, 0)),
    ]
    args = [h, h, h, h, cos, sin, lg_tab]
    if state0 is not None:
        in_specs.append(pl.BlockSpec((None, heads, DK_B, DV_B), smap))
        args.append(state0)
    aliases = {}
    if alias is not None:
        in_specs.append(pl.BlockSpec(memory_space=pl.ANY))
        aliases = {len(args): 0}
        args.append(alias)
    nbytes = heads * (2 * rows * (2 * DK_B + 2 * DV_B) * 4 + 2 * rows * DV_B * 2 + 5 * DK_B * DV_B * 4
                      + 6 * rows * rows * 4 + 8 * rows * DV_B * 4) + 4 * rows * DK_B * 4
    return pl.pallas_call(
        functools.partial(_ret_body, rows=rows, n_t=n_t, t_valid=t_valid, heads=heads, has_s0=state0 is not None,
                          has_alias=alias is not None),
        out_shape=(jax.ShapeDtypeStruct((out_rows, V_B), BF16), jax.ShapeDtypeStruct((n_seq, H_B, DK_B, DV_B), F32)),
        grid=(n_seq * groups, n_t),
        in_specs=in_specs,
        out_specs=(pl.BlockSpec((rows, wv), rmap(0)), pl.BlockSpec((None, heads, DK_B, DV_B), smap)),
        scratch_shapes=[pltpu.VMEM((heads, DK_B, DV_B), F32), pltpu.VMEM((heads, rows, rows), F32),
                        pltpu.VMEM((heads, rows, DK_B), F32), pltpu.VMEM((heads, rows, DK_B), F32)],
        input_output_aliases=aliases,
        compiler_params=_params(("parallel", "arbitrary"), nbytes),
        name="retention",
    )(*args)


def _dil_prefill_body(slopes_ref, *refs, seq):
    qkv = refs[:9]
    o_ref, og_scr, lse_scr = refs[9:]
    slot = pl.program_id(1)
    scale = HD_C ** -0.5
    rr = _iota((BAND, BAND), 0)
    cc = _iota((BAND, BAND), 1)
    dist_cur = (rr - cc).astype(F32)
    ok_cur = rr >= cc
    ok_prev = cc >= rr
    n_units = seq // BAND
    for g, d in enumerate(DILATIONS):
        q_ref, k_ref, v_ref = qkv[3 * g:3 * g + 3]
        pen = slopes_ref[g * G_C + slot] * float(d)
        shift = d.bit_length() - 1

        def rows_of(start, d=d):
            return pl.ds(start, BAND) if d == 1 else pl.ds(start, BAND, stride=d)

        has_prev = seq // (BAND * d) > 1

        def unit(u, carry, q_ref=q_ref, k_ref=k_ref, v_ref=v_ref, pen=pen, d=d, shift=shift, rows_of=rows_of, g=g,
                 has_prev=has_prev):
            blk = u >> shift
            start = (u & (d - 1)) + blk * (BAND * d)
            q = q_ref[rows_of(start), :].astype(BF16)
            keys = k_ref[rows_of(start), :].astype(BF16)
            vals = v_ref[rows_of(start), :].astype(BF16)
            if has_prev:
                prev = jnp.maximum(start - BAND * d, 0)
                keys = jnp.concatenate([k_ref[rows_of(prev), :].astype(BF16), keys], axis=0)
                vals = jnp.concatenate([v_ref[rows_of(prev), :].astype(BF16), vals], axis=0)
                bias = jnp.concatenate([jnp.where(ok_prev, -pen * (dist_cur + float(BAND)), NEG)
                                        + jnp.where(blk > 0, 0.0, NEG),
                                        jnp.where(ok_cur, -pen * dist_cur, NEG)], axis=1)
            else:
                bias = jnp.where(ok_cur, -pen * dist_cur, NEG)
            s = _dot_nt(q, keys) * scale + bias
            m = jnp.max(s, axis=-1, keepdims=True)
            p = jnp.exp(s - m)
            l = jnp.sum(p, axis=-1, keepdims=True)
            acc = _dot(p.astype(BF16), vals)
            og_scr[g, rows_of(start), :] = acc / l
            lse_scr[g, rows_of(start), :] = jnp.broadcast_to(m + jnp.log(l), (BAND, HD_C))
            return carry

        lax.fori_loop(0, n_units, unit, 0, unroll=True)

    lse = [lse_scr[g] for g in range(N_GROUPS_C)]
    top = jnp.maximum(jnp.maximum(lse[0], lse[1]), lse[2])
    w = [jnp.exp(x - top) for x in lse]
    total = w[0] + w[1] + w[2]
    o_ref[...] = ((w[0] * og_scr[0] + w[1] * og_scr[1] + w[2] * og_scr[2]) / total).astype(o_ref.dtype)


def _dil_prefill(h, slopes, *, n_seq, seq, out_rows):
    def cmap(g, j):
        return lambda b, s: (b, (OFF_C + g * 3 * V_C + j * V_C) // HD_C + s)

    in_specs = [pl.BlockSpec(memory_space=pltpu.SMEM)]
    in_specs += [pl.BlockSpec((seq, HD_C), cmap(g, j)) for g in range(N_GROUPS_C) for j in range(3)]
    nbytes = 2 * 9 * seq * HD_C * 4 + 2 * seq * HD_C * 2 + 2 * N_GROUPS_C * seq * HD_C * 4 + 8 * seq * HD_C * 4
    return pl.pallas_call(
        functools.partial(_dil_prefill_body, seq=seq),
        out_shape=jax.ShapeDtypeStruct((out_rows, V_C), BF16),
        grid=(n_seq, G_C),
        in_specs=in_specs,
        out_specs=pl.BlockSpec((seq, HD_C), lambda b, s: (b, s)),
        scratch_shapes=[pltpu.VMEM((N_GROUPS_C, seq, HD_C), F32), pltpu.VMEM((N_GROUPS_C, seq, HD_C), F32)],
        compiler_params=_params(("parallel", "parallel"), nbytes),
        name="dilated_prefill",
    )(slopes, *([h] * 9))


def _dil_step_body(slopes_ref, *refs):
    new = refs[:9]
    caches = refs[9:12]
    o_ref = refs[13]
    scale = HD_C ** -0.5
    col = _iota((SEQ_PAD, BAND * G_C), 1)
    col_slot = col & (G_C - 1)
    steps = (BAND - (col >> (G_C.bit_length() - 1))).astype(F32)
    keys = [c[0].reshape(BAND * G_C, HD_C).astype(BF16) for c in caches]
    vals = [c[1].reshape(BAND * G_C, HD_C).astype(BF16) for c in caches]
    for slot in range(G_C):
        lanes = slice(slot * HD_C, (slot + 1) * HD_C)
        scores, self_scores = [], []
        for g, d in enumerate(DILATIONS):
            q_ref, k_ref, _ = new[3 * g:3 * g + 3]
            q = q_ref[:, lanes]
            s = _dot_nt(q.astype(BF16), keys[g]) * scale
            pen = slopes_ref[g * G_C + slot] * float(d)
            scores.append(jnp.where(col_slot == slot, s - pen * steps, NEG))
            self_scores.append(jnp.sum(q * k_ref[:, lanes], axis=-1, keepdims=True) * scale)
        m = self_scores[0]
        for x in self_scores[1:]:
            m = jnp.maximum(m, x)
        for x in scores:
            m = jnp.maximum(m, jnp.max(x, axis=-1, keepdims=True))
        acc = jnp.zeros((SEQ_PAD, HD_C), F32)
        total = jnp.zeros((SEQ_PAD, 1), F32)
        for g in range(N_GROUPS_C):
            p = jnp.exp(scores[g] - m)
            p_self = jnp.exp(self_scores[g] - m)
            total += jnp.sum(p, axis=-1, keepdims=True) + p_self
            acc += _dot(p.astype(BF16), vals[g]) + p_self * new[3 * g + 2][:, lanes]
        o_ref[:, lanes] = (acc / total).astype(o_ref.dtype)


def _dil_step(h, slopes, caches, layer, oc, *, n_seq, row0):
    rb0 = row0 // SEQ_PAD

    def cmap(g, j):
        return lambda b: (rb0 + b, (OFF_C + g * 3 * V_C + j * V_C) // V_C)

    in_specs = [pl.BlockSpec(memory_space=pltpu.SMEM)]
    in_specs += [pl.BlockSpec((SEQ_PAD, V_C), cmap(g, j)) for g in range(N_GROUPS_C) for j in range(3)]
    strided = []
    for c, d in zip(caches, DILATIONS):
        assert c.shape[3] == BAND * d
        strided.append(c.reshape(c.shape[0], c.shape[1], 2, BAND, d, G_C, HD_C))
        in_specs.append(pl.BlockSpec((None, None, 2, BAND, None, G_C, HD_C), lambda b: (layer, b, 0, 0, 0, 0, 0)))
    in_specs.append(pl.BlockSpec(memory_space=pl.ANY))
    nbytes = 2 * N_GROUPS_C * 2 * BAND * V_C * 4 + 64 * SEQ_PAD * BAND * G_C * 4
    return pl.pallas_call(
        _dil_step_body,
        out_shape=jax.ShapeDtypeStruct(oc.shape, oc.dtype),
        grid=(n_seq,),
        in_specs=in_specs,
        out_specs=pl.BlockSpec((SEQ_PAD, V_C), lambda b: (rb0 + b, 0)),
        input_output_aliases={13: 0},
        compiler_params=_params(("parallel",), nbytes),
        name="dilated_step",
    )(slopes, *([h] * 9), *strided, oc)


def _shift_body(c_ref, new_ref, o_ref):
    length = c_ref.shape[1]
    o_ref[:, 0:length - 1] = c_ref[:, 1:length]
    o_ref[:, length - 1:length] = new_ref[...]


def _shift_windows(cache, new_rows):
    depth, n, _, length = cache.shape[:4]
    blk = lambda rows: pl.BlockSpec((None, None, 2, rows, G_C, HD_C), lambda l, b: (l, b, 0, 0, 0, 0))
    return pl.pallas_call(
        _shift_body,
        out_shape=jax.ShapeDtypeStruct(cache.shape, cache.dtype),
        grid=(depth, n),
        in_specs=[blk(length), blk(1)],
        out_specs=blk(length),
        compiler_params=_params(("parallel", "parallel"), 4 * 2 * length * V_C * 4),
        name="shift_windows",
    )(cache, new_rows)


def _rope_tables(positions):
    half = DK_B // 2
    inv = ROPE_BASE ** (-np.arange(half, dtype=np.float64) / half)
    ang = np.asarray(positions, np.float64)[:, None] * inv[None, :]
    cos = np.concatenate([np.cos(ang), np.cos(ang)], axis=-1)
    sin = np.concatenate([-np.sin(ang), np.sin(ang)], axis=-1)
    return jnp.asarray(cos, F32), jnp.asarray(sin, F32)


def _layer(x, l, w, lw, state_gla, state_ret, caches, tabs, dims):
    n_p, seq, n_s, m_p, m = dims
    tm = _pick_tile(m, 704)
    tm_big = _pick_tile(m, 1100)
    xn = _rmsnorm(x, lw["g_mix"], BF16)
    h = _in_proj(xn, w["w_in_t"], l, tm=tm_big, tn=768)
    la, bsum = _gate(xn, w["w_in_t"], l, lw["w_alpha2"], lw["b_alpha"], tm=_pick_tile(m, 704, GLA_ROWS), block=GLA_ROWS)

    gla = functools.partial(_gla, h, la, g_gla=lw["g_gla"], out_rows=m)
    oa, sa_p = gla(bsum, n_seq=n_p, seq_rows=seq, rows=GLA_ROWS, row0=0, t_valid=None, state0=None, alias=None)
    oa, sa_s = gla(None, n_seq=n_s, seq_rows=SEQ_PAD, rows=SEQ_PAD, row0=m_p, t_valid=1, state0=state_gla, alias=oa)

    ret = functools.partial(_ret, h, out_rows=m)
    ob, sb_p = ret(tabs["cos_p"], tabs["sin_p"], tabs["lg"], n_seq=n_p, seq_rows=seq, rows=RET_ROWS, row0=0,
                   t_valid=None, state0=None, alias=None)
    ob, sb_s = ret(tabs["cos_s"], tabs["sin_s"], tabs["lg"], n_seq=n_s, seq_rows=SEQ_PAD, rows=SEQ_PAD, row0=m_p,
                   t_valid=1, state0=state_ret, alias=ob)

    oc = _dil_prefill(h, tabs["slopes"], n_seq=n_p, seq=seq, out_rows=m)
    oc = _dil_step(h, tabs["slopes"], caches, l, oc, n_seq=n_s, row0=m_p)

    gates = _matmul_ws(xn, w["w_merge"], l, tm=tm_big, tn=768, out_dtype=BF16, act=_sigmoid, chunks=ROW_CHUNKS,
                       name="merge_gates")
    merged = _merge_up(oa, ob, oc, gates, w["w_up_a"], w["w_up_b"], w["w_up_c"], l, tm=tm_big, tn=512)
    x = _matmul_ws(merged, w["w_out"], l, tm=tm_big, tn=512, residual=x, chunks=ROW_CHUNKS, name="out_proj")
    xf = _rmsnorm(x, lw["g_ffn"], BF16)
    hid = _ffn_up(xf, w["w_ffn_gate"], w["w_ffn_up"], l, tm=_pick_tile(m, 1728), tn=256)
    d_ff = hid.shape[1]
    x = _matmul_ksplit_res(hid, w["w_ffn_down"], l, x, tm=tm, tn=512, tk=d_ff // 2, name="ffn_down")

    win_p, new_s = [], []
    for g in range(N_GROUPS_C):
        off_k = OFF_C + g * 3 * V_C + V_C
        keep = min(BAND * DILATIONS[g], seq)
        kv_p = jnp.stack([lax.slice(h, ((b + 1) * seq - keep, off_k), ((b + 1) * seq, off_k + 2 * V_C))
                          for b in range(n_p)])
        win_p.append(kv_p.reshape(n_p, keep, 2, G_C, HD_C).transpose(0, 2, 1, 3, 4))
        kv_s = lax.slice(h, (m_p, off_k), (m, off_k + 2 * V_C)).reshape(n_s, SEQ_PAD, 2, G_C, HD_C)[:, :1]
        new_s.append(kv_s.transpose(0, 2, 1, 3, 4))
    return x, (sa_p, sa_s, sb_p, sb_s), win_p, new_s


def kernel(x_prompt, x_sample, state_gla, state_ret, cache_win0, cache_win1, cache_win2, g_mix, w_in, w_alpha2,
           b_alpha, g_gla, w_merge, w_up_a, w_up_b, w_up_c, w_out, g_ffn, w_ffn_gate, w_ffn_up, w_ffn_down, g_final):
    n_p, seq, d = x_prompt.shape
    n_s = x_sample.shape[0]
    depth = g_mix.shape[0]
    assert d == D_MODEL and x_sample.shape[1] == 1 and seq % (BAND * DILATIONS[-1]) == 0 and seq % RET_ROWS == 0
    m_p = n_p * seq
    m = m_p + n_s * SEQ_PAD
    dims = (n_p, seq, n_s, m_p, m)

    xs = jnp.pad(x_sample, ((0, 0), (0, SEQ_PAD - 1), (0, 0)))
    x = jnp.concatenate([x_prompt.reshape(m_p, d), xs.reshape(n_s * SEQ_PAD, d)], axis=0)

    cos_p, sin_p = _rope_tables(np.arange(seq))
    cos_s, sin_s = _rope_tables(PAST_LEN + np.arange(SEQ_PAD))
    log_gamma = np.log1p(-(2.0 ** (-5.0 - np.arange(H_B, dtype=np.float64))))
    heads = np.arange(1, N_GROUPS_C * G_C + 1, dtype=np.float64)
    tabs = {
        "cos_p": cos_p, "sin_p": sin_p, "cos_s": cos_s, "sin_s": sin_s,
        "lg": jnp.asarray(np.broadcast_to(log_gamma[:, None], (H_B, RET_ROWS)), F32),
        "slopes": jnp.asarray(2.0 ** (-8.0 * heads / (N_GROUPS_C * G_C)), F32),
    }

    assert w_in.shape[2] == N_MAIN + GATE_RANK
    w = {"w_in_t": jnp.swapaxes(w_in, 1, 2), "w_merge": w_merge, "w_up_a": w_up_a, "w_up_b": w_up_b, "w_up_c": w_up_c,
         "w_out": w_out, "w_ffn_gate": w_ffn_gate, "w_ffn_up": w_ffn_up, "w_ffn_down": w_ffn_down.astype(BF16)}
    caches = (cache_win0, cache_win1, cache_win2)
    outs = {k: [] for k in ("gla_p", "gla_s", "ret_p", "ret_s")}
    win_p = [[] for _ in range(N_GROUPS_C)]
    win_s = [[] for _ in range(N_GROUPS_C)]
    for l in range(depth):
        lw = {
            "g_mix": g_mix[l], "g_gla": g_gla[l], "g_ffn": g_ffn[l], "b_alpha": b_alpha[l],
            "w_alpha2": w_alpha2[l],
        }
        x, (sa_p, sa_s, sb_p, sb_s), wp, ws = _layer(x, l, w, lw, state_gla[l], state_ret[l], caches, tabs, dims)
        outs["gla_p"].append(sa_p)
        outs["gla_s"].append(sa_s)
        outs["ret_p"].append(sb_p)
        outs["ret_s"].append(sb_s)
        for g in range(N_GROUPS_C):
            win_p[g].append(wp[g])
            win_s[g].append(ws[g])

    y_prompt = _rmsnorm(x, g_final, F32, rows=m_p).reshape(n_p, seq, d)
    y_sample = _rmsnorm(x[m_p:], g_final, F32).reshape(n_s, SEQ_PAD, d)[:, :1]
    win_s = [_shift_windows(caches[g], jnp.stack(win_s[g])) for g in range(N_GROUPS_C)]
    return (y_prompt, y_sample, jnp.stack(outs["gla_p"]), jnp.stack(outs["gla_s"]), jnp.stack(outs["ret_p"]),
            jnp.stack(outs["ret_s"]), jnp.stack(win_p[0]), win_s[0], jnp.stack(win_p[1]), win_s[1],
            jnp.stack(win_p[2]), win_s[2])
```

```python
import functools

import numpy as np
import jax
import jax.numpy as jnp
from jax import lax
from jax.experimental import pallas as pl
from jax.experimental.pallas import tpu as pltpu

F32 = jnp.float32
BF16 = jnp.bfloat16
HIGHEST = lax.Precision.HIGHEST

D_MODEL = 4096
H_A, DK_A, DV_A = 4, 256, 512
GATE_RANK, GATE_TAU = 16, 16.0
H_B, DK_B, DV_B = 8, 128, 256
ROPE_BASE = 10000.0
DILATIONS = (1, 4, 16)
N_GROUPS_C, G_C, HD_C = 3, 4, 128
BAND = 128
PAST_LEN = 16384
QK_A, V_A = H_A * DK_A, H_A * DV_A
QK_B, V_B = H_B * DK_B, H_B * DV_B
V_C = G_C * HD_C
C_COLS = N_GROUPS_C * 3 * V_C
EPS = 1e-6
NEG = -1e30

OFF_QA, OFF_KA, OFF_VA, OFF_RA = 0, QK_A, 2 * QK_A, 2 * QK_A + V_A
OFF_B = 2 * QK_A + 2 * V_A
OFF_QB, OFF_KB, OFF_VB, OFF_GB = OFF_B, OFF_B + QK_B, OFF_B + 2 * QK_B, OFF_B + 2 * QK_B + V_B
OFF_C = OFF_B + 2 * QK_B + 2 * V_B
N_MAIN = OFF_C + C_COLS

LANES = 128
SEQ_PAD = 16
GLA_ROWS = 128
RET_ROWS = 256
RET_HEADS = 8
SUBLANES = 8
ROW_CHUNKS = 5
VMEM_CAP = 56 * 1024 * 1024


def _vmem(nbytes):
    return int(min(VMEM_CAP, nbytes + (6 << 20)))


def _params(sem, nbytes):
    return pltpu.CompilerParams(dimension_semantics=sem, vmem_limit_bytes=_vmem(nbytes))


def _pick_tile(m, cap, align=16):
    best = align
    for t in range(align, cap + 1, align):
        if m % t == 0:
            best = t
    assert m % best == 0, (m, cap, align)
    return best


def _iota(shape, dim):
    return lax.broadcasted_iota(jnp.int32, shape, dim)


def _dot(a, b):
    return jnp.dot(a, b, preferred_element_type=F32)


def _dot_nt(a, b):
    return lax.dot_general(a, b, (((1,), (1,)), ((), ())), preferred_element_type=F32)


def _dot_tn(a, b):
    return lax.dot_general(a, b, (((0,), (0,)), ((), ())), preferred_element_type=F32)


def _sigmoid(x):
    return 1.0 / (1.0 + jnp.exp(-x))


def _rmsnorm_body(x_ref, g_ref, o_ref):
    x = x_ref[...]
    ms = jnp.mean(x * x, axis=-1, keepdims=True)
    o_ref[...] = (x * lax.rsqrt(ms + EPS) * g_ref[...]).astype(o_ref.dtype)


def _rmsnorm(x, g, out_dtype, rows=None):
    m, d = x.shape
    m = m if rows is None else rows
    tm = _pick_tile(m, 512)
    nbytes = 2 * tm * d * (4 + jnp.dtype(out_dtype).itemsize) + 2 * tm * d * 4
    return pl.pallas_call(
        _rmsnorm_body,
        out_shape=jax.ShapeDtypeStruct((m, d), out_dtype),
        grid=(m // tm,),
        in_specs=[pl.BlockSpec((tm, d), lambda i: (i, 0)), pl.BlockSpec((1, d), lambda i: (0, 0))],
        out_specs=pl.BlockSpec((tm, d), lambda i: (i, 0)),
        compiler_params=_params(("parallel",), nbytes),
        name="rmsnorm",
    )(x, g.reshape(1, d))


def _row_chunks(tm, want):
    if want <= 1 or tm % (want * 16) != 0:
        return [slice(0, tm)]
    return [slice(c * (tm // want), (c + 1) * (tm // want)) for c in range(want)]


def _mm_ws_body(*refs, act, has_res, transposed, chunks):
    x_ref, w_ref = refs[:2]
    r_ref = refs[2] if has_res else None
    o_ref = refs[-1]
    w = w_ref[0].astype(BF16) if transposed else w_ref[...].astype(BF16)
    for rows in _row_chunks(x_ref.shape[0], chunks):
        y = _dot_nt(x_ref[rows, :], w) if transposed else _dot(x_ref[rows, :], w)
        if act is not None:
            y = act(y)
        if has_res:
            y = r_ref[rows, :] + y
        o_ref[rows, :] = y.astype(o_ref.dtype)


def _matmul_ws(x, w, layer, *, tm, tn, n=None, row_of=None, out_dtype=F32, act=None, residual=None, chunks=1,
               w_buffers=2, name="matmul"):
    m, k = x.shape
    transposed = row_of is not None
    n = w.shape[2] if n is None else n
    assert m % tm == 0 and n % tn == 0
    if transposed:
        w_spec = pl.BlockSpec((pl.Element(1), pl.Element(tn), pl.Element(k)),
                              lambda j, i: (layer, pl.multiple_of(row_of(j), SUBLANES), 0))
    else:
        w_spec = pl.BlockSpec((None, k, tn), lambda j, i: (layer, 0, j))
    if w_buffers == 1:
        assert not transposed
        w_spec = pl.BlockSpec((None, k, tn), lambda j, i: (layer, 0, j), pipeline_mode=pl.Buffered(1))
    in_specs = [pl.BlockSpec((tm, k), lambda j, i: (i, 0)), w_spec]
    args = [x, w]
    osz = jnp.dtype(out_dtype).itemsize
    nbytes = 2 * (tm * k * 2 + tm * tn * osz) + w_buffers * k * tn * 4 + k * tn * 2 + 2 * tm * tn * 4
    if residual is not None:
        in_specs.append(pl.BlockSpec((tm, tn), lambda j, i: (i, j)))
        args.append(residual)
        nbytes += 2 * tm * tn * 4
    return pl.pallas_call(
        functools.partial(_mm_ws_body, act=act, has_res=residual is not None, transposed=transposed, chunks=chunks),
        out_shape=jax.ShapeDtypeStruct((m, n), out_dtype),
        grid=(n // tn, m // tm),
        in_specs=in_specs,
        out_specs=pl.BlockSpec((tm, tn), lambda j, i: (i, j)),
        compiler_params=_params(("parallel", "parallel"), nbytes),
        name=name,
    )(*args)


def _in_proj(x, w_in_t, layer, *, tm, tn):
    assert OFF_B % tn == 0
    return _matmul_ws(x, w_in_t, layer, tm=tm, tn=tn, n=N_MAIN, name="in_proj",
                      row_of=lambda j: j * tn + jnp.where(j >= OFF_B // tn, GATE_RANK, 0))


def _mm_ksplit_res_body(x_ref, w_ref, r_ref, o_ref, acc_ref, *, tm, n_k):
    kk = pl.program_id(1)
    rows = pl.ds(pl.multiple_of(pl.program_id(2) * tm, tm), tm)
    part = _dot(x_ref[...], w_ref[...])

    @pl.when(kk == 0)
    def _():
        acc_ref[rows, :] = r_ref[...] + part

    @pl.when(jnp.logical_and(kk > 0, kk < n_k - 1))
    def _():
        acc_ref[rows, :] += part

    @pl.when(kk == n_k - 1)
    def _():
        o_ref[...] = acc_ref[rows, :] + part


def _matmul_ksplit_res(x, w, layer, residual, *, tm, tn, tk, name):
    m, k = x.shape
    n = w.shape[2]
    assert m % tm == 0 and n % tn == 0 and k % tk == 0 and k // tk >= 2
    n_k, n_i = k // tk, m // tm
    nbytes = 2 * (tm * tk * 2 + tk * tn * 2 + 2 * tm * tn * 4) + m * tn * 4 + 2 * tm * tn * 4
    return pl.pallas_call(
        functools.partial(_mm_ksplit_res_body, tm=tm, n_k=n_k),
        out_shape=jax.ShapeDtypeStruct((m, n), F32),
        grid=(n // tn, n_k, n_i),
        in_specs=[
            pl.BlockSpec((tm, tk), lambda j, kk, i: (i, kk)),
            pl.BlockSpec((None, tk, tn), lambda j, kk, i: (layer, kk, j)),
            pl.BlockSpec((tm, tn), lambda j, kk, i: (jnp.where(kk == 0, i, n_i - 1), j)),
        ],
        out_specs=pl.BlockSpec((tm, tn), lambda j, kk, i: (jnp.where(kk == n_k - 1, i, 0), j)),
        scratch_shapes=[pltpu.VMEM((m, tn), F32)],
        compiler_params=_params(("parallel", "arbitrary", "arbitrary"), nbytes),
        name=name,
    )(x, w, residual)


def _merge_up_body(oa_ref, ob_ref, oc_ref, ga_ref, gb_ref, gc_ref, wa_ref, wb_ref, wc_ref, o_ref):
    wa, wb, wc = wa_ref[...].astype(BF16), wb_ref[...].astype(BF16), wc_ref[...].astype(BF16)
    for rows in _row_chunks(oa_ref.shape[0], ROW_CHUNKS):
        acc = ga_ref[rows, :].astype(F32) * _dot(oa_ref[rows, :], wa)
        acc += gb_ref[rows, :].astype(F32) * _dot(ob_ref[rows, :], wb)
        acc += gc_ref[rows, :].astype(F32) * _dot(oc_ref[rows, :], wc)
        o_ref[rows, :] = acc.astype(o_ref.dtype)


def _merge_up(oa, ob, oc, gates, w_up_a, w_up_b, w_up_c, layer, *, tm, tn):
    m = oa.shape[0]
    d = w_up_a.shape[2]
    nt = d // tn
    assert m % tm == 0 and d % tn == 0

    def row(width):
        return pl.BlockSpec((tm, width), lambda j, i: (i, 0))

    def gate(branch):
        return pl.BlockSpec((tm, tn), lambda j, i: (i, branch * nt + j))

    def panel(kdim):
        return pl.BlockSpec((None, kdim, tn), lambda j, i: (layer, 0, j))

    kk = V_A + V_B + V_C
    nbytes = 2 * (tm * kk * 2 + 3 * tm * tn * 2 + kk * tn * 4 + tm * tn * 2) + kk * tn * 2 + 6 * tm * tn * 4
    return pl.pallas_call(
        _merge_up_body,
        out_shape=jax.ShapeDtypeStruct((m, d), BF16),
        grid=(nt, m // tm),
        in_specs=[row(V_A), row(V_B), row(V_C), gate(0), gate(1), gate(2), panel(V_A), panel(V_B), panel(V_C)],
        out_specs=pl.BlockSpec((tm, tn), lambda j, i: (i, j)),
        compiler_params=_params(("parallel", "parallel"), nbytes),
        name="merge_up",
    )(oa, ob, oc, gates, gates, gates, w_up_a, w_up_b, w_up_c)


def _ffn_up_body(x_ref, wg_ref, wu_ref, o_ref):
    wg = wg_ref[...].astype(BF16)
    wu = wu_ref[...].astype(BF16)
    half = x_ref.shape[0] // 2
    for rows in (slice(0, half), slice(half, 2 * half)):
        x = x_ref[rows, :]
        a = _dot(x, wg)
        o_ref[rows, :] = (a * _sigmoid(a) * _dot(x, wu)).astype(o_ref.dtype)


def _ffn_up(x, wg, wu, layer, *, tm, tn):
    m, k = x.shape
    n = wg.shape[2]
    assert m % tm == 0 and n % tn == 0
    nbytes = 2 * (tm * k * 2 + 2 * k * tn * 4 + tm * tn * 2) + 2 * k * tn * 2 + 3 * tm * tn * 4
    return pl.pallas_call(
        _ffn_up_body,
        out_shape=jax.ShapeDtypeStruct((m, n), BF16),
        grid=(m // tm, n // tn),
        in_specs=[pl.BlockSpec((tm, k), lambda i, j: (i, 0)), pl.BlockSpec((None, k, tn), lambda i, j: (layer, 0, j)),
                  pl.BlockSpec((None, k, tn), lambda i, j: (layer, 0, j))],
        out_specs=pl.BlockSpec((tm, tn), lambda i, j: (i, j)),
        compiler_params=_params(("parallel", "parallel"), nbytes),
        name="ffn_up",
    )(x, wg, wu)


def _gla_body(*refs, rows, n_t, t_valid, heads, has_bsum, has_s0, has_alias):
    it = iter(refs)
    q_ref, k_ref, v_ref, r_ref, la_ref, g_ref = [next(it) for _ in range(6)]
    bsum_ref = next(it) if has_bsum else None
    s0_ref = next(it) if has_s0 else None
    if has_alias:
        next(it)
    o_ref, sout_ref, st_scr, b_scr = [next(it) for _ in range(4)]
    t = pl.program_id(1)

    @pl.when(t == 0)
    def _():
        for hh in range(heads):
            st_scr[hh] = s0_ref[hh].T if has_s0 else jnp.zeros(st_scr.shape[1:], F32)

    for hh in range(heads):
        kk = slice(hh * DK_A, (hh + 1) * DK_A)
        vv = slice(hh * DV_A, (hh + 1) * DV_A)
        _gla_head(q_ref.at[:, kk], k_ref.at[:, kk], v_ref.at[:, vv], r_ref.at[:, vv], la_ref.at[:, kk], g_ref,
                  None if bsum_ref is None else bsum_ref.at[:, kk], o_ref.at[:, vv], st_scr.at[hh], b_scr.at[hh],
                  rows=rows, t_valid=t_valid)

    @pl.when(t == n_t - 1)
    def _():
        for hh in range(heads):
            sout_ref[hh] = st_scr[hh].T


def _gla_head(q_ref, k_ref, v_ref, r_ref, la_ref, g_ref, bsum_ref, o_ref, st_scr, b_scr, *, rows, t_valid):
    la = la_ref[...]
    row_k = _iota((rows, DK_A), 0)
    if t_valid is not None:
        la = jnp.where(row_k < t_valid, la, 0.0)
    if bsum_ref is not None:
        b = bsum_ref[...]
    else:
        tri = (_iota((rows, rows), 0) >= _iota((rows, rows), 1)).astype(F32)
        b = jnp.dot(tri, la, precision=HIGHEST, preferred_element_type=F32)
    b_scr[...] = b

    q = q_ref[...] * (DK_A ** -0.5)
    k = k_ref[...]
    v16 = v_ref[...].astype(BF16)
    st = st_scr[...]
    o = _dot_nt((q * jnp.exp(b)).astype(BF16), st.astype(BF16))

    t_idx = _iota((rows, rows), 0)
    s_idx = _iota((rows, rows), 1)
    scores = jnp.where(t_idx == s_idx, _dot_nt(q.astype(BF16), k.astype(BF16)), 0.0)
    m = rows // 2
    while m >= 1:
        sub = row_k & (2 * m - 1)
        if m >= 4:
            pieces = [jnp.broadcast_to(b_scr[pl.ds(p * 2 * m + m - 1, 1), :], (2 * m, DK_A)) for p in range(rows // (2 * m))]
            log_fac = -jnp.abs(b - (pieces[0] if len(pieces) == 1 else jnp.concatenate(pieces, axis=0)))
        elif m == 2:
            la_next = pltpu.roll(la, rows - 1, 0)
            la_prev = pltpu.roll(la, 1, 0)
            log_fac = jnp.where(sub == 0, la_next, jnp.where(sub == 1, 0.0, jnp.where(sub == 2, la, la + la_prev)))
        else:
            log_fac = jnp.where(sub == 1, la, 0.0)
        fac = jnp.exp(log_fac)
        upper = sub >= m
        qs = jnp.where(upper, q * fac, 0.0)
        ks = jnp.where(upper, 0.0, k * fac)
        part = _dot_nt(qs.astype(BF16), ks.astype(BF16))
        if 2 * m < rows:
            shift = (2 * m).bit_length() - 1
            part = jnp.where((t_idx >> shift) == (s_idx >> shift), part, 0.0)
        scores = scores + part
        m //= 2
    o = o + _dot(scores.astype(BF16), v16)

    b_last = b_scr[pl.ds(rows - 1, 1), :]
    k_st = k if t_valid is None else jnp.where(row_k < t_valid, k, 0.0)
    st_new = st * jnp.exp(b_last) + _dot_tn(v16, (k_st * jnp.exp(b_last - b)).astype(BF16))
    st_scr[...] = st_new

    ms = jnp.mean(o * o, axis=-1, keepdims=True)
    r = r_ref[...]
    o_ref[...] = (r * _sigmoid(r) * (o * lax.rsqrt(ms + EPS) * g_ref[...])).astype(o_ref.dtype)


def _gla(h, la, bsum, g_gla, *, n_seq, seq_rows, rows, row0, t_valid, state0, out_rows, alias, heads=H_A):
    n_t = seq_rows // rows
    rb0 = row0 // rows
    groups = H_A // heads
    wk, wv = heads * DK_A, heads * DV_A

    def rmap(blk0):
        return lambda s, t: (rb0 + (s // groups) * n_t + t, blk0 + s % groups)

    smap = lambda s, t: (s // groups, s % groups, 0, 0)
    in_specs = [
        pl.BlockSpec((rows, wk), rmap(OFF_QA // wk)),
        pl.BlockSpec((rows, wk), rmap(OFF_KA // wk)),
        pl.BlockSpec((rows, wv), rmap(OFF_VA // wv)),
        pl.BlockSpec((rows, wv), rmap(OFF_RA // wv)),
        pl.BlockSpec((rows, wk), rmap(0)),
        pl.BlockSpec((1, DV_A), lambda s, t: (0, 0)),
    ]
    args = [h, h, h, h, la, g_gla.reshape(1, DV_A)]
    if bsum is not None:
        in_specs.append(pl.BlockSpec((rows, wk), rmap(0)))
        args.append(bsum)
    if state0 is not None:
        in_specs.append(pl.BlockSpec((None, heads, DK_A, DV_A), smap))
        args.append(state0)
    aliases = {}
    if alias is not None:
        in_specs.append(pl.BlockSpec(memory_space=pl.ANY))
        aliases = {len(args): 0}
        args.append(alias)
    nbytes = heads * (2 * rows * (4 * DK_A + 2 * DV_A) * 4 + 2 * rows * DV_A * 2 + 5 * DK_A * DV_A * 4
                      + rows * DK_A * 4 + 16 * rows * DV_A * 4)
    return pl.pallas_call(
        functools.partial(_gla_body, rows=rows, n_t=n_t, t_valid=t_valid, heads=heads, has_bsum=bsum is not None,
                          has_s0=state0 is not None, has_alias=alias is not None),
        out_shape=(jax.ShapeDtypeStruct((out_rows, V_A), BF16), jax.ShapeDtypeStruct((n_seq, H_A, DK_A, DV_A), F32)),
        grid=(n_seq * groups, n_t),
        in_specs=in_specs,
        out_specs=(pl.BlockSpec((rows, wv), rmap(0)), pl.BlockSpec((None, heads, DK_A, DV_A), smap)),
        scratch_shapes=[pltpu.VMEM((heads, DV_A, DK_A), F32), pltpu.VMEM((heads, rows, DK_A), F32)],
        input_output_aliases=aliases,
        compiler_params=_params(("parallel", "arbitrary"), nbytes),
        name="gla",
    )(*args)


def _gate_body(x_ref, w_ref, wa_hi_ref, wa_lo_ref, ba_ref, la_ref, bsum_ref, *, block):
    za = _dot_nt(x_ref[...], w_ref[0].astype(BF16))
    za_hi = za.astype(BF16)
    za_lo = (za - za_hi.astype(F32)).astype(BF16)
    wa_hi = wa_hi_ref[...]
    z = _dot(za_hi, wa_hi) + (_dot(za_lo, wa_hi) + _dot(za_hi, wa_lo_ref[...])) + ba_ref[...]
    la = (jnp.minimum(z, 0.0) - jnp.log(1.0 + jnp.exp(-jnp.abs(z)))) * (1.0 / GATE_TAU)
    la_ref[...] = la
    tri = (_iota((block, block), 0) >= _iota((block, block), 1)).astype(BF16)
    la_hi = la.astype(BF16)
    rest = la - la_hi.astype(F32)
    la_mid = rest.astype(BF16)
    la_lo = (rest - la_mid.astype(F32)).astype(BF16)
    for p in range(x_ref.shape[0] // block):
        rows = slice(p * block, (p + 1) * block)
        bsum_ref[rows, :] = _dot(tri, la_hi[rows]) + (_dot(tri, la_mid[rows]) + _dot(tri, la_lo[rows]))


def _gate(x, w_in_t, layer, w_alpha2, b_alpha, *, tm, block):
    m, k = x.shape
    assert m % tm == 0 and tm % block == 0 and OFF_B % SUBLANES == 0
    wa = jnp.pad(w_alpha2, ((0, LANES - GATE_RANK), (0, 0)))
    wa_hi = wa.astype(BF16)
    wa_lo = (wa - wa_hi.astype(F32)).astype(BF16)
    full = lambda shape: pl.BlockSpec(shape, lambda i: (0, 0))
    nbytes = 2 * (tm * k * 2 + k * LANES * 4 + 2 * tm * QK_A * 4) + k * LANES * 2 + 8 * tm * QK_A * 4
    return pl.pallas_call(
        functools.partial(_gate_body, block=block),
        out_shape=(jax.ShapeDtypeStruct((m, QK_A), F32), jax.ShapeDtypeStruct((m, QK_A), F32)),
        grid=(m // tm,),
        in_specs=[pl.BlockSpec((tm, k), lambda i: (i, 0)),
                  pl.BlockSpec((pl.Element(1), pl.Element(LANES), pl.Element(k)), lambda i: (layer, OFF_B, 0)),
                  full((LANES, QK_A)), full((LANES, QK_A)), full((1, QK_A))],
        out_specs=(pl.BlockSpec((tm, QK_A), lambda i: (i, 0)), pl.BlockSpec((tm, QK_A), lambda i: (i, 0))),
        compiler_params=_params(("parallel",), nbytes),
        name="gla_gate",
    )(x, w_in_t, wa_hi, wa_lo, b_alpha.reshape(1, QK_A))


def _ret_body(*refs, rows, n_t, t_valid, heads, has_s0, has_alias):
    it = iter(refs)
    q_ref, k_ref, v_ref, g_ref, cos_ref, sin_ref, lg_ref = [next(it) for _ in range(7)]
    s0_ref = next(it) if has_s0 else None
    if has_alias:
        next(it)
    o_ref, sout_ref, st_scr, rel_scr, dq_scr, dk_scr = [next(it) for _ in range(6)]
    t = pl.program_id(1)
    head0 = (pl.program_id(0) % (H_B // heads)) * heads
    n_valid = rows if t_valid is None else t_valid
    row_k = _iota((rows, DK_B), 0)

    @pl.when(t == 0)
    def _():
        steps = jnp.minimum(row_k + 1, n_valid).astype(F32)
        st = jnp.minimum(_iota((rows, rows), 0) + 1, n_valid)
        ss = jnp.minimum(_iota((rows, rows), 1) + 1, n_valid)
        causal = _iota((rows, rows), 0) >= _iota((rows, rows), 1)
        for hh in range(heads):
            st_scr[hh] = s0_ref[hh] if has_s0 else jnp.zeros(st_scr.shape[1:], F32)
            lg = lg_ref[pl.ds(head0 + hh, 1), :]
            dq_scr[hh] = jnp.exp(lg[:, :DK_B] * steps)
            dk_scr[hh] = jnp.exp(lg[:, :DK_B] * (n_valid - steps))
            rel_scr[hh] = jnp.exp(jnp.where(causal, lg[:, :rows] * (st - ss).astype(F32), NEG))

    for hh in range(heads):
        kk = slice(hh * DK_B, (hh + 1) * DK_B)
        vv = slice(hh * DV_B, (hh + 1) * DV_B)
        _ret_head(q_ref.at[:, kk], k_ref.at[:, kk], v_ref.at[:, vv], g_ref.at[:, vv], cos_ref, sin_ref,
                  lg_ref.at[pl.ds(head0 + hh, 1), :], o_ref.at[:, vv], st_scr.at[hh], rel_scr.at[hh], dq_scr.at[hh],
                  dk_scr.at[hh], rows=rows, t_valid=t_valid)

    @pl.when(t == n_t - 1)
    def _():
        for hh in range(heads):
            sout_ref[hh] = st_scr[hh]


def _ret_head(q_ref, k_ref, v_ref, g_ref, cos_ref, sin_ref, lg_ref, o_ref, st_scr, rel_scr, dq_scr, dk_scr, *,
              rows, t_valid):
    lg = lg_ref[...]
    n_valid = rows if t_valid is None else t_valid
    row_k = _iota((rows, DK_B), 0)
    cos = cos_ref[...]
    sin = sin_ref[...]

    def rot(x):
        return x * cos + pltpu.roll(x, DK_B // 2, 1) * sin

    q = rot(q_ref[...])
    k = rot(k_ref[...]) * (DK_B ** -0.5)
    if t_valid is not None:
        k = jnp.where(row_k < t_valid, k, 0.0)
    decay_q = dq_scr[...]
    decay_k = dk_scr[...]

    v16 = v_ref[...].astype(BF16)
    state = st_scr[...]
    scores = _dot_nt(q.astype(BF16), k.astype(BF16)) * rel_scr[...]
    o = _dot(scores.astype(BF16), v16) + _dot((q * decay_q).astype(BF16), state.astype(BF16))
    state_new = state * jnp.exp(lg[:, :DV_B] * float(n_valid)) + _dot_tn((k * decay_k).astype(BF16), v16)
    st_scr[...] = state_new

    ms = jnp.mean(o * o, axis=-1, keepdims=True)
    g = g_ref[...]
    o_ref[...] = (g * _sigmoid(g) * (o * lax.rsqrt(ms + EPS))).astype(o_ref.dtype)


def _ret(h, cos, sin, lg_tab, *, n_seq, seq_rows, rows, row0, t_valid, state0, out_rows, alias, heads=RET_HEADS):
    n_t = seq_rows // rows
    rb0 = row0 // rows
    groups = H_B // heads
    wk, wv = heads * DK_B, heads * DV_B

    def rmap(blk0):
        return lambda s, t: (rb0 + (s // groups) * n_t + t, blk0 + s % groups)

    smap = lambda s, t: (s // groups, s % groups, 0, 0)
    in_specs = [
        pl.BlockSpec((rows, wk), rmap(OFF_QB // wk)),
        pl.BlockSpec((rows, wk), rmap(OFF_KB // wk)),
        pl.BlockSpec((rows, wv), rmap(OFF_VB // wv)),
        pl.BlockSpec((rows, wv), rmap(OFF_GB // wv)),
        pl.BlockSpec((rows, DK_B), lambda s, t: (t, 0)),
        pl.BlockSpec((rows, DK_B), lambda s, t: (t, 0)),
        pl.BlockSpec((H_B, RET_ROWS), lambda s, t: (0, 0)),
    ]
    args = [h, h, h, h, cos, sin, lg_tab]
    if state0 is not None:
        in_specs.append(pl.BlockSpec((None, heads, DK_B, DV_B), smap))
        args.append(state0)
    aliases = {}
    if alias is not None:
        in_specs.append(pl.BlockSpec(memory_space=pl.ANY))
        aliases = {len(args): 0}
        args.append(alias)
    nbytes = heads * (2 * rows * (2 * DK_B + 2 * DV_B) * 4 + 2 * rows * DV_B * 2 + 5 * DK_B * DV_B * 4
                      + 6 * rows * rows * 4 + 8 * rows * DV_B * 4) + 4 * rows * DK_B * 4
    return pl.pallas_call(
        functools.partial(_ret_body, rows=rows, n_t=n_t, t_valid=t_valid, heads=heads, has_s0=state0 is not None,
                          has_alias=alias is not None),
        out_shape=(jax.ShapeDtypeStruct((out_rows, V_B), BF16), jax.ShapeDtypeStruct((n_seq, H_B, DK_B, DV_B), F32)),
        grid=(n_seq * groups, n_t),
        in_specs=in_specs,
        out_specs=(pl.BlockSpec((rows, wv), rmap(0)), pl.BlockSpec((None, heads, DK_B, DV_B), smap)),
        scratch_shapes=[pltpu.VMEM((heads, DK_B, DV_B), F32), pltpu.VMEM((heads, rows, rows), F32),
                        pltpu.VMEM((heads, rows, DK_B), F32), pltpu.VMEM((heads, rows, DK_B), F32)],
        input_output_aliases=aliases,
        compiler_params=_params(("parallel", "arbitrary"), nbytes),
        name="retention",
    )(*args)


def _dil_prefill_body(slopes_ref, *refs, seq):
    qkv = refs[:9]
    o_ref, og_scr, lse_scr = refs[9:]
    slot = pl.program_id(1)
    scale = HD_C ** -0.5
    rr = _iota((BAND, BAND), 0)
    cc = _iota((BAND, BAND), 1)
    dist_cur = (rr - cc).astype(F32)
    ok_cur = rr >= cc
    ok_prev = cc >= rr
    n_units = seq // BAND
    for g, d in enumerate(DILATIONS):
        q_ref, k_ref, v_ref = qkv[3 * g:3 * g + 3]
        pen = slopes_ref[g * G_C + slot] * float(d)
        shift = d.bit_length() - 1

        def rows_of(start, d=d):
            return pl.ds(start, BAND) if d == 1 else pl.ds(start, BAND, stride=d)

        has_prev = seq // (BAND * d) > 1

        def unit(u, carry, q_ref=q_ref, k_ref=k_ref, v_ref=v_ref, pen=pen, d=d, shift=shift, rows_of=rows_of, g=g,
                 has_prev=has_prev):
            blk = u >> shift
            start = (u & (d - 1)) + blk * (BAND * d)
            q = q_ref[rows_of(start), :].astype(BF16)
            keys = k_ref[rows_of(start), :].astype(BF16)
            vals = v_ref[rows_of(start), :].astype(BF16)
            if has_prev:
                prev = jnp.maximum(start - BAND * d, 0)
                keys = jnp.concatenate([k_ref[rows_of(prev), :].astype(BF16), keys], axis=0)
                vals = jnp.concatenate([v_ref[rows_of(prev), :].astype(BF16), vals], axis=0)
                bias = jnp.concatenate([jnp.where(ok_prev, -pen * (dist_cur + float(BAND)), NEG)
                                        + jnp.where(blk > 0, 0.0, NEG),
                                        jnp.where(ok_cur, -pen * dist_cur, NEG)], axis=1)
            else:
                bias = jnp.where(ok_cur, -pen * dist_cur, NEG)
            s = _dot_nt(q, keys) * scale + bias
            m = jnp.max(s, axis=-1, keepdims=True)
            p = jnp.exp(s - m)
            l = jnp.sum(p, axis=-1, keepdims=True)
            acc = _dot(p.astype(BF16), vals)
            og_scr[g, rows_of(start), :] = acc / l
            lse_scr[g, rows_of(start), :] = jnp.broadcast_to(m + jnp.log(l), (BAND, HD_C))
            return carry

        lax.fori_loop(0, n_units, unit, 0, unroll=True)

    lse = [lse_scr[g] for g in range(N_GROUPS_C)]
    top = jnp.maximum(jnp.maximum(lse[0], lse[1]), lse[2])
    w = [jnp.exp(x - top) for x in lse]
    total = w[0] + w[1] + w[2]
    o_ref[...] = ((w[0] * og_scr[0] + w[1] * og_scr[1] + w[2] * og_scr[2]) / total).astype(o_ref.dtype)


def _dil_prefill(h, slopes, *, n_seq, seq, out_rows):
    def cmap(g, j):
        return lambda b, s: (b, (OFF_C + g * 3 * V_C + j * V_C) // HD_C + s)

    in_specs = [pl.BlockSpec(memory_space=pltpu.SMEM)]
    in_specs += [pl.BlockSpec((seq, HD_C), cmap(g, j)) for g in range(N_GROUPS_C) for j in range(3)]
    nbytes = 2 * 9 * seq * HD_C * 4 + 2 * seq * HD_C * 2 + 2 * N_GROUPS_C * seq * HD_C * 4 + 8 * seq * HD_C * 4
    return pl.pallas_call(
        functools.partial(_dil_prefill_body, seq=seq),
        out_shape=jax.ShapeDtypeStruct((out_rows, V_C), BF16),
        grid=(n_seq, G_C),
        in_specs=in_specs,
        out_specs=pl.BlockSpec((seq, HD_C), lambda b, s: (b, s)),
        scratch_shapes=[pltpu.VMEM((N_GROUPS_C, seq, HD_C), F32), pltpu.VMEM((N_GROUPS_C, seq, HD_C), F32)],
        compiler_params=_params(("parallel", "parallel"), nbytes),
        name="dilated_prefill",
    )(slopes, *([h] * 9))


def _dil_step_body(slopes_ref, *refs):
    new = refs[:9]
    caches = refs[9:12]
    o_ref = refs[13]
    scale = HD_C ** -0.5
    col = _iota((SEQ_PAD, BAND * G_C), 1)
    col_slot = col & (G_C - 1)
    steps = (BAND - (col >> (G_C.bit_length() - 1))).astype(F32)
    keys = [c[0].reshape(BAND * G_C, HD_C).astype(BF16) for c in caches]
    vals = [c[1].reshape(BAND * G_C, HD_C).astype(BF16) for c in caches]
    for slot in range(G_C):
        lanes = slice(slot * HD_C, (slot + 1) * HD_C)
        scores, self_scores = [], []
        for g, d in enumerate(DILATIONS):
            q_ref, k_ref, _ = new[3 * g:3 * g + 3]
            q = q_ref[:, lanes]
            s = _dot_nt(q.astype(BF16), keys[g]) * scale
            pen = slopes_ref[g * G_C + slot] * float(d)
            scores.append(jnp.where(col_slot == slot, s - pen * steps, NEG))
            self_scores.append(jnp.sum(q * k_ref[:, lanes], axis=-1, keepdims=True) * scale)
        m = self_scores[0]
        for x in self_scores[1:]:
            m = jnp.maximum(m, x)
        for x in scores:
            m = jnp.maximum(m, jnp.max(x, axis=-1, keepdims=True))
        acc = jnp.zeros((SEQ_PAD, HD_C), F32)
        total = jnp.zeros((SEQ_PAD, 1), F32)
        for g in range(N_GROUPS_C):
            p = jnp.exp(scores[g] - m)
            p_self = jnp.exp(self_scores[g] - m)
            total += jnp.sum(p, axis=-1, keepdims=True) + p_self
            acc += _dot(p.astype(BF16), vals[g]) + p_self * new[3 * g + 2][:, lanes]
        o_ref[:, lanes] = (acc / total).astype(o_ref.dtype)


def _dil_step(h, slopes, caches, layer, oc, *, n_seq, row0):
    rb0 = row0 // SEQ_PAD

    def cmap(g, j):
        return lambda b: (rb0 + b, (OFF_C + g * 3 * V_C + j * V_C) // V_C)

    in_specs = [pl.BlockSpec(memory_space=pltpu.SMEM)]
    in_specs += [pl.BlockSpec((SEQ_PAD, V_C), cmap(g, j)) for g in range(N_GROUPS_C) for j in range(3)]
    strided = []
    for c, d in zip(caches, DILATIONS):
        assert c.shape[3] == BAND * d
        strided.append(c.reshape(c.shape[0], c.shape[1], 2, BAND, d, G_C, HD_C))
        in_specs.append(pl.BlockSpec((None, None, 2, BAND, None, G_C, HD_C), lambda b: (layer, b, 0, 0, 0, 0, 0)))
    in_specs.append(pl.BlockSpec(memory_space=pl.ANY))
    nbytes = 2 * N_GROUPS_C * 2 * BAND * V_C * 4 + 64 * SEQ_PAD * BAND * G_C * 4
    return pl.pallas_call(
        _dil_step_body,
        out_shape=jax.ShapeDtypeStruct(oc.shape, oc.dtype),
        grid=(n_seq,),
        in_specs=in_specs,
        out_specs=pl.BlockSpec((SEQ_PAD, V_C), lambda b: (rb0 + b, 0)),
        input_output_aliases={13: 0},
        compiler_params=_params(("parallel",), nbytes),
        name="dilated_step",
    )(slopes, *([h] * 9), *strided, oc)


def _shift_body(c_ref, new_ref, o_ref):
    length = c_ref.shape[1]
    o_ref[:, 0:length - 1] = c_ref[:, 1:length]
    o_ref[:, length - 1:length] = new_ref[...]


def _shift_windows(cache, new_rows):
    depth, n, _, length = cache.shape[:4]
    blk = lambda rows: pl.BlockSpec((None, None, 2, rows, G_C, HD_C), lambda l, b: (l, b, 0, 0, 0, 0))
    return pl.pallas_call(
        _shift_body,
        out_shape=jax.ShapeDtypeStruct(cache.shape, cache.dtype),
        grid=(depth, n),
        in_specs=[blk(length), blk(1)],
        out_specs=blk(length),
        compiler_params=_params(("parallel", "parallel"), 4 * 2 * length * V_C * 4),
        name="shift_windows",
    )(cache, new_rows)


def _rope_tables(positions):
    half = DK_B // 2
    inv = ROPE_BASE ** (-np.arange(half, dtype=np.float64) / half)
    ang = np.asarray(positions, np.float64)[:, None] * inv[None, :]
    cos = np.concatenate([np.cos(ang), np.cos(ang)], axis=-1)
    sin = np.concatenate([-np.sin(ang), np.sin(ang)], axis=-1)
    return jnp.asarray(cos, F32), jnp.asarray(sin, F32)


def _layer(x, l, w, lw, state_gla, state_ret, caches, tabs, dims):
    n_p, seq, n_s, m_p, m = dims
    tm = _pick_tile(m, 704)
    tm_big = _pick_tile(m, 1100)
    xn = _rmsnorm(x, lw["g_mix"], BF16)
    h = _in_proj(xn, w["w_in_t"], l, tm=tm_big, tn=768)
    la, bsum = _gate(xn, w["w_in_t"], l, lw["w_alpha2"], lw["b_alpha"], tm=_pick_tile(m, 704, GLA_ROWS), block=GLA_ROWS)

    gla = functools.partial(_gla, h, la, g_gla=lw["g_gla"], out_rows=m)
    oa, sa_p = gla(bsum, n_seq=n_p, seq_rows=seq, rows=GLA_ROWS, row0=0, t_valid=None, state0=None, alias=None)
    oa, sa_s = gla(None, n_seq=n_s, seq_rows=SEQ_PAD, rows=SEQ_PAD, row0=m_p, t_valid=1, state0=state_gla, alias=oa)

    ret = functools.partial(_ret, h, out_rows=m)
    ob, sb_p = ret(tabs["cos_p"], tabs["sin_p"], tabs["lg"], n_seq=n_p, seq_rows=seq, rows=RET_ROWS, row0=0,
                   t_valid=None, state0=None, alias=None)
    ob, sb_s = ret(tabs["cos_s"], tabs["sin_s"], tabs["lg"], n_seq=n_s, seq_rows=SEQ_PAD, rows=SEQ_PAD, row0=m_p,
                   t_valid=1, state0=state_ret, alias=ob)

    oc = _dil_prefill(h, tabs["slopes"], n_seq=n_p, seq=seq, out_rows=m)
    oc = _dil_step(h, tabs["slopes"], caches, l, oc, n_seq=n_s, row0=m_p)

    gates = _matmul_ws(xn, w["w_merge"], l, tm=tm_big, tn=768, out_dtype=BF16, act=_sigmoid, chunks=ROW_CHUNKS,
                       name="merge_gates")
    merged = _merge_up(oa, ob, oc, gates, w["w_up_a"], w["w_up_b"], w["w_up_c"], l, tm=tm_big, tn=512)
    x = _matmul_ws(merged, w["w_out"], l, tm=tm, tn=1024, residual=x, chunks=ROW_CHUNKS, w_buffers=1,
                   name="out_proj")
    xf = _rmsnorm(x, lw["g_ffn"], BF16)
    hid = _ffn_up(xf, w["w_ffn_gate"], w["w_ffn_up"], l, tm=_pick_tile(m, 1728), tn=256)
    d_ff = hid.shape[1]
    x = _matmul_ksplit_res(hid, w["w_ffn_down"], l, x, tm=tm, tn=512, tk=d_ff // 2, name="ffn_down")

    win_p, new_s = [], []
    for g in range(N_GROUPS_C):
        off_k = OFF_C + g * 3 * V_C + V_C
        keep = min(BAND * DILATIONS[g], seq)
        kv_p = jnp.stack([lax.slice(h, ((b + 1) * seq - keep, off_k), ((b + 1) * seq, off_k + 2 * V_C))
                          for b in range(n_p)])
        win_p.append(kv_p.reshape(n_p, keep, 2, G_C, HD_C).transpose(0, 2, 1, 3, 4))
        kv_s = lax.slice(h, (m_p, off_k), (m, off_k + 2 * V_C)).reshape(n_s, SEQ_PAD, 2, G_C, HD_C)[:, :1]
        new_s.append(kv_s.transpose(0, 2, 1, 3, 4))
    return x, (sa_p, sa_s, sb_p, sb_s), win_p, new_s


def kernel(x_prompt, x_sample, state_gla, state_ret, cache_win0, cache_win1, cache_win2, g_mix, w_in, w_alpha2,
           b_alpha, g_gla, w_merge, w_up_a, w_up_b, w_up_c, w_out, g_ffn, w_ffn_gate, w_ffn_up, w_ffn_down, g_final):
    n_p, seq, d = x_prompt.shape
    n_s = x_sample.shape[0]
    depth = g_mix.shape[0]
    assert d == D_MODEL and x_sample.shape[1] == 1 and seq % (BAND * DILATIONS[-1]) == 0 and seq % RET_ROWS == 0
    m_p = n_p * seq
    m = m_p + n_s * SEQ_PAD
    dims = (n_p, seq, n_s, m_p, m)

    xs = jnp.pad(x_sample, ((0, 0), (0, SEQ_PAD - 1), (0, 0)))
    x = jnp.concatenate([x_prompt.reshape(m_p, d), xs.reshape(n_s * SEQ_PAD, d)], axis=0)

    cos_p, sin_p = _rope_tables(np.arange(seq))
    cos_s, sin_s = _rope_tables(PAST_LEN + np.arange(SEQ_PAD))
    log_gamma = np.log1p(-(2.0 ** (-5.0 - np.arange(H_B, dtype=np.float64))))
    heads = np.arange(1, N_GROUPS_C * G_C + 1, dtype=np.float64)
    tabs = {
        "cos_p": cos_p, "sin_p": sin_p, "cos_s": cos_s, "sin_s": sin_s,
        "lg": jnp.asarray(np.broadcast_to(log_gamma[:, None], (H_B, RET_ROWS)), F32),
        "slopes": jnp.asarray(2.0 ** (-8.0 * heads / (N_GROUPS_C * G_C)), F32),
    }

    assert w_in.shape[2] == N_MAIN + GATE_RANK
    w = {"w_in_t": jnp.swapaxes(w_in, 1, 2), "w_merge": w_merge, "w_up_a": w_up_a, "w_up_b": w_up_b, "w_up_c": w_up_c,
         "w_out": w_out, "w_ffn_gate": w_ffn_gate, "w_ffn_up": w_ffn_up, "w_ffn_down": w_ffn_down.astype(BF16)}
    caches = (cache_win0, cache_win1, cache_win2)
    outs = {k: [] for k in ("gla_p", "gla_s", "ret_p", "ret_s")}
    win_p = [[] for _ in range(N_GROUPS_C)]
    win_s = [[] for _ in range(N_GROUPS_C)]
    for l in range(depth):
        lw = {
            "g_mix": g_mix[l], "g_gla": g_gla[l], "g_ffn": g_ffn[l], "b_alpha": b_alpha[l],
            "w_alpha2": w_alpha2[l],
        }
        x, (sa_p, sa_s, sb_p, sb_s), wp, ws = _layer(x, l, w, lw, state_gla[l], state_ret[l], caches, tabs, dims)
        outs["gla_p"].append(sa_p)
        outs["gla_s"].append(sa_s)
        outs["ret_p"].append(sb_p)
        outs["ret_s"].append(sb_s)
        for g in range(N_GROUPS_C):
            win_p[g].append(wp[g])
            win_s[g].append(ws[g])

    y_prompt = _rmsnorm(x, g_final, F32, rows=m_p).reshape(n_p, seq, d)
    y_sample = _rmsnorm(x[m_p:], g_final, F32).reshape(n_s, SEQ_PAD, d)[:, :1]
    win_s = [_shift_windows(caches[g], jnp.stack(win_s[g])) for g in range(N_GROUPS_C)]
    return (y_prompt, y_sample, jnp.stack(outs["gla_p"]), jnp.stack(outs["gla_s"]), jnp.stack(outs["ret_p"]),
            jnp.stack(outs["ret_s"]), jnp.stack(win_p[0]), win_s[0], jnp.stack(win_p[1]), win_s[1],
            jnp.stack(win_p[2]), win_s[2])
```
